```python
import math
import jax, jax.numpy as jnp
from jax import lax
import numpy as np

D_MODEL = 1024
BATCH = 16
SEQ = 256
DEPTH = 2
DEC_BATCH = 8
DEC_SEQ = 2048
PAST_LEN = 256

GRID_W = 64
D_FF = 2816
N_MOD = 9
EPS = 1e-6
N_DIR = 2
CONV_W = 4
CHUNK = 64
SSM_HEADS = 8
SSM_HEAD_DIM = 64
SSM_INNER = SSM_HEADS * SSM_HEAD_DIM
SSM_GROUPS = 2
SSM_STATE = 64
SSM_CONV_DIM = SSM_INNER + 2 * SSM_GROUPS * SSM_STATE
ATTN_HEADS = 8
KV_HEADS = 2
HEAD_DIM = 64
ATTN_INNER = ATTN_HEADS * HEAD_DIM
ROPE_THETA = 10000.0
Q_BLOCK = 128
DN_HEADS = 8
DN_HEAD_DIM = 64
DN_INNER = DN_HEADS * DN_HEAD_DIM
LRU_BLOCKS = 8
LRU_BLOCK_DIM = 64
LRU_WIDTH = LRU_BLOCKS * LRU_BLOCK_DIM
LRU_C = 8.0
N_BRANCH = 4
BRANCH_W = 512
IN_SIZES = (SSM_INNER, SSM_CONV_DIM, N_DIR * SSM_HEADS,
            ATTN_INNER, KV_HEADS * HEAD_DIM, KV_HEADS * HEAD_DIM,
            DN_INNER, DN_INNER, DN_INNER, N_DIR * DN_HEADS, N_DIR * DN_HEADS, DN_INNER,
            LRU_WIDTH, LRU_WIDTH,
            N_BRANCH * D_MODEL)
N_IN = sum(IN_SIZES)

kernel_name = 'hybrid_prefix_diffusion_step'


def _rms_norm(x, g):
    xf = x.astype(jnp.float32)
    y = xf * lax.rsqrt(jnp.mean(xf * xf, axis=-1, keepdims=True) + EPS)
    return (y * g.astype(jnp.float32)).astype(x.dtype)


def _l2_norm(x):
    xf = x.astype(jnp.float32)
    return xf * lax.rsqrt(jnp.sum(xf * xf, axis=-1, keepdims=True) + EPS)


def _rev(z):
    return jnp.flip(z, axis=1)


def _adaln(cvec, w, b):
    return (jax.nn.silu(cvec) @ w + b).reshape(cvec.shape[0], N_MOD, D_MODEL)


def _swiglu(x, w13, w2):
    g, u = jnp.split(x @ w13, 2, axis=-1)
    return (jax.nn.silu(g) * u) @ w2


def _depthwise_conv(x, w, b):
    t = x.shape[1]
    lp = CONV_W // 2
    xp = jnp.pad(x, ((0, 0), (lp, CONV_W - 1 - lp), (0, 0)))
    out = b
    for j in range(CONV_W):
        out = out + w[j] * xp[:, j:j + t]
    return out


def _axial_rope(rows):
    n_freq = HEAD_DIM // 4
    inv = ROPE_THETA ** (-jnp.arange(n_freq, dtype=jnp.float32) / n_freq)
    row = jnp.repeat(jnp.arange(rows, dtype=jnp.float32), GRID_W)
    col = jnp.tile(jnp.arange(GRID_W, dtype=jnp.float32), rows)
    ang = jnp.stack([row[:, None] * inv, col[:, None] * inv], axis=1)
    return jnp.cos(ang), jnp.sin(ang)


def _apply_rope(x, cos, sin):
    b, t, h, d = x.shape
    xr = x.astype(jnp.float32).reshape(b, t, h, 2, 2, d // 4)
    x1, x2 = xr[..., 0, :], xr[..., 1, :]
    c = cos[None, :, None]
    s = sin[None, :, None]
    y = jnp.stack([x1 * c - x2 * s, x1 * s + x2 * c], axis=-2)
    return y.reshape(b, t, h, d).astype(x.dtype)


def _block_attention(q, k, v):
    b, t, h, d = q.shape
    g = h // KV_HEADS
    nb = t // Q_BLOCK
    qb = jnp.moveaxis(q.astype(jnp.float32).reshape(b, nb, Q_BLOCK, KV_HEADS, g, d), 1, 0)
    kf = k.astype(jnp.float32)
    vf = v.astype(jnp.float32)
    scale = d ** -0.5

    def attend(qblk):
        s = jnp.einsum('bqkgd,bskd->bkgqs', qblk, kf) * scale
        pr = jax.nn.softmax(s, axis=-1)
        return jnp.einsum('bkgqs,bskd->bqkgd', pr, vf)

    o = lax.map(attend, qb)
    return jnp.moveaxis(o, 0, 1).reshape(b, t, h, d)


def _ssd_scan(x, dt, a, bm, cm, h0):
    bsz, t, nh, hp = x.shape
    ns = bm.shape[-1]
    nc = t // CHUNK
    xc = x.astype(jnp.float32).reshape(bsz, nc, CHUNK, nh, hp)
    dtc = dt.astype(jnp.float32).reshape(bsz, nc, CHUNK, nh)
    bc = bm.astype(jnp.float32).reshape(bsz, nc, CHUNK, nh, ns)
    cc = cm.astype(jnp.float32).reshape(bsz, nc, CHUNK, nh, ns)
    acum = jnp.cumsum(dtc * a.astype(jnp.float32), axis=2)
    mask = jnp.tril(jnp.ones((CHUNK, CHUNK), dtype=bool))[None, None, :, :, None]
    seg = acum[:, :, :, None, :] - acum[:, :, None, :, :]
    decay = jnp.exp(jnp.where(mask, seg, -jnp.inf))
    scores = jnp.einsum('bcihn,bcjhn->bcijh', cc, bc) * decay * dtc[:, :, None, :, :]
    y_intra = jnp.einsum('bcijh,bcjhp->bcihp', scores, xc)
    w_end = jnp.exp(acum[:, :, -1:, :] - acum) * dtc
    chunk_states = jnp.einsum('bcjh,bcjhn,bcjhp->bchpn', w_end, bc, xc)
    chunk_decay = jnp.exp(acum[:, :, -1, :])

    def step(h, inp):
        s, d = inp
        return d[:, :, None, None] * h + s, h

    h_last, h_prev = lax.scan(step, h0.astype(jnp.float32),
                              (jnp.moveaxis(chunk_states, 1, 0), jnp.moveaxis(chunk_decay, 1, 0)))
    h_prev = jnp.moveaxis(h_prev, 0, 1)
    y_inter = jnp.einsum('bcihn,bchpn->bcihp', cc, h_prev) * jnp.exp(acum)[..., None]
    return (y_intra + y_inter).reshape(bsz, t, nh, hp), h_last


def _delta_scan(q, k, v, g, beta, s0):
    bsz, t, nh, dk = q.shape
    dv = v.shape[-1]
    nc = t // CHUNK

    def chunks(z):
        z = jnp.moveaxis(z.astype(jnp.float32), 2, 1)
        return z.reshape(bsz, nh, nc, CHUNK, *z.shape[3:])

    q = chunks(q) * dk ** -0.5
    k = chunks(k)
    v = chunks(v)
    g = chunks(g)
    beta = chunks(beta)
    gc = jnp.cumsum(g, axis=-1)
    incl = jnp.tril(jnp.ones((CHUNK, CHUNK), dtype=bool))
    strict = jnp.tril(jnp.ones((CHUNK, CHUNK), dtype=bool), k=-1)
    decay = jnp.exp(jnp.where(incl, gc[..., :, None] - gc[..., None, :], -jnp.inf))
    kb = k * beta[..., None]
    m = jnp.where(strict, jnp.einsum('bhcik,bhcjk->bhcij', kb, k) * decay, 0.0)
    eye = jnp.eye(CHUNK, dtype=jnp.float32)
    rhs = jnp.concatenate([v * beta[..., None], kb * jnp.exp(gc)[..., None]], axis=-1)
    sol = lax.linalg.triangular_solve(m + eye, rhs, left_side=True, lower=True, unit_diagonal=True)
    u, w = sol[..., :dv], sol[..., dv:]
    qk = jnp.where(incl, jnp.einsum('bhcik,bhcjk->bhcij', q, k) * decay, 0.0)
    qg = q * jnp.exp(gc)[..., None]
    kdec = k * jnp.exp(gc[..., -1:] - gc)[..., None]
    gl = jnp.exp(gc[..., -1])

    def step(s, inp):
        u_c, w_c, qk_c, qg_c, kd_c, gl_c = inp
        v_new = u_c - jnp.einsum('bhik,bhkv->bhiv', w_c, s)
        o = jnp.einsum('bhik,bhkv->bhiv', qg_c, s) + jnp.einsum('bhij,bhjv->bhiv', qk_c, v_new)
        s = s * gl_c[..., None, None] + jnp.einsum('bhjk,bhjv->bhkv', kd_c, v_new)
        return s, o

    xs = (jnp.moveaxis(u, 2, 0), jnp.moveaxis(w, 2, 0), jnp.moveaxis(qk, 2, 0),
          jnp.moveaxis(qg, 2, 0), jnp.moveaxis(kdec, 2, 0), jnp.moveaxis(gl, 2, 0))
    s_last, o = lax.scan(step, s0.astype(jnp.float32), xs)
    o = jnp.moveaxis(o, 0, 2).reshape(bsz, nh, t, dv)
    return jnp.moveaxis(o, 1, 2), s_last


def _linear_scan(a, b, h0):
    def comb(l, r):
        return (l[0] * r[0], r[0] * l[1] + r[1])

    acum, bcum = lax.associative_scan(comb, (a, b), axis=1)
    h = acum * h0.astype(jnp.float32)[:, None] + bcum
    return h, h[:, -1]


def _token_mix(xn, p, rope, ctx):
    bsz, t, _ = xn.shape
    f32 = jnp.float32
    proj = xn @ p['w_in']
    splits = np.cumsum(IN_SIZES)[:-1]
    (z_a, xbc_a, dt_a, q_b, k_b, v_b, q_c, k_c, v_c, beta_c, a_c, gate_c,
     x_d, y_d, gate_raw) = jnp.split(proj, splits, axis=-1)
    if ctx is None:
        ssm_f = jnp.zeros((bsz, SSM_HEADS, SSM_HEAD_DIM, SSM_STATE), f32)
        ssm_b = ssm_f
        dn_f = jnp.zeros((bsz, DN_HEADS, DN_HEAD_DIM, DN_HEAD_DIM), f32)
        dn_b = dn_f
        lru_f = jnp.zeros((bsz, LRU_WIDTH), f32)
        lru_b = lru_f
    else:
        ck, cv, ssm_f, ssm_b, dn_f, dn_b, lru_f, lru_b = ctx

    xbc = jax.nn.silu(_depthwise_conv(xbc_a, p['ssm_conv_w'], p['ssm_conv_b']))
    rep = SSM_HEADS // SSM_GROUPS
    xs = xbc[..., :SSM_INNER].reshape(bsz, t, SSM_HEADS, SSM_HEAD_DIM)
    gn = SSM_GROUPS * SSM_STATE
    bm = jnp.repeat(xbc[..., SSM_INNER:SSM_INNER + gn].reshape(bsz, t, SSM_GROUPS, SSM_STATE), rep, axis=2)
    cm = jnp.repeat(xbc[..., SSM_INNER + gn:].reshape(bsz, t, SSM_GROUPS, SSM_STATE), rep, axis=2)
    dt = jax.nn.softplus(dt_a.reshape(bsz, t, N_DIR, SSM_HEADS).astype(f32) + p['ssm_dt_bias'])
    a_ssm = -jnp.exp(p['ssm_a_log'].astype(f32))
    y_f, ssm_f_new = _ssd_scan(xs, dt[:, :, 0], a_ssm[0], bm, cm, ssm_f)
    y_bk, ssm_b_new = _ssd_scan(_rev(xs), _rev(dt[:, :, 1]), a_ssm[1], _rev(bm), _rev(cm), ssm_b)
    y = y_f + _rev(y_bk) + p['ssm_d'].astype(f32)[:, None] * xs
    o_a = _rms_norm(y.reshape(bsz, t, SSM_INNER) * jax.nn.silu(z_a.astype(f32)), p['ssm_norm'])

    q = _rms_norm(q_b.reshape(bsz, t, ATTN_HEADS, HEAD_DIM), p['attn_q_norm'])
    k = _rms_norm(k_b.reshape(bsz, t, KV_HEADS, HEAD_DIM), p['attn_k_norm'])
    v = v_b.reshape(bsz, t, KV_HEADS, HEAD_DIM)
    if ctx is None:
        o_b = _block_attention(q, k, v)
    else:
        cos, sin = rope
        q = _apply_rope(q, cos, sin)
        k_lat = _apply_rope(k, cos, sin)
        k_all = jnp.concatenate([ck.astype(k_lat.dtype), k_lat], axis=1)
        v_all = jnp.concatenate([cv.astype(v.dtype), v], axis=1)
        o_b = _block_attention(q, k_all, v_all)
    o_b = o_b.reshape(bsz, t, ATTN_INNER)

    qkv = jax.nn.silu(_depthwise_conv(jnp.concatenate([q_c, k_c, v_c], axis=-1), p['dn_conv_w'], p['dn_conv_b']))
    qd = _l2_norm(qkv[..., :DN_INNER].reshape(bsz, t, DN_HEADS, DN_HEAD_DIM))
    kd = _l2_norm(qkv[..., DN_INNER:2 * DN_INNER].reshape(bsz, t, DN_HEADS, DN_HEAD_DIM))
    vd = qkv[..., 2 * DN_INNER:].reshape(bsz, t, DN_HEADS, DN_HEAD_DIM)
    beta = jax.nn.sigmoid(beta_c.reshape(bsz, t, N_DIR, DN_HEADS).astype(f32))
    gdec = -jnp.exp(p['dn_a_log'].astype(f32)) * jax.nn.softplus(
        a_c.reshape(bsz, t, N_DIR, DN_HEADS).astype(f32) + p['dn_dt_bias'])
    o_f, dn_f_new = _delta_scan(qd, kd, vd, gdec[:, :, 0], beta[:, :, 0], dn_f)
    o_bk, dn_b_new = _delta_scan(_rev(qd), _rev(kd), _rev(vd), _rev(gdec[:, :, 1]), _rev(beta[:, :, 1]), dn_b)
    o_c = _rms_norm(o_f + _rev(o_bk), p['dn_norm']) * jax.nn.silu(
        gate_c.reshape(bsz, t, DN_HEADS, DN_HEAD_DIM).astype(f32))
    o_c = o_c.reshape(bsz, t, DN_INNER)

    xl = _depthwise_conv(x_d, p['lru_conv_w'], p['lru_conv_b']).astype(f32)
    xb = xl.reshape(bsz, t, LRU_BLOCKS, LRU_BLOCK_DIM)
    r = jax.nn.sigmoid(jnp.einsum('btkd,nkde->btnke', xb, p['lru_w_a'].astype(f32)).reshape(
        bsz, t, N_DIR, LRU_WIDTH) + p['lru_b_a'])
    ig = jax.nn.sigmoid(jnp.einsum('btkd,nkde->btnke', xb, p['lru_w_i'].astype(f32)).reshape(
        bsz, t, N_DIR, LRU_WIDTH) + p['lru_b_i'])
    log_a = -LRU_C * r * jax.nn.softplus(-p['lru_lambda'].astype(f32))
    a_l = jnp.exp(log_a)
    u_l = jnp.sqrt(-jnp.expm1(2.0 * log_a)) * ig * xl[:, :, None, :]
    h_f, lru_f_new = _linear_scan(a_l[:, :, 0], u_l[:, :, 0], lru_f)
    h_bk, lru_b_new = _linear_scan(_rev(a_l[:, :, 1]), _rev(u_l[:, :, 1]), lru_b)
    o_d = (h_f + _rev(h_bk)) * jax.nn.gelu(y_d.astype(f32))

    gates = jax.nn.sigmoid(gate_raw.reshape(bsz, t, N_BRANCH, D_MODEL))
    branches = (o_a, o_b, o_c, o_d)
    merged = gates[:, :, 0] * (branches[0].astype(xn.dtype) @ p['w_branch'][0])
    for n in range(1, N_BRANCH):
        merged = merged + gates[:, :, n] * (branches[n].astype(xn.dtype) @ p['w_branch'][n])
    out = (merged @ p['w_out']).astype(xn.dtype)
    if ctx is None:
        return out, (k, v, ssm_f_new, ssm_b_new, dn_f_new, dn_b_new, lru_f_new, lru_b_new)
    return out, None


def _trunk_layer(h, mod, p, rope, ctx):
    mod = mod.astype(h.dtype)
    m = [mod[:, i][:, None, :] for i in range(N_MOD)]
    xn = _rms_norm(h, p['norm_ffn1']) * (1.0 + m[1]) + m[0]
    h = h + 0.5 * m[2] * _swiglu(xn, p['ffn1_w13'], p['ffn1_w2'])
    xn = _rms_norm(h, p['norm_mix']) * (1.0 + m[4]) + m[3]
    mix, new_ctx = _token_mix(xn, p, rope, ctx)
    h = h + m[5] * mix
    xn = _rms_norm(h, p['norm_ffn2']) * (1.0 + m[7]) + m[6]
    h = h + 0.5 * m[8] * _swiglu(xn, p['ffn2_w13'], p['ffn2_w2'])
    return h, new_ctx


def setup_inputs(seed: int = 0) -> dict:
    key = jax.random.key(seed)
    keys = iter(jax.random.split(key, 64))
    f32 = jnp.float32

    def nrm(shape, scale):
        return scale * jax.random.normal(next(keys), shape, f32)

    def gain(shape):
        return 1.0 + nrm(shape, 0.01)

    def unif(shape, lo, hi):
        return jax.random.uniform(next(keys), shape, f32, lo, hi)

    def dt_bias(shape):
        dt0 = jnp.exp(unif(shape, math.log(1e-3), math.log(1e-1)))
        return dt0 + jnp.log(-jnp.expm1(-dt0))

    a_base = unif((DEPTH, N_DIR, LRU_WIDTH), 0.9, 0.999) ** (1.0 / LRU_C)
    lru_lambda = jnp.log(a_base) - jnp.log1p(-a_base)
    return {
        'x_prompt': nrm((BATCH, SEQ, D_MODEL), 1.0),
        'x_sample': nrm((DEC_BATCH, DEC_SEQ, D_MODEL), 1.0),
        'cache_k': nrm((DEC_BATCH, DEPTH, PAST_LEN, KV_HEADS, HEAD_DIM), 1.0),
        'cache_v': nrm((DEC_BATCH, DEPTH, PAST_LEN, KV_HEADS, HEAD_DIM), 1.0),
        'state_ssm': nrm((DEC_BATCH, DEPTH, N_DIR, SSM_HEADS, SSM_HEAD_DIM, SSM_STATE), 0.3),
        'state_delta': nrm((DEC_BATCH, DEPTH, N_DIR, DN_HEADS, DN_HEAD_DIM, DN_HEAD_DIM), 0.1),
        'state_lru': nrm((DEC_BATCH, DEPTH, N_DIR, LRU_WIDTH), 0.5),
        'c': nrm((DEC_BATCH, D_MODEL), 1.0),
        'c_ctx': nrm((D_MODEL,), 1.0),
        'w_ada': nrm((DEPTH, D_MODEL, N_MOD * D_MODEL), 0.5 * D_MODEL ** -0.5),
        'b_ada': nrm((DEPTH, N_MOD * D_MODEL), 0.01),
        'norm_ffn1': gain((DEPTH, D_MODEL)),
        'ffn1_w13': nrm((DEPTH, D_MODEL, 2 * D_FF), D_MODEL ** -0.5),
        'ffn1_w2': nrm((DEPTH, D_FF, D_MODEL), D_FF ** -0.5),
        'norm_mix': gain((DEPTH, D_MODEL)),
        'w_in': nrm((DEPTH, D_MODEL, N_IN), D_MODEL ** -0.5),
        'ssm_conv_w': nrm((DEPTH, CONV_W, SSM_CONV_DIM), 0.5),
        'ssm_conv_b': nrm((DEPTH, SSM_CONV_DIM), 0.01),
        'ssm_a_log': jnp.log(unif((DEPTH, N_DIR, SSM_HEADS), 1.0, 16.0)),
        'ssm_dt_bias': dt_bias((DEPTH, N_DIR, SSM_HEADS)),
        'ssm_d': gain((DEPTH, SSM_HEADS)),
        'ssm_norm': gain((DEPTH, SSM_INNER)),
        'attn_q_norm': gain((DEPTH, HEAD_DIM)),
        'attn_k_norm': gain((DEPTH, HEAD_DIM)),
        'dn_conv_w': nrm((DEPTH, CONV_W, 3 * DN_INNER), 0.5),
        'dn_conv_b': nrm((DEPTH, 3 * DN_INNER), 0.01),
        'dn_a_log': jnp.log(unif((DEPTH, N_DIR, DN_HEADS), 1.0, 16.0)),
        'dn_dt_bias': dt_bias((DEPTH, N_DIR, DN_HEADS)),
        'dn_norm': gain((DEPTH, DN_HEAD_DIM)),
        'lru_conv_w': nrm((DEPTH, CONV_W, LRU_WIDTH), 0.5),
        'lru_conv_b': nrm((DEPTH, LRU_WIDTH), 0.01),
        'lru_w_a': nrm((DEPTH, N_DIR, LRU_BLOCKS, LRU_BLOCK_DIM, LRU_BLOCK_DIM), LRU_BLOCK_DIM ** -0.5),
        'lru_b_a': nrm((DEPTH, N_DIR, LRU_WIDTH), 0.01),
        'lru_w_i': nrm((DEPTH, N_DIR, LRU_BLOCKS, LRU_BLOCK_DIM, LRU_BLOCK_DIM), LRU_BLOCK_DIM ** -0.5),
        'lru_b_i': nrm((DEPTH, N_DIR, LRU_WIDTH), 0.01),
        'lru_lambda': lru_lambda,
        'w_branch': nrm((DEPTH, N_BRANCH, BRANCH_W, D_MODEL), BRANCH_W ** -0.5),
        'w_out': nrm((DEPTH, D_MODEL, D_MODEL), D_MODEL ** -0.5),
        'norm_ffn2': gain((DEPTH, D_MODEL)),
        'ffn2_w13': nrm((DEPTH, D_MODEL, 2 * D_FF), D_MODEL ** -0.5),
        'ffn2_w2': nrm((DEPTH, D_FF, D_MODEL), D_FF ** -0.5),
        'final_norm': gain((D_MODEL,)),
    }


def reference(x_prompt, x_sample, cache_k, cache_v, state_ssm, state_delta, state_lru, c, c_ctx,
              w_ada, b_ada, norm_ffn1, ffn1_w13, ffn1_w2, norm_mix, w_in,
              ssm_conv_w, ssm_conv_b, ssm_a_log, ssm_dt_bias, ssm_d, ssm_norm,
              attn_q_norm, attn_k_norm,
              dn_conv_w, dn_conv_b, dn_a_log, dn_dt_bias, dn_norm,
              lru_conv_w, lru_conv_b, lru_w_a, lru_b_a, lru_w_i, lru_b_i, lru_lambda,
              w_branch, w_out, norm_ffn2, ffn2_w13, ffn2_w2, final_norm):
    rows = x_sample.shape[1] // GRID_W
    rope = _axial_rope(rows)
    hp = x_prompt
    hs = x_sample
    ks, vs, ssm_s, dn_s, lru_s = [], [], [], [], []
    for l in range(DEPTH):
        p = {
            'norm_ffn1': norm_ffn1[l], 'ffn1_w13': ffn1_w13[l], 'ffn1_w2': ffn1_w2[l],
            'norm_mix': norm_mix[l], 'w_in': w_in[l],
            'ssm_conv_w': ssm_conv_w[l], 'ssm_conv_b': ssm_conv_b[l], 'ssm_a_log': ssm_a_log[l],
            'ssm_dt_bias': ssm_dt_bias[l], 'ssm_d': ssm_d[l], 'ssm_norm': ssm_norm[l],
            'attn_q_norm': attn_q_norm[l], 'attn_k_norm': attn_k_norm[l],
            'dn_conv_w': dn_conv_w[l], 'dn_conv_b': dn_conv_b[l], 'dn_a_log': dn_a_log[l],
            'dn_dt_bias': dn_dt_bias[l], 'dn_norm': dn_norm[l],
            'lru_conv_w': lru_conv_w[l], 'lru_conv_b': lru_conv_b[l], 'lru_w_a': lru_w_a[l],
            'lru_b_a': lru_b_a[l], 'lru_w_i': lru_w_i[l], 'lru_b_i': lru_b_i[l], 'lru_lambda': lru_lambda[l],
            'w_branch': w_branch[l], 'w_out': w_out[l],
            'norm_ffn2': norm_ffn2[l], 'ffn2_w13': ffn2_w13[l], 'ffn2_w2': ffn2_w2[l],
        }
        mod_ctx = _adaln(c_ctx[None], w_ada[l], b_ada[l])
        mod_lat = _adaln(c, w_ada[l], b_ada[l])
        hp, st = _trunk_layer(hp, mod_ctx, p, None, None)
        k_c, v_c, sf, sb, df, db, lf, lb = st
        ks.append(k_c)
        vs.append(v_c)
        ssm_s.append(jnp.stack([sf, sb], axis=1))
        dn_s.append(jnp.stack([df, db], axis=1))
        lru_s.append(jnp.stack([lf, lb], axis=1))
        ctx_l = (cache_k[:, l], cache_v[:, l], state_ssm[:, l, 0], state_ssm[:, l, 1],
                 state_delta[:, l, 0], state_delta[:, l, 1], state_lru[:, l, 0], state_lru[:, l, 1])
        hs, _ = _trunk_layer(hs, mod_lat, p, rope, ctx_l)
    y_prompt = _rms_norm(hp, final_norm)
    y_sample = _rms_norm(hs, final_norm)
    new_k = jnp.stack(ks, axis=1)
    new_v = jnp.stack(vs, axis=1)
    new_ssm = jnp.stack(ssm_s, axis=1)
    new_delta = jnp.stack(dn_s, axis=1)
    new_lru = jnp.stack(lru_s, axis=1)
    return (y_prompt, y_sample, new_k, new_v, new_ssm, new_delta, new_lru)
```

```python
import functools
import math

import jax
import jax.numpy as jnp
from jax import lax
from jax.experimental import pallas as pl
from jax.experimental.pallas import tpu as pltpu

F32 = jnp.float32
BF16 = jnp.bfloat16

D_MODEL = 1024
D_FF = 2816
N_MOD = 9
MOD_ROWS = 16
EPS = 1e-6
GRID_W = 64
CONV_W = 4
CONV_LP = CONV_W // 2
HALO = 8
HEAD = 64
PAIR = 2 * HEAD
N_HEADS = 8
N_PAIRS = N_HEADS // 2
INNER = N_HEADS * HEAD
KV_HEADS = 2
SSM_GROUPS = 2
ROPE_THETA = 10000.0
LRU_C = 8.0
N_BRANCH = 4
SSD_CHUNK = 256
DN_CHUNK = 64
DN_BASE = 8
ROW_CHUNK = 256
ATT_TQ = 128
VMEM_LIMIT = 56 * 1024 * 1024

SSD_N = 1408
ATT_N = 1024
DN_N = 2176
LRU_N = 1024


def _dot(a, b):
    return jnp.dot(a.astype(BF16), b.astype(BF16), preferred_element_type=F32)


def _dot_nt(a, b):
    return lax.dot_general(a.astype(BF16), b.astype(BF16), (((1,), (1,)), ((), ())),
                           preferred_element_type=F32)


def _dot_tn(a, b):
    return lax.dot_general(a.astype(BF16), b.astype(BF16), (((0,), (0,)), ((), ())),
                           preferred_element_type=F32)


def _split3(x):
    hi = x.astype(BF16)
    r1 = x - hi.astype(F32)
    mid = r1.astype(BF16)
    lo = (r1 - mid.astype(F32)).astype(BF16)
    return hi, mid, lo


def _dot_sel(x, sel):
    hi, mid, lo = _split3(x)
    s = sel.astype(BF16)
    return (jnp.dot(hi, s, preferred_element_type=F32) + jnp.dot(mid, s, preferred_element_type=F32)
            + jnp.dot(lo, s, preferred_element_type=F32))


def _sel_dot(sel, x):
    hi, mid, lo = _split3(x)
    s = sel.astype(BF16)
    return (jnp.dot(s, hi, preferred_element_type=F32) + jnp.dot(s, mid, preferred_element_type=F32)
            + jnp.dot(s, lo, preferred_element_type=F32))


def _iota(shape, axis):
    return lax.broadcasted_iota(jnp.int32, shape, axis)


def _silu(x):
    return x * jax.nn.sigmoid(x)


def _softplus(x):
    return jnp.maximum(x, 0.0) + jnp.log1p(jnp.exp(-jnp.abs(x)))


def _rms(x, g):
    return x * lax.rsqrt(jnp.mean(x * x, axis=-1, keepdims=True) + EPS) * g


def _head_ones():
    return (_iota((PAIR, PAIR), 0) // HEAD == _iota((PAIR, PAIR), 1) // HEAD).astype(F32)


def _head_sumsq(x):
    return _dot_sel(x * x, _head_ones())


def _lo_tri(n):
    return (_iota((n, n), 1) <= _iota((n, n), 0)).astype(F32)


def _up_tri(n):
    return (_iota((n, n), 1) >= _iota((n, n), 0)).astype(F32)


def _conv_window(ref, r0, c, nchunks, rows, t_total):
    cur = ref[0, pl.ds(r0, rows), :]
    prev_start = pl.multiple_of(jnp.maximum(r0 - HALO, 0), HALO)
    next_start = pl.multiple_of(jnp.minimum(r0 + rows, t_total - HALO), HALO)
    prev = jnp.where(c > 0, ref[0, pl.ds(prev_start, HALO), :], 0.0)
    nxt = jnp.where(c < nchunks - 1, ref[0, pl.ds(next_start, HALO), :], 0.0)
    return jnp.concatenate([prev, cur, nxt], axis=0)


def _dwconv(ext, w, b, rows):
    out = b
    for j in range(CONV_W):
        off = HALO - CONV_LP + j
        out = out + w[j:j + 1, :] * ext[off:off + rows, :]
    return out


def _expand_heads(n_src_rows, first, width):
    r = _iota((n_src_rows, width), 0)
    l = _iota((n_src_rows, width), 1)
    return (r - first == l // HEAD).astype(F32)


def _full(shape):
    zeros = (0,) * len(shape)
    return pl.BlockSpec(shape, lambda *_: zeros, pipeline_mode=pl.Buffered(1))


def _params():
    return pltpu.CompilerParams(vmem_limit_bytes=VMEM_LIMIT)


def _adaln_body(c_ref, w_ref, b_ref, o_ref):
    o_ref[0] = _dot(_silu(c_ref[...]), w_ref[0]) + b_ref[0]


def _adaln(cvec, w_ada, b_ada):
    depth = w_ada.shape[0]
    n = w_ada.shape[2]
    tn = n // N_MOD
    return pl.pallas_call(
        _adaln_body,
        out_shape=jax.ShapeDtypeStruct((depth, MOD_ROWS, n), F32),
        grid=(depth, N_MOD),
        in_specs=[pl.BlockSpec((MOD_ROWS, D_MODEL), lambda l, j: (0, 0)),
                  pl.BlockSpec((1, D_MODEL, tn), lambda l, j: (l, 0, j)),
                  pl.BlockSpec((1, 1, tn), lambda l, j: (l, 0, j))],
        out_specs=pl.BlockSpec((1, MOD_ROWS, tn), lambda l, j: (l, 0, j)),
        compiler_params=_params(),
        name="adaln",
    )(cvec, w_ada, b_ada.reshape(depth, 1, n))


def _ffn_body(h_ref, mod_ref, gain_ref, wg_ref, wu_ref, w2_ref, fin_ref, o_ref, xn_sc, acc_sc,
              *, rows, n_ff, final):
    shift, scale, gate = rows
    f = pl.program_id(2)

    @pl.when(f == 0)
    def _():
        ms = mod_ref[0]
        xn = _rms(h_ref[0], gain_ref[...]) * (1.0 + ms[scale:scale + 1]) + ms[shift:shift + 1]
        xn_sc[...] = xn.astype(BF16)
        acc_sc[...] = jnp.zeros_like(acc_sc)

    xn = xn_sc[...]
    g = jnp.dot(xn, wg_ref[...], preferred_element_type=F32)
    u = jnp.dot(xn, wu_ref[...], preferred_element_type=F32)
    acc_sc[...] += _dot(_silu(g) * u, w2_ref[...])

    @pl.when(f == n_ff - 1)
    def _():
        ms = mod_ref[0]
        hn = h_ref[0] + 0.5 * ms[gate:gate + 1] * acc_sc[...]
        o_ref[0] = _rms(hn, fin_ref[...]) if final else hn


def _ffn(h, mod, gain, w13, w2, fin_gain, *, rows, final):
    bsz, t, d = h.shape
    tm = min(512, t)
    n_ff = 2
    tf = D_FF // n_ff
    mb = mod.shape[0]
    mod_ix = (lambda b, i, f: (b, 0, 0)) if mb > 1 else (lambda b, i, f: (0, 0, 0))
    return pl.pallas_call(
        functools.partial(_ffn_body, rows=rows, n_ff=n_ff, final=final),
        out_shape=jax.ShapeDtypeStruct((bsz, t, d), F32),
        grid=(bsz, t // tm, n_ff),
        in_specs=[pl.BlockSpec((1, tm, d), lambda b, i, f: (b, i, 0)),
                  pl.BlockSpec((1, MOD_ROWS, d), mod_ix),
                  pl.BlockSpec((1, d), lambda b, i, f: (0, 0)),
                  pl.BlockSpec((d, tf), lambda b, i, f: (0, f)),
                  pl.BlockSpec((d, tf), lambda b, i, f: (0, n_ff + f)),
                  pl.BlockSpec((tf, d), lambda b, i, f: (f, 0)),
                  pl.BlockSpec((1, d), lambda b, i, f: (0, 0))],
        out_specs=pl.BlockSpec((1, tm, d), lambda b, i, f: (b, i, 0)),
        scratch_shapes=[pltpu.VMEM((tm, d), BF16), pltpu.VMEM((tm, d), F32)],
        compiler_params=_params(),
        name="ffn",
    )(h, mod, gain, w13, w13, w2, fin_gain)


def _proj_body(h_ref, mod_ref, gain_ref, w_ref, *rest, has_t):
    ms = mod_ref[0]
    xn = (_rms(h_ref[0], gain_ref[...]) * (1.0 + ms[4:5]) + ms[3:4]).astype(BF16)
    if has_t:
        wt_ref, o_ref, ot_ref = rest
        ot_ref[0] = lax.dot_general(wt_ref[...], xn, (((1,), (1,)), ((), ())), preferred_element_type=F32)
    else:
        (o_ref,) = rest
    o_ref[0] = jnp.dot(xn, w_ref[...], preferred_element_type=F32)


def _proj(h, mod, gain, w, wt=None):
    bsz, t, d = h.shape
    n = w.shape[1]
    tm = min(512, t)
    mb = mod.shape[0]
    mod_ix = (lambda b, i: (b, 0, 0)) if mb > 1 else (lambda b, i: (0, 0, 0))
    in_specs = [pl.BlockSpec((1, tm, d), lambda b, i: (b, i, 0)),
                pl.BlockSpec((1, MOD_ROWS, d), mod_ix),
                pl.BlockSpec((1, d), lambda b, i: (0, 0)),
                _full((d, n))]
    out_shape = [jax.ShapeDtypeStruct((bsz, t, n), F32)]
    out_specs = [pl.BlockSpec((1, tm, n), lambda b, i: (b, i, 0))]
    args = [h, mod, gain, w]
    if wt is not None:
        r = wt.shape[0]
        in_specs.append(_full((r, d)))
        out_shape.append(jax.ShapeDtypeStruct((bsz, r, t), F32))
        out_specs.append(pl.BlockSpec((1, r, tm), lambda b, i: (b, 0, i)))
        args.append(wt)
    res = pl.pallas_call(
        functools.partial(_proj_body, has_t=wt is not None),
        out_shape=out_shape,
        grid=(bsz, t // tm),
        in_specs=in_specs,
        out_specs=out_specs,
        compiler_params=_params(),
        name="proj",
    )(*args)
    return res if wt is not None else res[0]


def _merge_body(h_ref, mod_ref, gain_ref, oa_ref, ob_ref, oc_ref, od_ref, wg_ref, wb_ref, wo_ref, o_ref):
    ms = mod_ref[0]
    h = h_ref[0]
    xn = (_rms(h, gain_ref[...]) * (1.0 + ms[4:5]) + ms[3:4]).astype(BF16)
    merged = None
    for n, br_ref in enumerate((oa_ref, ob_ref, oc_ref, od_ref)):
        gate = jax.nn.sigmoid(jnp.dot(xn, wg_ref[:, n * D_MODEL:(n + 1) * D_MODEL], preferred_element_type=F32))
        term = gate * jnp.dot(br_ref[0], wb_ref[n], preferred_element_type=F32)
        merged = term if merged is None else merged + term
    o_ref[0] = h + ms[5:6] * _dot(merged, wo_ref[...])


def _merge(h, mod, gain, branches, w_gate, w_branch, w_out):
    bsz, t, d = h.shape
    tm = min(512, t)
    mb = mod.shape[0]
    mod_ix = (lambda b, i: (b, 0, 0)) if mb > 1 else (lambda b, i: (0, 0, 0))
    br_spec = pl.BlockSpec((1, tm, INNER), lambda b, i: (b, i, 0))
    return pl.pallas_call(
        _merge_body,
        out_shape=jax.ShapeDtypeStruct((bsz, t, d), F32),
        grid=(bsz, t // tm),
        in_specs=[pl.BlockSpec((1, tm, d), lambda b, i: (b, i, 0)),
                  pl.BlockSpec((1, MOD_ROWS, d), mod_ix),
                  pl.BlockSpec((1, d), lambda b, i: (0, 0)),
                  br_spec, br_spec, br_spec, br_spec,
                  _full((d, N_BRANCH * d)), _full((N_BRANCH, INNER, d)), _full((d, d))],
        out_specs=pl.BlockSpec((1, tm, d), lambda b, i: (b, i, 0)),
        compiler_params=_params(),
        name="merge",
    )(h, mod, gain, *branches, w_gate, w_branch, w_out)


def _ssd_body(*refs, t, has_init, emit_state):
    (z_ref, xs_ref, bc_ref, dt_ref, dtT_ref, cwx_ref, cbx_ref, cwbc_ref, cbbc_ref,
     dtb_row_ref, dtb_col_ref, alog_row_ref, alog_col_ref, dskip_ref, gnorm_ref) = refs[:15]
    pos = 15
    h0_ref = None
    if has_init:
        h0_ref = refs[pos]
        pos += 1
    o_ref = refs[pos]
    pos += 1
    st_ref = None
    if emit_state:
        st_ref = refs[pos]
        pos += 1
    y_sc, cum_sc, cm_sc, dh_sc, tot_sc, hst_sc = refs[pos:]

    cl = SSD_CHUNK
    nc = t // cl
    lane = _iota((cl, PAIR), 1)
    ii = _iota((cl, cl), 0)
    jj = _iota((cl, cl), 1)
    lo = _lo_tri(cl)
    up = _up_tri(cl)
    hp = lax.Precision.HIGHEST
    a_row = -jnp.exp(alog_row_ref[...])
    a_col = -jnp.exp(alog_col_ref[...])
    sel_f = _expand_heads(PAIR, 0, INNER)
    sel_b = _expand_heads(PAIR, N_HEADS, INNER)

    def intra(c, carry):
        r0 = pl.multiple_of(c * cl, cl)
        xs = _silu(_dwconv(_conv_window(xs_ref, r0, c, nc, cl, t), cwx_ref[...], cbx_ref[...], cl))
        bc = _silu(_dwconv(_conv_window(bc_ref, r0, c, nc, cl, t), cwbc_ref[...], cbbc_ref[...], cl))
        bm = bc[:, :PAIR]
        cm = bc[:, PAIR:]
        dt = _softplus(dt_ref[0, pl.ds(r0, cl), :] + dtb_row_ref[...])
        dtT = _softplus(dtT_ref[0, :, pl.ds(r0, cl)] + dtb_col_ref[...])
        da = dt * a_row
        daT = dtT * a_col
        cum = jnp.where(lane < N_HEADS, jnp.dot(lo, da, precision=hp, preferred_element_type=F32),
                        jnp.dot(up, da, precision=hp, preferred_element_type=F32))
        rowsel = _iota((2 * N_HEADS, cl), 0) < N_HEADS
        cumT = jnp.where(rowsel, jnp.dot(daT, up, precision=hp, preferred_element_type=F32),
                         jnp.dot(daT, lo, precision=hp, preferred_element_type=F32))
        cum_sc[pl.ds(r0, cl), :] = cum
        cm_sc[pl.ds(r0, cl), :] = cm
        cb = []
        for g in range(SSM_GROUPS):
            cg = jnp.where(lane // HEAD == g, cm, 0.0)
            cb.append(_dot_nt(cg, bm))
        ypairs = []
        for p in range(N_PAIRS):
            xp = xs[:, p * PAIR:(p + 1) * PAIR].astype(BF16)
            halves = []
            for a in range(2):
                h = 2 * p + a
                g = h // (N_HEADS // SSM_GROUPS)
                hb = N_HEADS + h
                sf = jnp.exp(jnp.where(jj <= ii, cum[:, h:h + 1] - cumT[h:h + 1, :], -jnp.inf)) * dtT[h:h + 1, :]
                sb = jnp.exp(jnp.where(jj >= ii, cum[:, hb:hb + 1] - cumT[hb:hb + 1, :], -jnp.inf)) * dtT[hb:hb + 1, :]
                s = (cb[g] * (sf + sb)).astype(BF16)
                halves.append(jnp.dot(s, xp, preferred_element_type=F32))
            ypairs.append(jnp.where(lane < HEAD, halves[0], halves[1]))
        y = jnp.concatenate(ypairs, axis=1) + dskip_ref[...] * xs
        y_sc[pl.ds(r0, cl), :] = y
        tot = jnp.where(lane[0:1] < N_HEADS, cum[cl - 1:cl, :], cum[0:1, :])
        tot_sc[c] = jnp.broadcast_to(tot, (HALO, PAIR))
        wexp = jnp.exp(tot - cum) * dt
        for d, sel in enumerate((sel_f, sel_b)):
            xw = xs * _dot_sel(wexp, sel)
            for p in range(N_PAIRS):
                g = p // (N_PAIRS // SSM_GROUPS)
                bg = jnp.where(lane // HEAD == g, bm, 0.0)
                dh_sc[c, d, p] = _dot_tn(xw[:, p * PAIR:(p + 1) * PAIR], bg)
        return carry

    lax.fori_loop(0, nc, intra, 0)

    rr = _iota((PAIR, PAIR), 0)
    for d in range(2):
        sel = sel_f if d == 0 else sel_b
        for p in range(N_PAIRS):
            g = p // (N_PAIRS // SSM_GROUPS)
            if has_init:
                blk = h0_ref[0, d, p * PAIR:(p + 1) * PAIR, :]
                z64 = jnp.zeros_like(blk)
                hst_sc[p] = jnp.concatenate([blk, z64] if g == 0 else [z64, blk], axis=1)
            else:
                hst_sc[p] = jnp.zeros((PAIR, PAIR), F32)

        def inter(k, carry, d=d, sel=sel):
            c = k if d == 0 else nc - 1 - k
            r0 = pl.multiple_of(c * cl, cl)
            ecum = _dot_sel(jnp.exp(cum_sc[pl.ds(r0, cl), :]), sel)
            cm = cm_sc[pl.ds(r0, cl), :]
            tot = tot_sc[c]
            dec = jnp.exp(tot[0:1, :])
            for p in range(N_PAIRS):
                g = p // (N_PAIRS // SSM_GROUPS)
                cg = jnp.where(lane // HEAD == g, cm, 0.0)
                hs = hst_sc[p]
                yi = _dot_nt(cg, hs)
                cols = pl.ds(p * PAIR, PAIR)
                y_sc[pl.ds(r0, cl), cols] = y_sc[pl.ds(r0, cl), cols] + yi * ecum[:, p * PAIR:(p + 1) * PAIR]
                h0i = d * N_HEADS + 2 * p
                dcol = jnp.where(rr < HEAD, dec[:, h0i:h0i + 1], dec[:, h0i + 1:h0i + 2])
                hst_sc[p] = hs * dcol + dh_sc[c, d, p]
            return carry

        lax.fori_loop(0, nc, inter, 0)
        if emit_state:
            for p in range(N_PAIRS):
                g = p // (N_PAIRS // SSM_GROUPS)
                st_ref[0, d, p * PAIR:(p + 1) * PAIR, :] = hst_sc[p][:, g * HEAD:(g + 1) * HEAD]

    def finish(c, carry):
        r0 = pl.multiple_of(c * cl, cl)
        v = y_sc[pl.ds(r0, cl), :] * _silu(z_ref[0, pl.ds(r0, cl), :])
        o_ref[0, pl.ds(r0, cl), :] = _rms(v, gnorm_ref[...]).astype(BF16)
        return carry

    lax.fori_loop(0, nc, finish, 0)


def _ssd(p_ssd, dtT, prm, h0, *, emit_state):
    bsz, t, _ = p_ssd.shape
    nc = t // SSD_CHUNK
    has_init = h0 is not None
    seq = lambda w, j: pl.BlockSpec((1, t, w), lambda b: (b, 0, j), pipeline_mode=pl.Buffered(1))
    in_specs = [seq(INNER, 0), seq(INNER, 1), seq(2 * PAIR, 4), seq(PAIR, 10),
                pl.BlockSpec((1, 2 * N_HEADS, t), lambda b: (b, 0, 0), pipeline_mode=pl.Buffered(1))]
    args = [p_ssd, p_ssd, p_ssd, p_ssd, dtT]
    for name in ("cwx", "cbx", "cwbc", "cbbc", "dtb_row", "dtb_col", "alog_row", "alog_col", "dskip", "gnorm"):
        a = prm[name]
        in_specs.append(_full(a.shape))
        args.append(a)
    if has_init:
        in_specs.append(pl.BlockSpec((1, 2, INNER, HEAD), lambda b: (b, 0, 0, 0)))
        args.append(h0)
    out_shape = [jax.ShapeDtypeStruct((bsz, t, INNER), BF16)]
    out_specs = [pl.BlockSpec((1, t, INNER), lambda b: (b, 0, 0))]
    if emit_state:
        out_shape.append(jax.ShapeDtypeStruct((bsz, 2, INNER, HEAD), F32))
        out_specs.append(pl.BlockSpec((1, 2, INNER, HEAD), lambda b: (b, 0, 0, 0)))
    res = pl.pallas_call(
        functools.partial(_ssd_body, t=t, has_init=has_init, emit_state=emit_state),
        out_shape=out_shape,
        grid=(bsz,),
        in_specs=in_specs,
        out_specs=out_specs,
        scratch_shapes=[pltpu.VMEM((t, INNER), F32), pltpu.VMEM((t, PAIR), F32), pltpu.VMEM((t, PAIR), F32),
                        pltpu.VMEM((nc, 2, N_PAIRS, PAIR, PAIR), F32), pltpu.VMEM((nc, HALO, PAIR), F32),
                        pltpu.VMEM((N_PAIRS, PAIR, PAIR), F32)],
        compiler_params=_params(),
        name="ssd",
    )(*args)
    return (res[0], res[1]) if emit_state else (res[0], None)


def _attn_body(*refs, t, s_ctx, rope):
    q_ref, kk_ref, vv_ref, qg_ref, kg_ref = refs[:5]
    pos = 5
    if rope:
        cos_ref, sin_ref, ck_ref, cv_ref = refs[pos:pos + 4]
        pos += 4
    o_ref = refs[pos]
    pos += 1
    if not rope:
        knew_ref, vnew_ref = refs[pos:pos + 2]
        pos += 2
    q_sc, k_sc, v_sc = refs[pos:]

    rc = ROW_CHUNK
    lane = _iota((rc, PAIR), 1)
    first_half = (lane % (HEAD // 2)) < HEAD // 4

    def rot(x, cos, sin):
        partner = jnp.where(first_half, pltpu.roll(x, PAIR - HEAD // 4, 1), pltpu.roll(x, HEAD // 4, 1))
        return x * cos + partner * sin

    def normed(x, gain):
        return x * lax.rsqrt(_head_sumsq(x) * (1.0 / HEAD) + EPS) * gain

    if rope:
        k_sc[0:s_ctx, :] = ck_ref[0].astype(BF16)
        v_sc[0:s_ctx, :] = cv_ref[0].astype(BF16)

    def prep(c, carry):
        r0 = pl.multiple_of(c * rc, rc)
        if rope:
            cos = cos_ref[pl.ds(r0, rc), :]
            sin = sin_ref[pl.ds(r0, rc), :]
        for p in range(N_PAIRS):
            x = normed(q_ref[0, pl.ds(r0, rc), p * PAIR:(p + 1) * PAIR], qg_ref[...])
            if rope:
                x = rot(x, cos, sin)
            q_sc[pl.ds(r0, rc), p * PAIR:(p + 1) * PAIR] = (x * (HEAD ** -0.5)).astype(BF16)
        kn = []
        for g in range(KV_HEADS):
            x = normed(kk_ref[0, pl.ds(r0, rc), g * PAIR:(g + 1) * PAIR], kg_ref[...])
            kn.append(x)
            if rope:
                x = rot(x, cos, sin)
            k_sc[pl.ds(s_ctx + r0, rc), g * PAIR:(g + 1) * PAIR] = x.astype(BF16)
        vv = vv_ref[0, pl.ds(r0, rc), :]
        v_sc[pl.ds(s_ctx + r0, rc), :] = vv.astype(BF16)
        if not rope:
            knew_ref[0, pl.ds(r0, rc), :] = jnp.where(lane < HEAD, kn[0], kn[1])
            vnew_ref[0, pl.ds(r0, rc), :] = jnp.where(lane < HEAD, vv[:, :PAIR], vv[:, PAIR:])
        return carry

    lax.fori_loop(0, t // rc, prep, 0)

    tq = ATT_TQ
    lane_q = _iota((tq, PAIR), 1)

    def attend(i, carry):
        r0 = pl.multiple_of(i * tq, tq)
        for p in range(N_PAIRS):
            g = p // (N_PAIRS // KV_HEADS)
            qp = q_sc[pl.ds(r0, tq), p * PAIR:(p + 1) * PAIR]
            kg = k_sc[:, g * PAIR:(g + 1) * PAIR]
            vg = v_sc[:, g * PAIR:(g + 1) * PAIR]
            halves = []
            for a in range(2):
                qa = jnp.where(lane_q // HEAD == a, qp, jnp.zeros_like(qp))
                s = lax.dot_general(qa, kg, (((1,), (1,)), ((), ())), preferred_element_type=F32)
                e = jnp.exp(s - jnp.max(s, axis=-1, keepdims=True))
                l = jnp.sum(e, axis=-1, keepdims=True)
                halves.append(jnp.dot(e.astype(BF16), vg, preferred_element_type=F32) / l)
            o_ref[0, pl.ds(r0, tq), p * PAIR:(p + 1) * PAIR] = jnp.where(lane_q < HEAD, halves[0], halves[1]).astype(BF16)
        return carry

    lax.fori_loop(0, t // tq, attend, 0)


def _attn(p_att, prm, rope_tabs, ctx_kv):
    bsz, t, _ = p_att.shape
    rope = rope_tabs is not None
    s_ctx = ctx_kv[0].shape[1] if rope else 0
    s_all = s_ctx + t
    seq = lambda w, j: pl.BlockSpec((1, t, w), lambda b: (b, 0, j), pipeline_mode=pl.Buffered(1))
    in_specs = [seq(INNER, 0), seq(2 * PAIR, 2), seq(2 * PAIR, 3), _full((1, PAIR)), _full((1, PAIR))]
    args = [p_att, p_att, p_att, prm["qgain"], prm["kgain"]]
    if rope:
        in_specs += [_full((t, PAIR)), _full((t, PAIR)),
                     pl.BlockSpec((1, s_ctx, 2 * PAIR), lambda b: (b, 0, 0)),
                     pl.BlockSpec((1, s_ctx, 2 * PAIR), lambda b: (b, 0, 0))]
        args += [rope_tabs[0], rope_tabs[1], ctx_kv[0], ctx_kv[1]]
    out_shape = [jax.ShapeDtypeStruct((bsz, t, INNER), BF16)]
    out_specs = [pl.BlockSpec((1, t, INNER), lambda b: (b, 0, 0))]
    if not rope:
        out_shape += [jax.ShapeDtypeStruct((bsz, t, PAIR), F32)] * 2
        out_specs += [pl.BlockSpec((1, t, PAIR), lambda b: (b, 0, 0))] * 2
    res = pl.pallas_call(
        functools.partial(_attn_body, t=t, s_ctx=s_ctx, rope=rope),
        out_shape=out_shape,
        grid=(bsz,),
        in_specs=in_specs,
        out_specs=out_specs,
        scratch_shapes=[pltpu.VMEM((t, INNER), BF16), pltpu.VMEM((s_all, 2 * PAIR), BF16),
                        pltpu.VMEM((s_all, 2 * PAIR), BF16)],
        compiler_params=_params(),
        name="attn",
    )(*args)
    return res


def _dn_body(*refs, t, has_init, emit_state):
    (q_ref, k_ref, v_ref, gate_ref, ba_ref, baT_ref, cwq_ref, cwk_ref, cwv_ref, cbq_ref, cbk_ref, cbv_ref,
     alog_row_ref, bias_row_ref, alog_col_ref, bias_col_ref, gnorm_ref) = refs[:17]
    pos = 17
    s0_ref = None
    if has_init:
        s0_ref = refs[pos]
        pos += 1
    o_ref = refs[pos]
    pos += 1
    st_ref = None
    if emit_state:
        st_ref = refs[pos]
        pos += 1
    qn_sc, kn_sc, vc_sc, bg_sc, bgT_sc, of_sc, ob_sc, s_sc = refs[pos:]

    rc = ROW_CHUNK
    cl = DN_CHUNK
    nc = t // cl
    nrc = t // rc
    hp = lax.Precision.HIGHEST
    nb = 2 * N_HEADS

    lane_rc = _iota((rc, PAIR), 1)
    row_t = _iota((2 * nb, rc), 0)

    def prep(c, carry):
        r0 = pl.multiple_of(c * rc, rc)
        for src, cw, cb, dst, norm in ((q_ref, cwq_ref, cbq_ref, qn_sc, True), (k_ref, cwk_ref, cbk_ref, kn_sc, True),
                                       (v_ref, cwv_ref, cbv_ref, vc_sc, False)):
            x = _silu(_dwconv(_conv_window(src, r0, c, nrc, rc, t), cw[...], cb[...], rc))
            if norm:
                tiles = []
                for p in range(N_PAIRS):
                    xp = x[:, p * PAIR:(p + 1) * PAIR]
                    tiles.append(xp * lax.rsqrt(_head_sumsq(xp) + EPS))
                x = jnp.concatenate(tiles, axis=1)
                if dst is qn_sc:
                    x = x * (HEAD ** -0.5)
            dst[pl.ds(r0, rc), :] = x
        ba = ba_ref[0, pl.ds(r0, rc), :]
        beta = jax.nn.sigmoid(ba)
        gdec = -jnp.exp(alog_row_ref[...]) * _softplus(ba + bias_row_ref[...])
        bg_sc[pl.ds(r0, rc), :] = jnp.where(lane_rc < nb, beta, gdec)
        baT = baT_ref[0, :, pl.ds(r0, rc)]
        betaT = jax.nn.sigmoid(baT)
        gdecT = -jnp.exp(alog_col_ref[...]) * _softplus(baT + bias_col_ref[...])
        bgT = jnp.where(row_t < nb, betaT, gdecT)
        for k in range(rc // cl):
            bgT_sc[c * (rc // cl) + k] = bgT[:, k * cl:(k + 1) * cl]
        return carry

    lax.fori_loop(0, nrc, prep, 0)

    rr = _iota((PAIR, PAIR), 0)
    cc = _iota((PAIR, PAIR), 1)
    same = (rr // HEAD) == (cc // HEAD)
    ii = rr % HEAD
    jj = cc % HEAD
    lane_c = _iota((cl, PAIR), 1)
    lo = _lo_tri(cl)
    up = _up_tri(cl)
    eye = (rr == cc).astype(F32)
    blk_base = same & (ii // DN_BASE == jj // DN_BASE)
    off_masks = []
    b = DN_BASE
    while b < cl:
        off_masks.append(same & (ii // (2 * b) == jj // (2 * b)) & (ii // b != jj // b))
        b *= 2

    for d in range(2):
        for p in range(N_PAIRS):
            if has_init:
                b0 = s0_ref[0, d, (2 * p) * HEAD:(2 * p + 1) * HEAD, :]
                b1 = s0_ref[0, d, (2 * p + 1) * HEAD:(2 * p + 2) * HEAD, :]
                z64 = jnp.zeros_like(b0)
                s_sc[d, p] = jnp.concatenate([jnp.concatenate([b0, z64], axis=1),
                                              jnp.concatenate([z64, b1], axis=1)], axis=0)
            else:
                s_sc[d, p] = jnp.zeros((PAIR, PAIR), F32)

    def stack2(x):
        return jnp.concatenate([jnp.where(lane_c < HEAD, x, 0.0), jnp.where(lane_c >= HEAD, x, 0.0)], axis=0)

    def col2(m, c0):
        return jnp.concatenate([m[:, c0:c0 + 1], m[:, c0 + 1:c0 + 2]], axis=0)

    def chunk(k, carry):
        for d in range(2):
            c = k if d == 0 else nc - 1 - k
            r0 = pl.multiple_of(c * cl, cl)
            bg = bg_sc[pl.ds(r0, cl), :]
            bgT = bgT_sc[c]
            tri = lo if d == 0 else up
            triT = up if d == 0 else lo
            gc = jnp.dot(tri, bg, precision=hp, preferred_element_type=F32)
            gcT = jnp.dot(bgT, triT, precision=hp, preferred_element_type=F32)
            last = cl - 1 if d == 0 else 0
            incl = (jj <= ii) if d == 0 else (jj >= ii)
            dst = of_sc if d == 0 else ob_sc
            for p in range(N_PAIRS):
                h0i = d * N_HEADS + 2 * p
                cbeta = col2(bg, h0i)
                cgc = col2(gc, nb + h0i)
                rgc = jnp.concatenate([gcT[nb + h0i:nb + h0i + 1, :], gcT[nb + h0i + 1:nb + h0i + 2, :]], axis=1)
                glast = jnp.where(rr[:, 0:1] < HEAD, gc[last:last + 1, nb + h0i:nb + h0i + 1],
                                  gc[last:last + 1, nb + h0i + 1:nb + h0i + 2])
                decay = jnp.exp(jnp.where(same & incl, cgc - rgc, -jnp.inf))
                cols = pl.ds(p * PAIR, PAIR)
                ks = stack2(kn_sc[pl.ds(r0, cl), cols])
                qs = stack2(qn_sc[pl.ds(r0, cl), cols])
                vs = stack2(vc_sc[pl.ds(r0, cl), cols])
                ksb = ks.astype(BF16)
                gm = _dot_nt(ksb, ksb)
                am = _dot_nt(qs, ksb)
                m = gm * cbeta * decay * (ii != jj).astype(F32)
                aq = am * decay
                egc = jnp.exp(cgc)
                rhs = jnp.concatenate([vs * cbeta, ks * (cbeta * egc)], axis=1)
                n1 = jnp.where(blk_base, m, 0.0)
                n2 = _dot(n1, n1)
                n4 = _dot(n2, n2)
                pm = eye + n2 + n4 + _dot(n2, n4)
                x = pm - _dot(n1, pm)
                for off_mask in off_masks:
                    x = x - _dot(x, _dot(jnp.where(off_mask, m, 0.0), x))
                sol = _dot(x, rhs)
                u = sol[:, :PAIR]
                w = sol[:, PAIR:]
                s = s_sc[d, p]
                vnew = u - _dot(w, s)
                ost = _dot(qs * egc, s) + _dot(aq, vnew)
                dst[pl.ds(r0, cl), cols] = ost[:cl] + ost[cl:]
                kd = ks * jnp.exp(glast - cgc)
                s_sc[d, p] = s * jnp.exp(glast) + _dot_tn(kd, vnew)
        return carry

    lax.fori_loop(0, nc, chunk, 0)

    if emit_state:
        for d in range(2):
            for p in range(N_PAIRS):
                s = s_sc[d, p]
                st_ref[0, d, (2 * p) * HEAD:(2 * p + 1) * HEAD, :] = s[:HEAD, :HEAD]
                st_ref[0, d, (2 * p + 1) * HEAD:(2 * p + 2) * HEAD, :] = s[HEAD:, HEAD:]

    def finish(c, carry):
        r0 = pl.multiple_of(c * rc, rc)
        for p in range(N_PAIRS):
            cols = pl.ds(p * PAIR, PAIR)
            o = of_sc[pl.ds(r0, rc), cols] + ob_sc[pl.ds(r0, rc), cols]
            on = o * lax.rsqrt(_head_sumsq(o) * (1.0 / HEAD) + EPS) * gnorm_ref[...]
            o_ref[0, pl.ds(r0, rc), cols] = (on * _silu(gate_ref[0, pl.ds(r0, rc), cols])).astype(BF16)
        return carry

    lax.fori_loop(0, nrc, finish, 0)


def _dn(p_dn, baT, prm, s0, *, emit_state):
    bsz, t, _ = p_dn.shape
    nc = t // DN_CHUNK
    has_init = s0 is not None
    seq = lambda w, j: pl.BlockSpec((1, t, w), lambda b: (b, 0, j), pipeline_mode=pl.Buffered(1))
    in_specs = [seq(INNER, 0), seq(INNER, 1), seq(INNER, 2), seq(INNER, 3), seq(PAIR, 16),
                pl.BlockSpec((1, 4 * N_HEADS, t), lambda b: (b, 0, 0), pipeline_mode=pl.Buffered(1))]
    args = [p_dn] * 5 + [baT]
    for name in ("cwq", "cwk", "cwv", "cbq", "cbk", "cbv", "alog_row", "bias_row", "alog_col", "bias_col", "gnorm"):
        a = prm[name]
        in_specs.append(_full(a.shape))
        args.append(a)
    if has_init:
        in_specs.append(pl.BlockSpec((1, 2, INNER, HEAD), lambda b: (b, 0, 0, 0)))
        args.append(s0)
    out_shape = [jax.ShapeDtypeStruct((bsz, t, INNER), BF16)]
    out_specs = [pl.BlockSpec((1, t, INNER), lambda b: (b, 0, 0))]
    if emit_state:
        out_shape.append(jax.ShapeDtypeStruct((bsz, 2, INNER, HEAD), F32))
        out_specs.append(pl.BlockSpec((1, 2, INNER, HEAD), lambda b: (b, 0, 0, 0)))
    res = pl.pallas_call(
        functools.partial(_dn_body, t=t, has_init=has_init, emit_state=emit_state),
        out_shape=out_shape,
        grid=(bsz,),
        in_specs=in_specs,
        out_specs=out_specs,
        scratch_shapes=[pltpu.VMEM((t, INNER), F32), pltpu.VMEM((t, INNER), F32), pltpu.VMEM((t, INNER), F32),
                        pltpu.VMEM((t, PAIR), F32), pltpu.VMEM((nc, 4 * N_HEADS, DN_CHUNK), F32),
                        pltpu.VMEM((t, INNER), F32), pltpu.VMEM((t, INNER), F32),
                        pltpu.VMEM((2, N_PAIRS, PAIR, PAIR), F32)],
        compiler_params=_params(),
        name="deltanet",
    )(*args)
    return (res[0], res[1]) if emit_state else (res[0], None)


def _lru_body(*refs, t, has_init, emit_state):
    x_ref, y_ref, cw_ref, cb_ref, wbd_ref, bias_ref, lam_ref = refs[:7]
    pos = 7
    h0_ref = None
    if has_init:
        h0_ref = refs[pos]
        pos += 1
    o_ref = refs[pos]
    pos += 1
    st_ref = None
    if emit_state:
        st_ref = refs[pos]
        pos += 1
    xl_sc, af_sc, uf_sc, ab_sc, ub_sc, hf_sc, hb_sc = refs[pos:]

    rc = ROW_CHUNK
    nrc = t // rc
    row8 = _iota((HALO, PAIR), 0)

    def conv(c, carry):
        r0 = pl.multiple_of(c * rc, rc)
        xl_sc[pl.ds(r0, rc), :] = _dwconv(_conv_window(x_ref, r0, c, nrc, rc, t), cw_ref[...], cb_ref[...], rc)
        return carry

    lax.fori_loop(0, nrc, conv, 0)

    for p in range(N_PAIRS):
        cols = pl.ds(p * PAIR, PAIR)
        sp_lam = _softplus(-lam_ref[:, p * PAIR:(p + 1) * PAIR])

        def gates(c, carry, p=p, cols=cols, sp_lam=sp_lam):
            r0 = pl.multiple_of(c * rc, rc)
            xl = xl_sc[pl.ds(r0, rc), cols]
            pre = _dot(xl, wbd_ref[p]) + bias_ref[p]
            for d, (a_sc, u_sc) in enumerate(((af_sc, uf_sc), (ab_sc, ub_sc))):
                r = jax.nn.sigmoid(pre[:, d * PAIR:(d + 1) * PAIR])
                ig = jax.nn.sigmoid(pre[:, (2 + d) * PAIR:(3 + d) * PAIR])
                log_a = -LRU_C * r * sp_lam[d:d + 1, :]
                a_sc[pl.ds(r0, rc), :] = jnp.exp(log_a)
                u_sc[pl.ds(r0, rc), :] = jnp.sqrt(-jnp.tanh(log_a) * (jnp.exp(2.0 * log_a) + 1.0)) * ig * xl
            return carry

        lax.fori_loop(0, nrc, gates, 0)

        if has_init:
            cf0 = h0_ref[0, 0:1, p * PAIR:(p + 1) * PAIR]
            cb0 = h0_ref[0, 1:2, p * PAIR:(p + 1) * PAIR]
        else:
            cf0 = jnp.zeros((1, PAIR), F32)
            cb0 = cf0

        def scan(k, carry):
            cf, cb = carry
            rf = pl.multiple_of(k * HALO, HALO)
            a8 = af_sc[pl.ds(rf, HALO), :]
            b8 = uf_sc[pl.ds(rf, HALO), :]
            for s in (1, 2, 4):
                ok = row8 >= s
                b8 = jnp.where(ok, a8 * pltpu.roll(b8, s, 0) + b8, b8)
                a8 = jnp.where(ok, a8 * pltpu.roll(a8, s, 0), a8)
            hf = a8 * cf + b8
            hf_sc[pl.ds(rf, HALO), :] = hf
            rb = pl.multiple_of(t - HALO - k * HALO, HALO)
            a8 = ab_sc[pl.ds(rb, HALO), :]
            b8 = ub_sc[pl.ds(rb, HALO), :]
            for s in (1, 2, 4):
                ok = row8 < HALO - s
                b8 = jnp.where(ok, a8 * pltpu.roll(b8, HALO - s, 0) + b8, b8)
                a8 = jnp.where(ok, a8 * pltpu.roll(a8, HALO - s, 0), a8)
            hb = a8 * cb + b8
            hb_sc[pl.ds(rb, HALO), :] = hb
            return hf[HALO - 1:HALO, :], hb[0:1, :]

        cf, cb = lax.fori_loop(0, t // HALO, scan, (cf0, cb0))
        if emit_state:
            st_ref[0, 0:1, p * PAIR:(p + 1) * PAIR] = cf
            st_ref[0, 1:2, p * PAIR:(p + 1) * PAIR] = cb

        def finish(c, carry, cols=cols):
            r0 = pl.multiple_of(c * rc, rc)
            y = y_ref[0, pl.ds(r0, rc), cols]
            gelu = 0.5 * y * (1.0 + jnp.tanh(math.sqrt(2.0 / math.pi) * (y + 0.044715 * (y * y * y))))
            o_ref[0, pl.ds(r0, rc), cols] = ((hf_sc[pl.ds(r0, rc), :] + hb_sc[pl.ds(r0, rc), :]) * gelu).astype(BF16)
            return carry

        lax.fori_loop(0, nrc, finish, 0)


def _lru(p_lru, prm, h0, *, emit_state):
    bsz, t, _ = p_lru.shape
    has_init = h0 is not None
    seq = lambda w, j: pl.BlockSpec((1, t, w), lambda b: (b, 0, j), pipeline_mode=pl.Buffered(1))
    in_specs = [seq(INNER, 0), seq(INNER, 1)]
    args = [p_lru, p_lru]
    for name in ("cw", "cb", "wbd", "bias", "lam"):
        a = prm[name]
        in_specs.append(_full(a.shape))
        args.append(a)
    if has_init:
        in_specs.append(pl.BlockSpec((1, 2, INNER), lambda b: (b, 0, 0)))
        args.append(h0)
    out_shape = [jax.ShapeDtypeStruct((bsz, t, INNER), BF16)]
    out_specs = [pl.BlockSpec((1, t, INNER), lambda b: (b, 0, 0))]
    if emit_state:
        out_shape.append(jax.ShapeDtypeStruct((bsz, 2, INNER), F32))
        out_specs.append(pl.BlockSpec((1, 2, INNER), lambda b: (b, 0, 0)))
    res = pl.pallas_call(
        functools.partial(_lru_body, t=t, has_init=has_init, emit_state=emit_state),
        out_shape=out_shape,
        grid=(bsz,),
        in_specs=in_specs,
        out_specs=out_specs,
        scratch_shapes=[pltpu.VMEM((t, INNER), F32)] + [pltpu.VMEM((t, PAIR), F32)] * 6,
        compiler_params=_params(),
        name="rglru",
    )(*args)
    return (res[0], res[1]) if emit_state else (res[0], None)


def _pad_cols(x, n):
    return jnp.pad(x, ((0, 0), (0, n - x.shape[1])))


def _layer_params(l, w_in, ssm_conv_w, ssm_conv_b, ssm_a_log, ssm_dt_bias, ssm_d, ssm_norm, attn_q_norm, attn_k_norm,
                  dn_conv_w, dn_conv_b, dn_a_log, dn_dt_bias, dn_norm, lru_conv_w, lru_conv_b, lru_w_a, lru_b_a,
                  lru_w_i, lru_b_i, lru_lambda):
    w = w_in[l]
    o = 0

    def take(n):
        nonlocal o
        part = w[:, o:o + n]
        o += n
        return part

    z_a, xs_a, bc_a, dt_a = take(INNER), take(INNER), take(2 * PAIR), take(2 * N_HEADS)
    q_b, k_b, v_b = take(INNER), take(PAIR), take(PAIR)
    q_c, k_c, v_c = take(INNER), take(INNER), take(INNER)
    beta_c, a_c, gate_c = take(2 * N_HEADS), take(2 * N_HEADS), take(INNER)
    x_d, y_d = take(INNER), take(INNER)
    gate_raw = take(N_BRANCH * D_MODEL)

    def dup(x):
        return jnp.concatenate([x[:, :HEAD], x[:, :HEAD], x[:, HEAD:], x[:, HEAD:]], axis=1)

    nh2 = 2 * N_HEADS
    prm = {
        "w_ssd": _pad_cols(jnp.concatenate([z_a, xs_a, bc_a, dt_a], axis=1), SSD_N).astype(BF16),
        "w_ssdT": dt_a.T.astype(BF16),
        "w_att": jnp.concatenate([q_b, dup(k_b), dup(v_b)], axis=1).astype(BF16),
        "w_dn": _pad_cols(jnp.concatenate([q_c, k_c, v_c, gate_c, beta_c, a_c], axis=1), DN_N).astype(BF16),
        "w_dnT": jnp.concatenate([beta_c, a_c], axis=1).T.astype(BF16),
        "w_lru": jnp.concatenate([x_d, y_d], axis=1).astype(BF16),
        "w_gate": gate_raw.astype(BF16),
    }
    cw, cb = ssm_conv_w[l], ssm_conv_b[l][None, :]
    dtb = ssm_dt_bias[l].reshape(1, nh2)
    alog = ssm_a_log[l].reshape(1, nh2)
    prm["ssd"] = {
        "cwx": cw[:, :INNER], "cbx": cb[:, :INNER], "cwbc": cw[:, INNER:], "cbbc": cb[:, INNER:],
        "dtb_row": _pad_cols(dtb, PAIR), "dtb_col": jnp.broadcast_to(dtb.T, (nh2, SSD_CHUNK)),
        "alog_row": _pad_cols(alog, PAIR), "alog_col": jnp.broadcast_to(alog.T, (nh2, SSD_CHUNK)),
        "dskip": jnp.repeat(ssm_d[l], HEAD)[None, :], "gnorm": ssm_norm[l][None, :],
    }
    prm["att"] = {"qgain": jnp.tile(attn_q_norm[l], 2)[None, :], "kgain": jnp.tile(attn_k_norm[l], 2)[None, :]}
    dcw, dcb = dn_conv_w[l], dn_conv_b[l][None, :]
    zeros16 = jnp.zeros((1, nh2), F32)
    d_alog = jnp.concatenate([zeros16, dn_a_log[l].reshape(1, nh2)], axis=1)
    d_bias = jnp.concatenate([zeros16, dn_dt_bias[l].reshape(1, nh2)], axis=1)
    prm["dn"] = {
        "cwq": dcw[:, :INNER], "cwk": dcw[:, INNER:2 * INNER], "cwv": dcw[:, 2 * INNER:],
        "cbq": dcb[:, :INNER], "cbk": dcb[:, INNER:2 * INNER], "cbv": dcb[:, 2 * INNER:],
        "alog_row": _pad_cols(d_alog, PAIR), "bias_row": _pad_cols(d_bias, PAIR),
        "alog_col": jnp.broadcast_to(d_alog.T, (2 * nh2, ROW_CHUNK)),
        "bias_col": jnp.broadcast_to(d_bias.T, (2 * nh2, ROW_CHUNK)),
        "gnorm": jnp.tile(dn_norm[l], 2)[None, :],
    }
    wa, wi = lru_w_a[l], lru_w_i[l]
    z64 = jnp.zeros((HEAD, HEAD), F32)
    wbd, bias = [], []
    for p in range(N_PAIRS):
        blocks = []
        for wsrc in (wa, wi):
            for d in range(2):
                top = jnp.concatenate([wsrc[d, 2 * p], z64], axis=1)
                bot = jnp.concatenate([z64, wsrc[d, 2 * p + 1]], axis=1)
                blocks.append(jnp.concatenate([top, bot], axis=0))
        wbd.append(jnp.concatenate(blocks, axis=1))
        bias.append(jnp.concatenate([lru_b_a[l][0, p * PAIR:(p + 1) * PAIR], lru_b_a[l][1, p * PAIR:(p + 1) * PAIR],
                                     lru_b_i[l][0, p * PAIR:(p + 1) * PAIR], lru_b_i[l][1, p * PAIR:(p + 1) * PAIR]])[None, :])
    prm["lru"] = {"cw": lru_conv_w[l], "cb": lru_conv_b[l][None, :], "wbd": jnp.stack(wbd).astype(BF16),
                  "bias": jnp.stack(bias), "lam": lru_lambda[l]}
    return prm


def _rope_tables(t):
    n_freq = HEAD // 4
    inv = ROPE_THETA ** (-jnp.arange(n_freq, dtype=F32) / n_freq)
    rows = t // GRID_W
    row = jnp.repeat(jnp.arange(rows, dtype=F32), GRID_W)
    col = jnp.tile(jnp.arange(GRID_W, dtype=F32), rows)
    ang_r = row[:, None] * inv
    ang_c = col[:, None] * inv
    ang = jnp.concatenate([ang_r, ang_r, ang_c, ang_c], axis=1)
    sign = jnp.tile(jnp.concatenate([-jnp.ones((n_freq,), F32), jnp.ones((n_freq,), F32)]), 2)
    cos = jnp.cos(ang)
    sin = jnp.sin(ang) * sign
    return jnp.tile(cos, (1, 2)), jnp.tile(sin, (1, 2))


def _dup_kv(x):
    return jnp.concatenate([x[:, :, 0], x[:, :, 0], x[:, :, 1], x[:, :, 1]], axis=-1)


def _trunk_layer(h, mod, lw, prm, rope_tabs, ctx, fin_gain, *, final):
    h = _ffn(h, mod, lw["norm_ffn1"], lw["ffn1_w13"], lw["ffn1_w2"], fin_gain, rows=(0, 1, 2), final=False)
    gain = lw["norm_mix"]
    emit = ctx is None
    p_ssd, dtT = _proj(h, mod, gain, prm["w_ssd"], prm["w_ssdT"])
    p_att = _proj(h, mod, gain, prm["w_att"])
    p_dn, baT = _proj(h, mod, gain, prm["w_dn"], prm["w_dnT"])
    p_lru = _proj(h, mod, gain, prm["w_lru"])
    if emit:
        o_a, st_ssm = _ssd(p_ssd, dtT, prm["ssd"], None, emit_state=True)
        o_b, k_new, v_new = _attn(p_att, prm["att"], None, None)
        o_c, st_dn = _dn(p_dn, baT, prm["dn"], None, emit_state=True)
        o_d, st_lru = _lru(p_lru, prm["lru"], None, emit_state=True)
        new_ctx = (k_new, v_new, st_ssm, st_dn, st_lru)
    else:
        ck, cv, ssm0, dn0, lru0 = ctx
        o_a, _ = _ssd(p_ssd, dtT, prm["ssd"], ssm0, emit_state=False)
        (o_b,) = _attn(p_att, prm["att"], rope_tabs, (ck, cv))
        o_c, _ = _dn(p_dn, baT, prm["dn"], dn0, emit_state=False)
        o_d, _ = _lru(p_lru, prm["lru"], lru0, emit_state=False)
        new_ctx = None
    h = _merge(h, mod, gain, (o_a, o_b, o_c, o_d), prm["w_gate"], lw["w_branch"], lw["w_out"])
    h = _ffn(h, mod, lw["norm_ffn2"], lw["ffn2_w13"], lw["ffn2_w2"], fin_gain, rows=(6, 7, 8), final=final)
    return h, new_ctx


def kernel(x_prompt, x_sample, cache_k, cache_v, state_ssm, state_delta, state_lru, c, c_ctx,
           w_ada, b_ada, norm_ffn1, ffn1_w13, ffn1_w2, norm_mix, w_in,
           ssm_conv_w, ssm_conv_b, ssm_a_log, ssm_dt_bias, ssm_d, ssm_norm,
           attn_q_norm, attn_k_norm,
           dn_conv_w, dn_conv_b, dn_a_log, dn_dt_bias, dn_norm,
           lru_conv_w, lru_conv_b, lru_w_a, lru_b_a, lru_w_i, lru_b_i, lru_lambda,
           w_branch, w_out, norm_ffn2, ffn2_w13, ffn2_w2, final_norm):
    depth = w_in.shape[0]
    bsz_p, t_p, _ = x_prompt.shape
    bsz_s, t_s, _ = x_sample.shape
    assert bsz_s + 1 <= MOD_ROWS
    cvec = jnp.concatenate([c, c_ctx[None], jnp.zeros((MOD_ROWS - bsz_s - 1, D_MODEL), F32)], axis=0)
    mod_all = _adaln(cvec, w_ada, b_ada)
    mod_all = mod_all.reshape(depth, MOD_ROWS, N_MOD, D_MODEL)
    mod_all = jnp.pad(mod_all, ((0, 0), (0, 0), (0, MOD_ROWS - N_MOD), (0, 0)))
    rope_tabs = _rope_tables(t_s)
    fin_gain = final_norm[None, :]

    hp, hs = x_prompt, x_sample
    ks, vs, ssm_s, dn_s, lru_s = [], [], [], [], []
    for l in range(depth):
        prm = _layer_params(l, w_in, ssm_conv_w, ssm_conv_b, ssm_a_log, ssm_dt_bias, ssm_d, ssm_norm,
                            attn_q_norm, attn_k_norm, dn_conv_w, dn_conv_b, dn_a_log, dn_dt_bias, dn_norm,
                            lru_conv_w, lru_conv_b, lru_w_a, lru_b_a, lru_w_i, lru_b_i, lru_lambda)
        lw = {
            "norm_ffn1": norm_ffn1[l][None, :], "ffn1_w13": ffn1_w13[l].astype(BF16), "ffn1_w2": ffn1_w2[l].astype(BF16),
            "norm_mix": norm_mix[l][None, :], "w_branch": w_branch[l].astype(BF16), "w_out": w_out[l].astype(BF16),
            "norm_ffn2": norm_ffn2[l][None, :], "ffn2_w13": ffn2_w13[l].astype(BF16), "ffn2_w2": ffn2_w2[l].astype(BF16),
        }
        final = l == depth - 1
        mod_lat = mod_all[l, :bsz_s]
        mod_ctx = mod_all[l, bsz_s:bsz_s + 1]
        hp, st = _trunk_layer(hp, mod_ctx, lw, prm, None, None, fin_gain, final=final)
        k_c, v_c, st_ssm, st_dn, st_lru = st
        ks.append(k_c.reshape(bsz_p, t_p, KV_HEADS, HEAD))
        vs.append(v_c.reshape(bsz_p, t_p, KV_HEADS, HEAD))
        ssm_s.append(st_ssm.reshape(bsz_p, 2, N_HEADS, HEAD, HEAD))
        dn_s.append(st_dn.reshape(bsz_p, 2, N_HEADS, HEAD, HEAD))
        lru_s.append(st_lru)
        ctx_l = (_dup_kv(cache_k[:, l]), _dup_kv(cache_v[:, l]),
                 state_ssm[:, l].reshape(bsz_s, 2, INNER, HEAD), state_delta[:, l].reshape(bsz_s, 2, INNER, HEAD),
                 state_lru[:, l])
        hs, _ = _trunk_layer(hs, mod_lat, lw, prm, rope_tabs, ctx_l, fin_gain, final=final)
    return (hp, hs, jnp.stack(ks, axis=1), jnp.stack(vs, axis=1), jnp.stack(ssm_s, axis=1),
            jnp.stack(dn_s, axis=1), jnp.stack(lru_s, axis=1))
```

```python
import functools
import math

import jax
import jax.numpy as jnp
from jax import lax
from jax.experimental import pallas as pl
from jax.experimental.pallas import tpu as pltpu

F32 = jnp.float32
BF16 = jnp.bfloat16

D_MODEL = 1024
D_FF = 2816
N_MOD = 9
MOD_ROWS = 16
EPS = 1e-6
GRID_W = 64
CONV_W = 4
CONV_LP = CONV_W // 2
HALO = 8
HEAD = 64
PAIR = 2 * HEAD
N_HEADS = 8
N_PAIRS = N_HEADS // 2
INNER = N_HEADS * HEAD
KV_HEADS = 2
SSM_GROUPS = 2
ROPE_THETA = 10000.0
LRU_C = 8.0
N_BRANCH = 4
SSD_CHUNK = 256
DN_CHUNK = 64
DN_BASE = 8
ROW_CHUNK = 256
ATT_TQ = 128
VMEM_LIMIT = 56 * 1024 * 1024

SSD_N = 1408
ATT_N = 1024
DN_N = 2176
LRU_N = 1024


def _dot(a, b):
    return jnp.dot(a.astype(BF16), b.astype(BF16), preferred_element_type=F32)


def _dot_nt(a, b):
    return lax.dot_general(a.astype(BF16), b.astype(BF16), (((1,), (1,)), ((), ())),
                           preferred_element_type=F32)


def _dot_tn(a, b):
    return lax.dot_general(a.astype(BF16), b.astype(BF16), (((0,), (0,)), ((), ())),
                           preferred_element_type=F32)


def _split3(x):
    hi = x.astype(BF16)
    r1 = x - hi.astype(F32)
    mid = r1.astype(BF16)
    lo = (r1 - mid.astype(F32)).astype(BF16)
    return hi, mid, lo


def _dot_sel(x, sel):
    hi, mid, lo = _split3(x)
    s = sel.astype(BF16)
    return (jnp.dot(hi, s, preferred_element_type=F32) + jnp.dot(mid, s, preferred_element_type=F32)
            + jnp.dot(lo, s, preferred_element_type=F32))


def _sel_dot(sel, x):
    hi, mid, lo = _split3(x)
    s = sel.astype(BF16)
    return (jnp.dot(s, hi, preferred_element_type=F32) + jnp.dot(s, mid, preferred_element_type=F32)
            + jnp.dot(s, lo, preferred_element_type=F32))


def _iota(shape, axis):
    return lax.broadcasted_iota(jnp.int32, shape, axis)


def _silu(x):
    return x * jax.nn.sigmoid(x)


def _softplus(x):
    return jnp.maximum(x, 0.0) + jnp.log1p(jnp.exp(-jnp.abs(x)))


def _rms(x, g):
    return x * lax.rsqrt(jnp.mean(x * x, axis=-1, keepdims=True) + EPS) * g


def _head_ones():
    return (_iota((PAIR, PAIR), 0) // HEAD == _iota((PAIR, PAIR), 1) // HEAD).astype(F32)


def _head_sumsq(x):
    return _dot_sel(x * x, _head_ones())


def _lo_tri(n):
    return (_iota((n, n), 1) <= _iota((n, n), 0)).astype(F32)


def _up_tri(n):
    return (_iota((n, n), 1) >= _iota((n, n), 0)).astype(F32)


def _conv_window(ref, r0, c, nchunks, rows, t_total):
    cur = ref[0, pl.ds(r0, rows), :]
    prev_start = pl.multiple_of(jnp.maximum(r0 - HALO, 0), HALO)
    next_start = pl.multiple_of(jnp.minimum(r0 + rows, t_total - HALO), HALO)
    prev = jnp.where(c > 0, ref[0, pl.ds(prev_start, HALO), :], 0.0)
    nxt = jnp.where(c < nchunks - 1, ref[0, pl.ds(next_start, HALO), :], 0.0)
    return jnp.concatenate([prev, cur, nxt], axis=0)


def _dwconv(ext, w, b, rows):
    out = b
    for j in range(CONV_W):
        off = HALO - CONV_LP + j
        out = out + w[j:j + 1, :] * ext[off:off + rows, :]
    return out


def _expand_heads(n_src_rows, first, width):
    r = _iota((n_src_rows, width), 0)
    l = _iota((n_src_rows, width), 1)
    return (r - first == l // HEAD).astype(F32)


def _full(shape):
    zeros = (0,) * len(shape)
    return pl.BlockSpec(shape, lambda *_: zeros, pipeline_mode=pl.Buffered(1))


def _params():
    return pltpu.CompilerParams(vmem_limit_bytes=VMEM_LIMIT)


def _adaln_body(c_ref, w_ref, b_ref, o_ref):
    o_ref[0] = _dot(_silu(c_ref[...]), w_ref[0]) + b_ref[0]


def _adaln(cvec, w_ada, b_ada):
    depth = w_ada.shape[0]
    n = w_ada.shape[2]
    tn = n // N_MOD
    return pl.pallas_call(
        _adaln_body,
        out_shape=jax.ShapeDtypeStruct((depth, MOD_ROWS, n), F32),
        grid=(depth, N_MOD),
        in_specs=[pl.BlockSpec((MOD_ROWS, D_MODEL), lambda l, j: (0, 0)),
                  pl.BlockSpec((1, D_MODEL, tn), lambda l, j: (l, 0, j)),
                  pl.BlockSpec((1, 1, tn), lambda l, j: (l, 0, j))],
        out_specs=pl.BlockSpec((1, MOD_ROWS, tn), lambda l, j: (l, 0, j)),
        compiler_params=_params(),
        name="adaln",
    )(cvec, w_ada, b_ada.reshape(depth, 1, n))


def _ffn_body(h_ref, mod_ref, gain_ref, wg_ref, wu_ref, w2_ref, fin_ref, o_ref, xn_sc, acc_sc,
              *, rows, n_ff, final):
    shift, scale, gate = rows
    f = pl.program_id(2)

    @pl.when(f == 0)
    def _():
        ms = mod_ref[0]
        xn = _rms(h_ref[0], gain_ref[...]) * (1.0 + ms[scale:scale + 1]) + ms[shift:shift + 1]
        xn_sc[...] = xn.astype(BF16)
        acc_sc[...] = jnp.zeros_like(acc_sc)

    xn = xn_sc[...]
    g = jnp.dot(xn, wg_ref[...], preferred_element_type=F32)
    u = jnp.dot(xn, wu_ref[...], preferred_element_type=F32)
    acc_sc[...] += _dot(_silu(g) * u, w2_ref[...])

    @pl.when(f == n_ff - 1)
    def _():
        ms = mod_ref[0]
        hn = h_ref[0] + 0.5 * ms[gate:gate + 1] * acc_sc[...]
        o_ref[0] = _rms(hn, fin_ref[...]) if final else hn


def _ffn(h, mod, gain, w13, w2, fin_gain, *, rows, final):
    bsz, t, d = h.shape
    tm = min(512, t)
    n_ff = 2
    tf = D_FF // n_ff
    mb = mod.shape[0]
    mod_ix = (lambda b, i, f: (b, 0, 0)) if mb > 1 else (lambda b, i, f: (0, 0, 0))
    return pl.pallas_call(
        functools.partial(_ffn_body, rows=rows, n_ff=n_ff, final=final),
        out_shape=jax.ShapeDtypeStruct((bsz, t, d), F32),
        grid=(bsz, t // tm, n_ff),
        in_specs=[pl.BlockSpec((1, tm, d), lambda b, i, f: (b, i, 0)),
                  pl.BlockSpec((1, MOD_ROWS, d), mod_ix),
                  pl.BlockSpec((1, d), lambda b, i, f: (0, 0)),
                  pl.BlockSpec((d, tf), lambda b, i, f: (0, f)),
                  pl.BlockSpec((d, tf), lambda b, i, f: (0, n_ff + f)),
                  pl.BlockSpec((tf, d), lambda b, i, f: (f, 0)),
                  pl.BlockSpec((1, d), lambda b, i, f: (0, 0))],
        out_specs=pl.BlockSpec((1, tm, d), lambda b, i, f: (b, i, 0)),
        scratch_shapes=[pltpu.VMEM((tm, d), BF16), pltpu.VMEM((tm, d), F32)],
        compiler_params=_params(),
        name="ffn",
    )(h, mod, gain, w13, w13, w2, fin_gain)


def _proj_body(h_ref, mod_ref, gain_ref, w_ref, *rest, has_t):
    ms = mod_ref[0]
    xn = (_rms(h_ref[0], gain_ref[...]) * (1.0 + ms[4:5]) + ms[3:4]).astype(BF16)
    if has_t:
        wt_ref, o_ref, ot_ref = rest
        ot_ref[0] = lax.dot_general(wt_ref[...], xn, (((1,), (1,)), ((), ())), preferred_element_type=F32)
    else:
        (o_ref,) = rest
    o_ref[0] = jnp.dot(xn, w_ref[...], preferred_element_type=F32)


def _proj(h, mod, gain, w, wt=None):
    bsz, t, d = h.shape
    n = w.shape[1]
    tm = min(512, t)
    mb = mod.shape[0]
    mod_ix = (lambda b, i: (b, 0, 0)) if mb > 1 else (lambda b, i: (0, 0, 0))
    in_specs = [pl.BlockSpec((1, tm, d), lambda b, i: (b, i, 0)),
                pl.BlockSpec((1, MOD_ROWS, d), mod_ix),
                pl.BlockSpec((1, d), lambda b, i: (0, 0)),
                _full((d, n))]
    out_shape = [jax.ShapeDtypeStruct((bsz, t, n), F32)]
    out_specs = [pl.BlockSpec((1, tm, n), lambda b, i: (b, i, 0))]
    args = [h, mod, gain, w]
    if wt is not None:
        r = wt.shape[0]
        in_specs.append(_full((r, d)))
        out_shape.append(jax.ShapeDtypeStruct((bsz, r, t), F32))
        out_specs.append(pl.BlockSpec((1, r, tm), lambda b, i: (b, 0, i)))
        args.append(wt)
    res = pl.pallas_call(
        functools.partial(_proj_body, has_t=wt is not None),
        out_shape=out_shape,
        grid=(bsz, t // tm),
        in_specs=in_specs,
        out_specs=out_specs,
        compiler_params=_params(),
        name="proj",
    )(*args)
    return res if wt is not None else res[0]


def _merge_body(h_ref, mod_ref, gain_ref, oa_ref, ob_ref, oc_ref, od_ref, wg_ref, wb_ref, wo_ref, o_ref):
    ms = mod_ref[0]
    h = h_ref[0]
    xn = (_rms(h, gain_ref[...]) * (1.0 + ms[4:5]) + ms[3:4]).astype(BF16)
    merged = None
    for n, br_ref in enumerate((oa_ref, ob_ref, oc_ref, od_ref)):
        gate = jax.nn.sigmoid(jnp.dot(xn, wg_ref[:, n * D_MODEL:(n + 1) * D_MODEL], preferred_element_type=F32))
        term = gate * jnp.dot(br_ref[0], wb_ref[n], preferred_element_type=F32)
        merged = term if merged is None else merged + term
    o_ref[0] = h + ms[5:6] * _dot(merged, wo_ref[...])


def _merge(h, mod, gain, branches, w_gate, w_branch, w_out):
    bsz, t, d = h.shape
    tm = min(512, t)
    mb = mod.shape[0]
    mod_ix = (lambda b, i: (b, 0, 0)) if mb > 1 else (lambda b, i: (0, 0, 0))
    br_spec = pl.BlockSpec((1, tm, INNER), lambda b, i: (b, i, 0))
    return pl.pallas_call(
        _merge_body,
        out_shape=jax.ShapeDtypeStruct((bsz, t, d), F32),
        grid=(bsz, t // tm),
        in_specs=[pl.BlockSpec((1, tm, d), lambda b, i: (b, i, 0)),
                  pl.BlockSpec((1, MOD_ROWS, d), mod_ix),
                  pl.BlockSpec((1, d), lambda b, i: (0, 0)),
                  br_spec, br_spec, br_spec, br_spec,
                  _full((d, N_BRANCH * d)), _full((N_BRANCH, INNER, d)), _full((d, d))],
        out_specs=pl.BlockSpec((1, tm, d), lambda b, i: (b, i, 0)),
        compiler_params=_params(),
        name="merge",
    )(h, mod, gain, *branches, w_gate, w_branch, w_out)


def _ssd_body(*refs, t, has_init, emit_state):
    (z_ref, xs_ref, bc_ref, dt_ref, dtT_ref, cwx_ref, cbx_ref, cwbc_ref, cbbc_ref,
     dtb_row_ref, dtb_col_ref, alog_row_ref, alog_col_ref, dskip_ref, gnorm_ref) = refs[:15]
    pos = 15
    h0_ref = None
    if has_init:
        h0_ref = refs[pos]
        pos += 1
    o_ref = refs[pos]
    pos += 1
    st_ref = None
    if emit_state:
        st_ref = refs[pos]
        pos += 1
    y_sc, cum_sc, cm_sc, dh_sc, tot_sc, hst_sc = refs[pos:]

    cl = SSD_CHUNK
    nc = t // cl
    lane = _iota((cl, PAIR), 1)
    ii = _iota((cl, cl), 0)
    jj = _iota((cl, cl), 1)
    lo = _lo_tri(cl)
    up = _up_tri(cl)
    hp = lax.Precision.HIGHEST
    a_row = -jnp.exp(alog_row_ref[...])
    a_col = -jnp.exp(alog_col_ref[...])
    sel_f = _expand_heads(PAIR, 0, INNER)
    sel_b = _expand_heads(PAIR, N_HEADS, INNER)

    def intra(c, carry):
        r0 = pl.multiple_of(c * cl, cl)
        xs = _silu(_dwconv(_conv_window(xs_ref, r0, c, nc, cl, t), cwx_ref[...], cbx_ref[...], cl))
        bc = _silu(_dwconv(_conv_window(bc_ref, r0, c, nc, cl, t), cwbc_ref[...], cbbc_ref[...], cl))
        bm = bc[:, :PAIR]
        cm = bc[:, PAIR:]
        dt = _softplus(dt_ref[0, pl.ds(r0, cl), :] + dtb_row_ref[...])
        dtT = _softplus(dtT_ref[0, :, pl.ds(r0, cl)] + dtb_col_ref[...])
        da = dt * a_row
        daT = dtT * a_col
        cum = jnp.where(lane < N_HEADS, jnp.dot(lo, da, precision=hp, preferred_element_type=F32),
                        jnp.dot(up, da, precision=hp, preferred_element_type=F32))
        rowsel = _iota((2 * N_HEADS, cl), 0) < N_HEADS
        cumT = jnp.where(rowsel, jnp.dot(daT, up, precision=hp, preferred_element_type=F32),
                         jnp.dot(daT, lo, precision=hp, preferred_element_type=F32))
        cum_sc[pl.ds(r0, cl), :] = cum
        cm_sc[pl.ds(r0, cl), :] = cm
        cb = []
        for g in range(SSM_GROUPS):
            cg = jnp.where(lane // HEAD == g, cm, 0.0)
            cb.append(_dot_nt(cg, bm))
        ypairs = []
        for p in range(N_PAIRS):
            xp = xs[:, p * PAIR:(p + 1) * PAIR].astype(BF16)
            halves = []
            for a in range(2):
                h = 2 * p + a
                g = h // (N_HEADS // SSM_GROUPS)
                hb = N_HEADS + h
                sf = jnp.exp(jnp.where(jj <= ii, cum[:, h:h + 1] - cumT[h:h + 1, :], -jnp.inf)) * dtT[h:h + 1, :]
                sb = jnp.exp(jnp.where(jj >= ii, cum[:, hb:hb + 1] - cumT[hb:hb + 1, :], -jnp.inf)) * dtT[hb:hb + 1, :]
                s = (cb[g] * (sf + sb)).astype(BF16)
                halves.append(jnp.dot(s, xp, preferred_element_type=F32))
            ypairs.append(jnp.where(lane < HEAD, halves[0], halves[1]))
        y = jnp.concatenate(ypairs, axis=1) + dskip_ref[...] * xs
        y_sc[pl.ds(r0, cl), :] = y
        tot = jnp.where(lane[0:1] < N_HEADS, cum[cl - 1:cl, :], cum[0:1, :])
        tot_sc[c] = jnp.broadcast_to(tot, (HALO, PAIR))
        wexp = jnp.exp(tot - cum) * dt
        for d, sel in enumerate((sel_f, sel_b)):
            xw = xs * _dot_sel(wexp, sel)
            for p in range(N_PAIRS):
                g = p // (N_PAIRS // SSM_GROUPS)
                bg = jnp.where(lane // HEAD == g, bm, 0.0)
                dh_sc[c, d, p] = _dot_tn(xw[:, p * PAIR:(p + 1) * PAIR], bg)
        return carry

    lax.fori_loop(0, nc, intra, 0)

    rr = _iota((PAIR, PAIR), 0)
    for d in range(2):
        sel = sel_f if d == 0 else sel_b
        for p in range(N_PAIRS):
            g = p // (N_PAIRS // SSM_GROUPS)
            if has_init:
                blk = h0_ref[0, d, p * PAIR:(p + 1) * PAIR, :]
                z64 = jnp.zeros_like(blk)
                hst_sc[p] = jnp.concatenate([blk, z64] if g == 0 else [z64, blk], axis=1)
            else:
                hst_sc[p] = jnp.zeros((PAIR, PAIR), F32)

        def inter(k, carry, d=d, sel=sel):
            c = k if d == 0 else nc - 1 - k
            r0 = pl.multiple_of(c * cl, cl)
            ecum = _dot_sel(jnp.exp(cum_sc[pl.ds(r0, cl), :]), sel)
            cm = cm_sc[pl.ds(r0, cl), :]
            tot = tot_sc[c]
            dec = jnp.exp(tot[0:1, :])
            for p in range(N_PAIRS):
                g = p // (N_PAIRS // SSM_GROUPS)
                cg = jnp.where(lane // HEAD == g, cm, 0.0)
                hs = hst_sc[p]
                yi = _dot_nt(cg, hs)
                cols = pl.ds(p * PAIR, PAIR)
                y_sc[pl.ds(r0, cl), cols] = y_sc[pl.ds(r0, cl), cols] + yi * ecum[:, p * PAIR:(p + 1) * PAIR]
                h0i = d * N_HEADS + 2 * p
                dcol = jnp.where(rr < HEAD, dec[:, h0i:h0i + 1], dec[:, h0i + 1:h0i + 2])
                hst_sc[p] = hs * dcol + dh_sc[c, d, p]
            return carry

        lax.fori_loop(0, nc, inter, 0)
        if emit_state:
            for p in range(N_PAIRS):
                g = p // (N_PAIRS // SSM_GROUPS)
                st_ref[0, d, p * PAIR:(p + 1) * PAIR, :] = hst_sc[p][:, g * HEAD:(g + 1) * HEAD]

    def finish(c, carry):
        r0 = pl.multiple_of(c * cl, cl)
        v = y_sc[pl.ds(r0, cl), :] * _silu(z_ref[0, pl.ds(r0, cl), :])
        o_ref[0, pl.ds(r0, cl), :] = _rms(v, gnorm_ref[...]).astype(BF16)
        return carry

    lax.fori_loop(0, nc, finish, 0)


def _ssd(p_ssd, dtT, prm, h0, *, emit_state):
    bsz, t, _ = p_ssd.shape
    nc = t // SSD_CHUNK
    has_init = h0 is not None
    seq = lambda w, j: pl.BlockSpec((1, t, w), lambda b: (b, 0, j), pipeline_mode=pl.Buffered(1))
    in_specs = [seq(INNER, 0), seq(INNER, 1), seq(2 * PAIR, 4), seq(PAIR, 10),
                pl.BlockSpec((1, 2 * N_HEADS, t), lambda b: (b, 0, 0), pipeline_mode=pl.Buffered(1))]
    args = [p_ssd, p_ssd, p_ssd, p_ssd, dtT]
    for name in ("cwx", "cbx", "cwbc", "cbbc", "dtb_row", "dtb_col", "alog_row", "alog_col", "dskip", "gnorm"):
        a = prm[name]
        in_specs.append(_full(a.shape))
        args.append(a)
    if has_init:
        in_specs.append(pl.BlockSpec((1, 2, INNER, HEAD), lambda b: (b, 0, 0, 0)))
        args.append(h0)
    out_shape = [jax.ShapeDtypeStruct((bsz, t, INNER), BF16)]
    out_specs = [pl.BlockSpec((1, t, INNER), lambda b: (b, 0, 0))]
    if emit_state:
        out_shape.append(jax.ShapeDtypeStruct((bsz, 2, INNER, HEAD), F32))
        out_specs.append(pl.BlockSpec((1, 2, INNER, HEAD), lambda b: (b, 0, 0, 0)))
    res = pl.pallas_call(
        functools.partial(_ssd_body, t=t, has_init=has_init, emit_state=emit_state),
        out_shape=out_shape,
        grid=(bsz,),
        in_specs=in_specs,
        out_specs=out_specs,
        scratch_shapes=[pltpu.VMEM((t, INNER), F32), pltpu.VMEM((t, PAIR), F32), pltpu.VMEM((t, PAIR), F32),
                        pltpu.VMEM((nc, 2, N_PAIRS, PAIR, PAIR), F32), pltpu.VMEM((nc, HALO, PAIR), F32),
                        pltpu.VMEM((N_PAIRS, PAIR, PAIR), F32)],
        compiler_params=_params(),
        name="ssd",
    )(*args)
    return (res[0], res[1]) if emit_state else (res[0], None)


def _attn_body(*refs, t, s_ctx, rope):
    q_ref, kk_ref, vv_ref, qg_ref, kg_ref = refs[:5]
    pos = 5
    if rope:
        cos_ref, sin_ref, ck_ref, cv_ref = refs[pos:pos + 4]
        pos += 4
    o_ref = refs[pos]
    pos += 1
    if not rope:
        knew_ref, vnew_ref = refs[pos:pos + 2]
        pos += 2
    q_sc, k_sc, v_sc = refs[pos:]

    rc = ROW_CHUNK
    lane = _iota((rc, PAIR), 1)
    first_half = (lane % (HEAD // 2)) < HEAD // 4

    def rot(x, cos, sin):
        partner = jnp.where(first_half, pltpu.roll(x, PAIR - HEAD // 4, 1), pltpu.roll(x, HEAD // 4, 1))
        return x * cos + partner * sin

    def normed(x, gain):
        return x * lax.rsqrt(_head_sumsq(x) * (1.0 / HEAD) + EPS) * gain

    if rope:
        k_sc[0:s_ctx, :] = ck_ref[0].astype(BF16)
        v_sc[0:s_ctx, :] = cv_ref[0].astype(BF16)

    def prep(c, carry):
        r0 = pl.multiple_of(c * rc, rc)
        if rope:
            cos = cos_ref[pl.ds(r0, rc), :]
            sin = sin_ref[pl.ds(r0, rc), :]
        for p in range(N_PAIRS):
            x = normed(q_ref[0, pl.ds(r0, rc), p * PAIR:(p + 1) * PAIR], qg_ref[...])
            if rope:
                x = rot(x, cos, sin)
            q_sc[pl.ds(r0, rc), p * PAIR:(p + 1) * PAIR] = (x * (HEAD ** -0.5)).astype(BF16)
        kn = []
        for g in range(KV_HEADS):
            x = normed(kk_ref[0, pl.ds(r0, rc), g * PAIR:(g + 1) * PAIR], kg_ref[...])
            kn.append(x)
            if rope:
                x = rot(x, cos, sin)
            k_sc[pl.ds(s_ctx + r0, rc), g * PAIR:(g + 1) * PAIR] = x.astype(BF16)
        vv = vv_ref[0, pl.ds(r0, rc), :]
        v_sc[pl.ds(s_ctx + r0, rc), :] = vv.astype(BF16)
        if not rope:
            knew_ref[0, pl.ds(r0, rc), :] = jnp.where(lane < HEAD, kn[0], kn[1])
            vnew_ref[0, pl.ds(r0, rc), :] = jnp.where(lane < HEAD, vv[:, :PAIR], vv[:, PAIR:])
        return carry

    lax.fori_loop(0, t // rc, prep, 0)

    tq = ATT_TQ
    lane_q = _iota((tq, PAIR), 1)

    def attend(i, carry):
        r0 = pl.multiple_of(i * tq, tq)
        for p in range(N_PAIRS):
            g = p // (N_PAIRS // KV_HEADS)
            qp = q_sc[pl.ds(r0, tq), p * PAIR:(p + 1) * PAIR]
            kg = k_sc[:, g * PAIR:(g + 1) * PAIR]
            vg = v_sc[:, g * PAIR:(g + 1) * PAIR]
            halves = []
            for a in range(2):
                qa = jnp.where(lane_q // HEAD == a, qp, jnp.zeros_like(qp))
                s = lax.dot_general(qa, kg, (((1,), (1,)), ((), ())), preferred_element_type=F32)
                e = jnp.exp(s - jnp.max(s, axis=-1, keepdims=True))
                l = jnp.sum(e, axis=-1, keepdims=True)
                halves.append(jnp.dot(e.astype(BF16), vg, preferred_element_type=F32) / l)
            o_ref[0, pl.ds(r0, tq), p * PAIR:(p + 1) * PAIR] = jnp.where(lane_q < HEAD, halves[0], halves[1]).astype(BF16)
        return carry

    lax.fori_loop(0, t // tq, attend, 0)


def _attn(p_att, prm, rope_tabs, ctx_kv):
    bsz, t, _ = p_att.shape
    rope = rope_tabs is not None
    s_ctx = ctx_kv[0].shape[1] if rope else 0
    s_all = s_ctx + t
    seq = lambda w, j: pl.BlockSpec((1, t, w), lambda b: (b, 0, j), pipeline_mode=pl.Buffered(1))
    in_specs = [seq(INNER, 0), seq(2 * PAIR, 2), seq(2 * PAIR, 3), _full((1, PAIR)), _full((1, PAIR))]
    args = [p_att, p_att, p_att, prm["qgain"], prm["kgain"]]
    if rope:
        in_specs += [_full((t, PAIR)), _full((t, PAIR)),
                     pl.BlockSpec((1, s_ctx, 2 * PAIR), lambda b: (b, 0, 0)),
                     pl.BlockSpec((1, s_ctx, 2 * PAIR), lambda b: (b, 0, 0))]
        args += [rope_tabs[0], rope_tabs[1], ctx_kv[0], ctx_kv[1]]
    out_shape = [jax.ShapeDtypeStruct((bsz, t, INNER), BF16)]
    out_specs = [pl.BlockSpec((1, t, INNER), lambda b: (b, 0, 0))]
    if not rope:
        out_shape += [jax.ShapeDtypeStruct((bsz, t, PAIR), F32)] * 2
        out_specs += [pl.BlockSpec((1, t, PAIR), lambda b: (b, 0, 0))] * 2
    res = pl.pallas_call(
        functools.partial(_attn_body, t=t, s_ctx=s_ctx, rope=rope),
        out_shape=out_shape,
        grid=(bsz,),
        in_specs=in_specs,
        out_specs=out_specs,
        scratch_shapes=[pltpu.VMEM((t, INNER), BF16), pltpu.VMEM((s_all, 2 * PAIR), BF16),
                        pltpu.VMEM((s_all, 2 * PAIR), BF16)],
        compiler_params=_params(),
        name="attn",
    )(*args)
    return res


def _dn_body(*refs, t, has_init, emit_state):
    (q_ref, k_ref, v_ref, gate_ref, ba_ref, baT_ref, cwq_ref, cwk_ref, cwv_ref, cbq_ref, cbk_ref, cbv_ref,
     alog_row_ref, bias_row_ref, alog_col_ref, bias_col_ref, gnorm_ref) = refs[:17]
    pos = 17
    s0_ref = None
    if has_init:
        s0_ref = refs[pos]
        pos += 1
    o_ref = refs[pos]
    pos += 1
    st_ref = None
    if emit_state:
        st_ref = refs[pos]
        pos += 1
    qn_sc, kn_sc, vc_sc, bg_sc, bgT_sc, of_sc, ob_sc, s_sc = refs[pos:]

    rc = ROW_CHUNK
    cl = DN_CHUNK
    nc = t // cl
    nrc = t // rc
    hp = lax.Precision.HIGHEST
    nb = 2 * N_HEADS

    lane_rc = _iota((rc, PAIR), 1)
    row_t = _iota((2 * nb, rc), 0)

    def prep(c, carry):
        r0 = pl.multiple_of(c * rc, rc)
        for src, cw, cb, dst, norm in ((q_ref, cwq_ref, cbq_ref, qn_sc, True), (k_ref, cwk_ref, cbk_ref, kn_sc, True),
                                       (v_ref, cwv_ref, cbv_ref, vc_sc, False)):
            x = _silu(_dwconv(_conv_window(src, r0, c, nrc, rc, t), cw[...], cb[...], rc))
            if norm:
                tiles = []
                for p in range(N_PAIRS):
                    xp = x[:, p * PAIR:(p + 1) * PAIR]
                    tiles.append(xp * lax.rsqrt(_head_sumsq(xp) + EPS))
                x = jnp.concatenate(tiles, axis=1)
                if dst is qn_sc:
                    x = x * (HEAD ** -0.5)
            dst[pl.ds(r0, rc), :] = x
        ba = ba_ref[0, pl.ds(r0, rc), :]
        beta = jax.nn.sigmoid(ba)
        gdec = -jnp.exp(alog_row_ref[...]) * _softplus(ba + bias_row_ref[...])
        bg_sc[pl.ds(r0, rc), :] = jnp.where(lane_rc < nb, beta, gdec)
        baT = baT_ref[0, :, pl.ds(r0, rc)]
        betaT = jax.nn.sigmoid(baT)
        gdecT = -jnp.exp(alog_col_ref[...]) * _softplus(baT + bias_col_ref[...])
        bgT = jnp.where(row_t < nb, betaT, gdecT)
        for k in range(rc // cl):
            bgT_sc[c * (rc // cl) + k] = bgT[:, k * cl:(k + 1) * cl]
        return carry

    lax.fori_loop(0, nrc, prep, 0)

    rr = _iota((PAIR, PAIR), 0)
    cc = _iota((PAIR, PAIR), 1)
    same = (rr // HEAD) == (cc // HEAD)
    ii = rr % HEAD
    jj = cc % HEAD
    lane_c = _iota((cl, PAIR), 1)
    lo = _lo_tri(cl)
    up = _up_tri(cl)
    eye = (rr == cc).astype(F32)
    blk_base = same & (ii // DN_BASE == jj // DN_BASE)
    off_masks = []
    b = DN_BASE
    while b < cl:
        off_masks.append(same & (ii // (2 * b) == jj // (2 * b)) & (ii // b != jj // b))
        b *= 2

    for d in range(2):
        for p in range(N_PAIRS):
            if has_init:
                b0 = s0_ref[0, d, (2 * p) * HEAD:(2 * p + 1) * HEAD, :]
                b1 = s0_ref[0, d, (2 * p + 1) * HEAD:(2 * p + 2) * HEAD, :]
                z64 = jnp.zeros_like(b0)
                s_sc[d, p] = jnp.concatenate([jnp.concatenate([b0, z64], axis=1),
                                              jnp.concatenate([z64, b1], axis=1)], axis=0)
            else:
                s_sc[d, p] = jnp.zeros((PAIR, PAIR), F32)

    def stack2(x):
        return jnp.concatenate([jnp.where(lane_c < HEAD, x, 0.0), jnp.where(lane_c >= HEAD, x, 0.0)], axis=0)

    def col2(m, c0):
        return jnp.concatenate([m[:, c0:c0 + 1], m[:, c0 + 1:c0 + 2]], axis=0)

    systems = [(d, p) for d in range(2) for p in range(N_PAIRS)]
    offdiag = (ii != jj).astype(F32)

    def chunk(k, carry):
        dirs = []
        for d in range(2):
            c = k if d == 0 else nc - 1 - k
            r0 = pl.multiple_of(c * cl, cl)
            bg = bg_sc[pl.ds(r0, cl), :]
            bgT = bgT_sc[c]
            tri = lo if d == 0 else up
            triT = up if d == 0 else lo
            gc = jnp.dot(tri, bg, precision=hp, preferred_element_type=F32)
            gcT = jnp.dot(bgT, triT, precision=hp, preferred_element_type=F32)
            dirs.append((r0, bg, gc, gcT))
        st = []
        for d, p in systems:
            r0, bg, gc, gcT = dirs[d]
            last = cl - 1 if d == 0 else 0
            incl = (jj <= ii) if d == 0 else (jj >= ii)
            h0i = d * N_HEADS + 2 * p
            cbeta = col2(bg, h0i)
            cgc = col2(gc, nb + h0i)
            rgc = jnp.concatenate([gcT[nb + h0i:nb + h0i + 1, :], gcT[nb + h0i + 1:nb + h0i + 2, :]], axis=1)
            glast = jnp.where(rr[:, 0:1] < HEAD, gc[last:last + 1, nb + h0i:nb + h0i + 1],
                              gc[last:last + 1, nb + h0i + 1:nb + h0i + 2])
            decay = jnp.exp(jnp.where(same & incl, cgc - rgc, -jnp.inf))
            cols = pl.ds(p * PAIR, PAIR)
            ks = stack2(kn_sc[pl.ds(r0, cl), cols])
            qs = stack2(qn_sc[pl.ds(r0, cl), cols])
            vs = stack2(vc_sc[pl.ds(r0, cl), cols])
            egc = jnp.exp(cgc)
            st.append(dict(r0=r0, cols=cols, cbeta=cbeta, cgc=cgc, glast=glast, decay=decay, ks=ks, qs=qs,
                           egc=egc, rhs=jnp.concatenate([vs * cbeta, ks * (cbeta * egc)], axis=1)))
        for e in st:
            ksb = e["ks"].astype(BF16)
            e["gm"] = _dot_nt(ksb, ksb)
            e["am"] = _dot_nt(e["qs"], ksb)
        for e in st:
            e["m"] = e["gm"] * e["cbeta"] * e["decay"] * offdiag
            e["aq"] = e["am"] * e["decay"]
            e["n1"] = jnp.where(blk_base, e["m"], 0.0)
        for e in st:
            e["n2"] = _dot(e["n1"], e["n1"])
        for e in st:
            e["n4"] = _dot(e["n2"], e["n2"])
        for e in st:
            e["pm"] = eye + e["n2"] + e["n4"] + _dot(e["n2"], e["n4"])
        for e in st:
            e["x"] = e["pm"] - _dot(e["n1"], e["pm"])
        for off_mask in off_masks:
            for e in st:
                e["cx"] = _dot(jnp.where(off_mask, e["m"], 0.0), e["x"])
            for e in st:
                e["x"] = e["x"] - _dot(e["x"], e["cx"])
        for e in st:
            e["sol"] = _dot(e["x"], e["rhs"])
        for (d, p), e in zip(systems, st):
            e["s"] = s_sc[d, p]
            e["vnew"] = e["sol"][:, :PAIR] - _dot(e["sol"][:, PAIR:], e["s"])
        for (d, p), e in zip(systems, st):
            ost = _dot(e["qs"] * e["egc"], e["s"]) + _dot(e["aq"], e["vnew"])
            dst = of_sc if d == 0 else ob_sc
            dst[pl.ds(e["r0"], cl), e["cols"]] = ost[:cl] + ost[cl:]
            kd = e["ks"] * jnp.exp(e["glast"] - e["cgc"])
            s_sc[d, p] = e["s"] * jnp.exp(e["glast"]) + _dot_tn(kd, e["vnew"])
        return carry

    lax.fori_loop(0, nc, chunk, 0)

    if emit_state:
        for d in range(2):
            for p in range(N_PAIRS):
                s = s_sc[d, p]
                st_ref[0, d, (2 * p) * HEAD:(2 * p + 1) * HEAD, :] = s[:HEAD, :HEAD]
                st_ref[0, d, (2 * p + 1) * HEAD:(2 * p + 2) * HEAD, :] = s[HEAD:, HEAD:]

    def finish(c, carry):
        r0 = pl.multiple_of(c * rc, rc)
        for p in range(N_PAIRS):
            cols = pl.ds(p * PAIR, PAIR)
            o = of_sc[pl.ds(r0, rc), cols] + ob_sc[pl.ds(r0, rc), cols]
            on = o * lax.rsqrt(_head_sumsq(o) * (1.0 / HEAD) + EPS) * gnorm_ref[...]
            o_ref[0, pl.ds(r0, rc), cols] = (on * _silu(gate_ref[0, pl.ds(r0, rc), cols])).astype(BF16)
        return carry

    lax.fori_loop(0, nrc, finish, 0)


def _dn(p_dn, baT, prm, s0, *, emit_state):
    bsz, t, _ = p_dn.shape
    nc = t // DN_CHUNK
    has_init = s0 is not None
    seq = lambda w, j: pl.BlockSpec((1, t, w), lambda b: (b, 0, j), pipeline_mode=pl.Buffered(1))
    in_specs = [seq(INNER, 0), seq(INNER, 1), seq(INNER, 2), seq(INNER, 3), seq(PAIR, 16),
                pl.BlockSpec((1, 4 * N_HEADS, t), lambda b: (b, 0, 0), pipeline_mode=pl.Buffered(1))]
    args = [p_dn] * 5 + [baT]
    for name in ("cwq", "cwk", "cwv", "cbq", "cbk", "cbv", "alog_row", "bias_row", "alog_col", "bias_col", "gnorm"):
        a = prm[name]
        in_specs.append(_full(a.shape))
        args.append(a)
    if has_init:
        in_specs.append(pl.BlockSpec((1, 2, INNER, HEAD), lambda b: (b, 0, 0, 0)))
        args.append(s0)
    out_shape = [jax.ShapeDtypeStruct((bsz, t, INNER), BF16)]
    out_specs = [pl.BlockSpec((1, t, INNER), lambda b: (b, 0, 0))]
    if emit_state:
        out_shape.append(jax.ShapeDtypeStruct((bsz, 2, INNER, HEAD), F32))
        out_specs.append(pl.BlockSpec((1, 2, INNER, HEAD), lambda b: (b, 0, 0, 0)))
    res = pl.pallas_call(
        functools.partial(_dn_body, t=t, has_init=has_init, emit_state=emit_state),
        out_shape=out_shape,
        grid=(bsz,),
        in_specs=in_specs,
        out_specs=out_specs,
        scratch_shapes=[pltpu.VMEM((t, INNER), F32), pltpu.VMEM((t, INNER), F32), pltpu.VMEM((t, INNER), F32),
                        pltpu.VMEM((t, PAIR), F32), pltpu.VMEM((nc, 4 * N_HEADS, DN_CHUNK), F32),
                        pltpu.VMEM((t, INNER), F32), pltpu.VMEM((t, INNER), F32),
                        pltpu.VMEM((2, N_PAIRS, PAIR, PAIR), F32)],
        compiler_params=_params(),
        name="deltanet",
    )(*args)
    return (res[0], res[1]) if emit_state else (res[0], None)


def _lru_body(*refs, t, has_init, emit_state):
    x_ref, y_ref, cw_ref, cb_ref, wbd_ref, bias_ref, lam_ref = refs[:7]
    pos = 7
    h0_ref = None
    if has_init:
        h0_ref = refs[pos]
        pos += 1
    o_ref = refs[pos]
    pos += 1
    st_ref = None
    if emit_state:
        st_ref = refs[pos]
        pos += 1
    xl_sc, af_sc, uf_sc, ab_sc, ub_sc, hf_sc, hb_sc = refs[pos:]

    rc = ROW_CHUNK
    nrc = t // rc
    row8 = _iota((HALO, PAIR), 0)

    def conv(c, carry):
        r0 = pl.multiple_of(c * rc, rc)
        xl_sc[pl.ds(r0, rc), :] = _dwconv(_conv_window(x_ref, r0, c, nrc, rc, t), cw_ref[...], cb_ref[...], rc)
        return carry

    lax.fori_loop(0, nrc, conv, 0)

    for p in range(N_PAIRS):
        cols = pl.ds(p * PAIR, PAIR)
        sp_lam = _softplus(-lam_ref[:, p * PAIR:(p + 1) * PAIR])

        def gates(c, carry, p=p, cols=cols, sp_lam=sp_lam):
            r0 = pl.multiple_of(c * rc, rc)
            xl = xl_sc[pl.ds(r0, rc), cols]
            pre = _dot(xl, wbd_ref[p]) + bias_ref[p]
            for d, (a_sc, u_sc) in enumerate(((af_sc, uf_sc), (ab_sc, ub_sc))):
                r = jax.nn.sigmoid(pre[:, d * PAIR:(d + 1) * PAIR])
                ig = jax.nn.sigmoid(pre[:, (2 + d) * PAIR:(3 + d) * PAIR])
                log_a = -LRU_C * r * sp_lam[d:d + 1, :]
                a_sc[pl.ds(r0, rc), :] = jnp.exp(log_a)
                u_sc[pl.ds(r0, rc), :] = jnp.sqrt(-jnp.tanh(log_a) * (jnp.exp(2.0 * log_a) + 1.0)) * ig * xl
            return carry

        lax.fori_loop(0, nrc, gates, 0)

        if has_init:
            cf0 = h0_ref[0, 0:1, p * PAIR:(p + 1) * PAIR]
            cb0 = h0_ref[0, 1:2, p * PAIR:(p + 1) * PAIR]
        else:
            cf0 = jnp.zeros((1, PAIR), F32)
            cb0 = cf0

        def scan(k, carry):
            cf, cb = carry
            rf = pl.multiple_of(k * HALO, HALO)
            a8 = af_sc[pl.ds(rf, HALO), :]
            b8 = uf_sc[pl.ds(rf, HALO), :]
            for s in (1, 2, 4):
                ok = row8 >= s
                b8 = jnp.where(ok, a8 * pltpu.roll(b8, s, 0) + b8, b8)
                a8 = jnp.where(ok, a8 * pltpu.roll(a8, s, 0), a8)
            hf = a8 * cf + b8
            hf_sc[pl.ds(rf, HALO), :] = hf
            rb = pl.multiple_of(t - HALO - k * HALO, HALO)
            a8 = ab_sc[pl.ds(rb, HALO), :]
            b8 = ub_sc[pl.ds(rb, HALO), :]
            for s in (1, 2, 4):
                ok = row8 < HALO - s
                b8 = jnp.where(ok, a8 * pltpu.roll(b8, HALO - s, 0) + b8, b8)
                a8 = jnp.where(ok, a8 * pltpu.roll(a8, HALO - s, 0), a8)
            hb = a8 * cb + b8
            hb_sc[pl.ds(rb, HALO), :] = hb
            return hf[HALO - 1:HALO, :], hb[0:1, :]

        cf, cb = lax.fori_loop(0, t // HALO, scan, (cf0, cb0))
        if emit_state:
            st_ref[0, 0:1, p * PAIR:(p + 1) * PAIR] = cf
            st_ref[0, 1:2, p * PAIR:(p + 1) * PAIR] = cb

        def finish(c, carry, cols=cols):
            r0 = pl.multiple_of(c * rc, rc)
            y = y_ref[0, pl.ds(r0, rc), cols]
            gelu = 0.5 * y * (1.0 + jnp.tanh(math.sqrt(2.0 / math.pi) * (y + 0.044715 * (y * y * y))))
            o_ref[0, pl.ds(r0, rc), cols] = ((hf_sc[pl.ds(r0, rc), :] + hb_sc[pl.ds(r0, rc), :]) * gelu).astype(BF16)
            return carry

        lax.fori_loop(0, nrc, finish, 0)


def _lru(p_lru, prm, h0, *, emit_state):
    bsz, t, _ = p_lru.shape
    has_init = h0 is not None
    seq = lambda w, j: pl.BlockSpec((1, t, w), lambda b: (b, 0, j), pipeline_mode=pl.Buffered(1))
    in_specs = [seq(INNER, 0), seq(INNER, 1)]
    args = [p_lru, p_lru]
    for name in ("cw", "cb", "wbd", "bias", "lam"):
        a = prm[name]
        in_specs.append(_full(a.shape))
        args.append(a)
    if has_init:
        in_specs.append(pl.BlockSpec((1, 2, INNER), lambda b: (b, 0, 0)))
        args.append(h0)
    out_shape = [jax.ShapeDtypeStruct((bsz, t, INNER), BF16)]
    out_specs = [pl.BlockSpec((1, t, INNER), lambda b: (b, 0, 0))]
    if emit_state:
        out_shape.append(jax.ShapeDtypeStruct((bsz, 2, INNER), F32))
        out_specs.append(pl.BlockSpec((1, 2, INNER), lambda b: (b, 0, 0)))
    res = pl.pallas_call(
        functools.partial(_lru_body, t=t, has_init=has_init, emit_state=emit_state),
        out_shape=out_shape,
        grid=(bsz,),
        in_specs=in_specs,
        out_specs=out_specs,
        scratch_shapes=[pltpu.VMEM((t, INNER), F32)] + [pltpu.VMEM((t, PAIR), F32)] * 6,
        compiler_params=_params(),
        name="rglru",
    )(*args)
    return (res[0], res[1]) if emit_state else (res[0], None)


def _pad_cols(x, n):
    return jnp.pad(x, ((0, 0), (0, n - x.shape[1])))


def _layer_params(l, w_in, ssm_conv_w, ssm_conv_b, ssm_a_log, ssm_dt_bias, ssm_d, ssm_norm, attn_q_norm, attn_k_norm,
                  dn_conv_w, dn_conv_b, dn_a_log, dn_dt_bias, dn_norm, lru_conv_w, lru_conv_b, lru_w_a, lru_b_a,
                  lru_w_i, lru_b_i, lru_lambda):
    w = w_in[l]
    o = 0

    def take(n):
        nonlocal o
        part = w[:, o:o + n]
        o += n
        return part

    z_a, xs_a, bc_a, dt_a = take(INNER), take(INNER), take(2 * PAIR), take(2 * N_HEADS)
    q_b, k_b, v_b = take(INNER), take(PAIR), take(PAIR)
    q_c, k_c, v_c = take(INNER), take(INNER), take(INNER)
    beta_c, a_c, gate_c = take(2 * N_HEADS), take(2 * N_HEADS), take(INNER)
    x_d, y_d = take(INNER), take(INNER)
    gate_raw = take(N_BRANCH * D_MODEL)

    def dup(x):
        return jnp.concatenate([x[:, :HEAD], x[:, :HEAD], x[:, HEAD:], x[:, HEAD:]], axis=1)

    nh2 = 2 * N_HEADS
    prm = {
        "w_ssd": _pad_cols(jnp.concatenate([z_a, xs_a, bc_a, dt_a], axis=1), SSD_N).astype(BF16),
        "w_ssdT": dt_a.T.astype(BF16),
        "w_att": jnp.concatenate([q_b, dup(k_b), dup(v_b)], axis=1).astype(BF16),
        "w_dn": _pad_cols(jnp.concatenate([q_c, k_c, v_c, gate_c, beta_c, a_c], axis=1), DN_N).astype(BF16),
        "w_dnT": jnp.concatenate([beta_c, a_c], axis=1).T.astype(BF16),
        "w_lru": jnp.concatenate([x_d, y_d], axis=1).astype(BF16),
        "w_gate": gate_raw.astype(BF16),
    }
    cw, cb = ssm_conv_w[l], ssm_conv_b[l][None, :]
    dtb = ssm_dt_bias[l].reshape(1, nh2)
    alog = ssm_a_log[l].reshape(1, nh2)
    prm["ssd"] = {
        "cwx": cw[:, :INNER], "cbx": cb[:, :INNER], "cwbc": cw[:, INNER:], "cbbc": cb[:, INNER:],
        "dtb_row": _pad_cols(dtb, PAIR), "dtb_col": jnp.broadcast_to(dtb.T, (nh2, SSD_CHUNK)),
        "alog_row": _pad_cols(alog, PAIR), "alog_col": jnp.broadcast_to(alog.T, (nh2, SSD_CHUNK)),
        "dskip": jnp.repeat(ssm_d[l], HEAD)[None, :], "gnorm": ssm_norm[l][None, :],
    }
    prm["att"] = {"qgain": jnp.tile(attn_q_norm[l], 2)[None, :], "kgain": jnp.tile(attn_k_norm[l], 2)[None, :]}
    dcw, dcb = dn_conv_w[l], dn_conv_b[l][None, :]
    zeros16 = jnp.zeros((1, nh2), F32)
    d_alog = jnp.concatenate([zeros16, dn_a_log[l].reshape(1, nh2)], axis=1)
    d_bias = jnp.concatenate([zeros16, dn_dt_bias[l].reshape(1, nh2)], axis=1)
    prm["dn"] = {
        "cwq": dcw[:, :INNER], "cwk": dcw[:, INNER:2 * INNER], "cwv": dcw[:, 2 * INNER:],
        "cbq": dcb[:, :INNER], "cbk": dcb[:, INNER:2 * INNER], "cbv": dcb[:, 2 * INNER:],
        "alog_row": _pad_cols(d_alog, PAIR), "bias_row": _pad_cols(d_bias, PAIR),
        "alog_col": jnp.broadcast_to(d_alog.T, (2 * nh2, ROW_CHUNK)),
        "bias_col": jnp.broadcast_to(d_bias.T, (2 * nh2, ROW_CHUNK)),
        "gnorm": jnp.tile(dn_norm[l], 2)[None, :],
    }
    wa, wi = lru_w_a[l], lru_w_i[l]
    z64 = jnp.zeros((HEAD, HEAD), F32)
    wbd, bias = [], []
    for p in range(N_PAIRS):
        blocks = []
        for wsrc in (wa, wi):
            for d in range(2):
                top = jnp.concatenate([wsrc[d, 2 * p], z64], axis=1)
                bot = jnp.concatenate([z64, wsrc[d, 2 * p + 1]], axis=1)
                blocks.append(jnp.concatenate([top, bot], axis=0))
        wbd.append(jnp.concatenate(blocks, axis=1))
        bias.append(jnp.concatenate([lru_b_a[l][0, p * PAIR:(p + 1) * PAIR], lru_b_a[l][1, p * PAIR:(p + 1) * PAIR],
                                     lru_b_i[l][0, p * PAIR:(p + 1) * PAIR], lru_b_i[l][1, p * PAIR:(p + 1) * PAIR]])[None, :])
    prm["lru"] = {"cw": lru_conv_w[l], "cb": lru_conv_b[l][None, :], "wbd": jnp.stack(wbd).astype(BF16),
                  "bias": jnp.stack(bias), "lam": lru_lambda[l]}
    return prm


def _rope_tables(t):
    n_freq = HEAD // 4
    inv = ROPE_THETA ** (-jnp.arange(n_freq, dtype=F32) / n_freq)
    rows = t // GRID_W
    row = jnp.repeat(jnp.arange(rows, dtype=F32), GRID_W)
    col = jnp.tile(jnp.arange(GRID_W, dtype=F32), rows)
    ang_r = row[:, None] * inv
    ang_c = col[:, None] * inv
    ang = jnp.concatenate([ang_r, ang_r, ang_c, ang_c], axis=1)
    sign = jnp.tile(jnp.concatenate([-jnp.ones((n_freq,), F32), jnp.ones((n_freq,), F32)]), 2)
    cos = jnp.cos(ang)
    sin = jnp.sin(ang) * sign
    return jnp.tile(cos, (1, 2)), jnp.tile(sin, (1, 2))


def _dup_kv(x):
    return jnp.concatenate([x[:, :, 0], x[:, :, 0], x[:, :, 1], x[:, :, 1]], axis=-1)


def _trunk_layer(h, mod, lw, prm, rope_tabs, ctx, fin_gain, *, final):
    h = _ffn(h, mod, lw["norm_ffn1"], lw["ffn1_w13"], lw["ffn1_w2"], fin_gain, rows=(0, 1, 2), final=False)
    gain = lw["norm_mix"]
    emit = ctx is None
    p_ssd, dtT = _proj(h, mod, gain, prm["w_ssd"], prm["w_ssdT"])
    p_att = _proj(h, mod, gain, prm["w_att"])
    p_dn, baT = _proj(h, mod, gain, prm["w_dn"], prm["w_dnT"])
    p_lru = _proj(h, mod, gain, prm["w_lru"])
    if emit:
        o_a, st_ssm = _ssd(p_ssd, dtT, prm["ssd"], None, emit_state=True)
        o_b, k_new, v_new = _attn(p_att, prm["att"], None, None)
        o_c, st_dn = _dn(p_dn, baT, prm["dn"], None, emit_state=True)
        o_d, st_lru = _lru(p_lru, prm["lru"], None, emit_state=True)
        new_ctx = (k_new, v_new, st_ssm, st_dn, st_lru)
    else:
        ck, cv, ssm0, dn0, lru0 = ctx
        o_a, _ = _ssd(p_ssd, dtT, prm["ssd"], ssm0, emit_state=False)
        (o_b,) = _attn(p_att, prm["att"], rope_tabs, (ck, cv))
        o_c, _ = _dn(p_dn, baT, prm["dn"], dn0, emit_state=False)
        o_d, _ = _lru(p_lru, prm["lru"], lru0, emit_state=False)
        new_ctx = None
    h = _merge(h, mod, gain, (o_a, o_b, o_c, o_d), prm["w_gate"], lw["w_branch"], lw["w_out"])
    h = _ffn(h, mod, lw["norm_ffn2"], lw["ffn2_w13"], lw["ffn2_w2"], fin_gain, rows=(6, 7, 8), final=final)
    return h, new_ctx


def kernel(x_prompt, x_sample, cache_k, cache_v, state_ssm, state_delta, state_lru, c, c_ctx,
           w_ada, b_ada, norm_ffn1, ffn1_w13, ffn1_w2, norm_mix, w_in,
           ssm_conv_w, ssm_conv_b, ssm_a_log, ssm_dt_bias, ssm_d, ssm_norm,
           attn_q_norm, attn_k_norm,
           dn_conv_w, dn_conv_b, dn_a_log, dn_dt_bias, dn_norm,
           lru_conv_w, lru_conv_b, lru_w_a, lru_b_a, lru_w_i, lru_b_i, lru_lambda,
           w_branch, w_out, norm_ffn2, ffn2_w13, ffn2_w2, final_norm):
    depth = w_in.shape[0]
    bsz_p, t_p, _ = x_prompt.shape
    bsz_s, t_s, _ = x_sample.shape
    assert bsz_s + 1 <= MOD_ROWS
    cvec = jnp.concatenate([c, c_ctx[None], jnp.zeros((MOD_ROWS - bsz_s - 1, D_MODEL), F32)], axis=0)
    mod_all = _adaln(cvec, w_ada, b_ada)
    mod_all = mod_all.reshape(depth, MOD_ROWS, N_MOD, D_MODEL)
    mod_all = jnp.pad(mod_all, ((0, 0), (0, 0), (0, MOD_ROWS - N_MOD), (0, 0)))
    rope_tabs = _rope_tables(t_s)
    fin_gain = final_norm[None, :]

    hp, hs = x_prompt, x_sample
    ks, vs, ssm_s, dn_s, lru_s = [], [], [], [], []
    for l in range(depth):
        prm = _layer_params(l, w_in, ssm_conv_w, ssm_conv_b, ssm_a_log, ssm_dt_bias, ssm_d, ssm_norm,
                            attn_q_norm, attn_k_norm, dn_conv_w, dn_conv_b, dn_a_log, dn_dt_bias, dn_norm,
                            lru_conv_w, lru_conv_b, lru_w_a, lru_b_a, lru_w_i, lru_b_i, lru_lambda)
        lw = {
            "norm_ffn1": norm_ffn1[l][None, :], "ffn1_w13": ffn1_w13[l].astype(BF16), "ffn1_w2": ffn1_w2[l].astype(BF16),
            "norm_mix": norm_mix[l][None, :], "w_branch": w_branch[l].astype(BF16), "w_out": w_out[l].astype(BF16),
            "norm_ffn2": norm_ffn2[l][None, :], "ffn2_w13": ffn2_w13[l].astype(BF16), "ffn2_w2": ffn2_w2[l].astype(BF16),
        }
        final = l == depth - 1
        mod_lat = mod_all[l, :bsz_s]
        mod_ctx = mod_all[l, bsz_s:bsz_s + 1]
        hp, st = _trunk_layer(hp, mod_ctx, lw, prm, None, None, fin_gain, final=final)
        k_c, v_c, st_ssm, st_dn, st_lru = st
        ks.append(k_c.reshape(bsz_p, t_p, KV_HEADS, HEAD))
        vs.append(v_c.reshape(bsz_p, t_p, KV_HEADS, HEAD))
        ssm_s.append(st_ssm.reshape(bsz_p, 2, N_HEADS, HEAD, HEAD))
        dn_s.append(st_dn.reshape(bsz_p, 2, N_HEADS, HEAD, HEAD))
        lru_s.append(st_lru)
        ctx_l = (_dup_kv(cache_k[:, l]), _dup_kv(cache_v[:, l]),
                 state_ssm[:, l].reshape(bsz_s, 2, INNER, HEAD), state_delta[:, l].reshape(bsz_s, 2, INNER, HEAD),
                 state_lru[:, l])
        hs, _ = _trunk_layer(hs, mod_lat, lw, prm, rope_tabs, ctx_l, fin_gain, final=final)
    return (hp, hs, jnp.stack(ks, axis=1), jnp.stack(vs, axis=1), jnp.stack(ssm_s, axis=1),
            jnp.stack(dn_s, axis=1), jnp.stack(lru_s, axis=1))
```

```python
import functools
import math

import jax
import jax.numpy as jnp
from jax import lax
from jax.experimental import pallas as pl
from jax.experimental.pallas import tpu as pltpu

F32 = jnp.float32
BF16 = jnp.bfloat16

D_MODEL = 1024
D_FF = 2816
N_MOD = 9
MOD_ROWS = 16
EPS = 1e-6
GRID_W = 64
CONV_W = 4
CONV_LP = CONV_W // 2
HALO = 8
HEAD = 64
PAIR = 2 * HEAD
N_HEADS = 8
N_PAIRS = N_HEADS // 2
INNER = N_HEADS * HEAD
KV_HEADS = 2
SSM_GROUPS = 2
ROPE_THETA = 10000.0
LRU_C = 8.0
N_BRANCH = 4
SSD_CHUNK = 256
DN_CHUNK = 64
DN_BASE = 8
ROW_CHUNK = 256
ATT_TQ = 128
FF_CHUNK = 256
VMEM_LIMIT = 56 * 1024 * 1024

COL_Z, COL_XS, COL_QB, COL_QC, COL_KC, COL_VC, COL_GC, COL_XD, COL_YD = range(9)
COL_BC, COL_KK, COL_VV = 18, 19, 20
COL_DT, COL_BA = 42, 43
PROJ_N = 44 * PAIR
ROWT_BA, ROWT_DT = 0, 2
PROJ_T = 48


def _dot(a, b):
    return jnp.dot(a.astype(BF16), b.astype(BF16), preferred_element_type=F32)


def _dot_nt(a, b):
    return lax.dot_general(a.astype(BF16), b.astype(BF16), (((1,), (1,)), ((), ())),
                           preferred_element_type=F32)


def _dot_tn(a, b):
    return lax.dot_general(a.astype(BF16), b.astype(BF16), (((0,), (0,)), ((), ())),
                           preferred_element_type=F32)


def _split3(x):
    hi = x.astype(BF16)
    r1 = x - hi.astype(F32)
    mid = r1.astype(BF16)
    lo = (r1 - mid.astype(F32)).astype(BF16)
    return hi, mid, lo


def _dot_sel(x, sel):
    hi, mid, lo = _split3(x)
    s = sel.astype(BF16)
    return (jnp.dot(hi, s, preferred_element_type=F32) + jnp.dot(mid, s, preferred_element_type=F32)
            + jnp.dot(lo, s, preferred_element_type=F32))


def _sel_dot(sel, x):
    hi, mid, lo = _split3(x)
    s = sel.astype(BF16)
    return (jnp.dot(s, hi, preferred_element_type=F32) + jnp.dot(s, mid, preferred_element_type=F32)
            + jnp.dot(s, lo, preferred_element_type=F32))


def _iota(shape, axis):
    return lax.broadcasted_iota(jnp.int32, shape, axis)


def _silu(x):
    return x * jax.nn.sigmoid(x)


def _softplus(x):
    return jnp.maximum(x, 0.0) + jnp.log1p(jnp.exp(-jnp.abs(x)))


def _rms(x, g):
    return x * lax.rsqrt(jnp.mean(x * x, axis=-1, keepdims=True) + EPS) * g


def _head_ones():
    return (_iota((PAIR, PAIR), 0) // HEAD == _iota((PAIR, PAIR), 1) // HEAD).astype(F32)


def _head_sumsq(x):
    return _dot_sel(x * x, _head_ones())


def _lo_tri(n):
    return (_iota((n, n), 1) <= _iota((n, n), 0)).astype(F32)


def _up_tri(n):
    return (_iota((n, n), 1) >= _iota((n, n), 0)).astype(F32)


def _conv_window(ref, r0, c, nchunks, rows, t_total):
    cur = ref[0, pl.ds(r0, rows), :]
    prev_start = pl.multiple_of(jnp.maximum(r0 - HALO, 0), HALO)
    next_start = pl.multiple_of(jnp.minimum(r0 + rows, t_total - HALO), HALO)
    prev = jnp.where(c > 0, ref[0, pl.ds(prev_start, HALO), :], 0.0)
    nxt = jnp.where(c < nchunks - 1, ref[0, pl.ds(next_start, HALO), :], 0.0)
    return jnp.concatenate([prev, cur, nxt], axis=0)


def _dwconv(ext, w, b, rows):
    out = b
    for j in range(CONV_W):
        off = HALO - CONV_LP + j
        out = out + w[j:j + 1, :] * ext[off:off + rows, :]
    return out


def _expand_heads(n_src_rows, first, width):
    r = _iota((n_src_rows, width), 0)
    l = _iota((n_src_rows, width), 1)
    return (r - first == l // HEAD).astype(F32)


def _full(shape):
    zeros = (0,) * len(shape)
    return pl.BlockSpec(shape, lambda *_: zeros, pipeline_mode=pl.Buffered(1))


def _params():
    return pltpu.CompilerParams(vmem_limit_bytes=VMEM_LIMIT)


def _adaln_body(c_ref, w_ref, b_ref, o_ref):
    o_ref[0] = _dot(_silu(c_ref[...]), w_ref[0]) + b_ref[0]


def _adaln(cvec, w_ada, b_ada):
    depth = w_ada.shape[0]
    n = w_ada.shape[2]
    tn = n // N_MOD
    return pl.pallas_call(
        _adaln_body,
        out_shape=jax.ShapeDtypeStruct((depth, MOD_ROWS, n), F32),
        grid=(depth, N_MOD),
        in_specs=[pl.BlockSpec((MOD_ROWS, D_MODEL), lambda l, j: (0, 0)),
                  pl.BlockSpec((1, D_MODEL, tn), lambda l, j: (l, 0, j)),
                  pl.BlockSpec((1, 1, tn), lambda l, j: (l, 0, j))],
        out_specs=pl.BlockSpec((1, MOD_ROWS, tn), lambda l, j: (l, 0, j)),
        compiler_params=_params(),
        name="adaln",
    )(cvec, w_ada, b_ada.reshape(depth, 1, n))


def _ffn_body(h_ref, mod_ref, gain_ref, w13_ref, w2_ref, fin_ref, o_ref, acc_sc, *, rows, final):
    shift, scale, gate = rows
    ms = mod_ref[0]
    h = h_ref[0]
    xn = (_rms(h, gain_ref[...]) * (1.0 + ms[scale:scale + 1]) + ms[shift:shift + 1]).astype(BF16)
    for j in range(D_FF // FF_CHUNK):
        lo, hi = j * FF_CHUNK, (j + 1) * FF_CHUNK
        g = jnp.dot(xn, w13_ref[:, lo:hi], preferred_element_type=F32)
        u = jnp.dot(xn, w13_ref[:, D_FF + lo:D_FF + hi], preferred_element_type=F32)
        part = _dot(_silu(g) * u, w2_ref[lo:hi, :])
        if j == 0:
            acc_sc[...] = part
        else:
            acc_sc[...] += part
    hn = h + 0.5 * ms[gate:gate + 1] * acc_sc[...]
    o_ref[0] = _rms(hn, fin_ref[...]) if final else hn


def _ffn(h, mod, gain, w13, w2, fin_gain, *, rows, final):
    bsz, t, d = h.shape
    tm = min(512, t)
    mb = mod.shape[0]
    mod_ix = (lambda b, i: (b, 0, 0)) if mb > 1 else (lambda b, i: (0, 0, 0))
    return pl.pallas_call(
        functools.partial(_ffn_body, rows=rows, final=final),
        out_shape=jax.ShapeDtypeStruct((bsz, t, d), F32),
        grid=(bsz, t // tm),
        in_specs=[pl.BlockSpec((1, tm, d), lambda b, i: (b, i, 0)),
                  pl.BlockSpec((1, MOD_ROWS, d), mod_ix),
                  pl.BlockSpec((1, d), lambda b, i: (0, 0)),
                  _full((d, 2 * D_FF)), _full((D_FF, d)),
                  pl.BlockSpec((1, d), lambda b, i: (0, 0))],
        out_specs=pl.BlockSpec((1, tm, d), lambda b, i: (b, i, 0)),
        scratch_shapes=[pltpu.VMEM((tm, d), F32)],
        compiler_params=_params(),
        name="ffn",
    )(h, mod, gain, w13, w2, fin_gain)


def _proj_body(h_ref, mod_ref, gain_ref, w_ref, wt_ref, o_ref, ot_ref):
    ms = mod_ref[0]
    xn = (_rms(h_ref[0], gain_ref[...]) * (1.0 + ms[4:5]) + ms[3:4]).astype(BF16)
    ot_ref[0] = lax.dot_general(wt_ref[...], xn, (((1,), (1,)), ((), ())), preferred_element_type=F32)
    o_ref[0] = jnp.dot(xn, w_ref[...], preferred_element_type=F32)


def _proj(h, mod, gain, w, wt):
    bsz, t, d = h.shape
    n = w.shape[1]
    r = wt.shape[0]
    tm = min(256, t)
    mb = mod.shape[0]
    mod_ix = (lambda b, i: (b, 0, 0)) if mb > 1 else (lambda b, i: (0, 0, 0))
    return pl.pallas_call(
        _proj_body,
        out_shape=[jax.ShapeDtypeStruct((bsz, t, n), F32), jax.ShapeDtypeStruct((bsz, r, t), F32)],
        grid=(bsz, t // tm),
        in_specs=[pl.BlockSpec((1, tm, d), lambda b, i: (b, i, 0)),
                  pl.BlockSpec((1, MOD_ROWS, d), mod_ix),
                  pl.BlockSpec((1, d), lambda b, i: (0, 0)),
                  _full((d, n)), _full((r, d))],
        out_specs=[pl.BlockSpec((1, tm, n), lambda b, i: (b, i, 0)),
                   pl.BlockSpec((1, r, tm), lambda b, i: (b, 0, i))],
        compiler_params=_params(),
        name="proj",
    )(h, mod, gain, w, wt)


def _merge_body(h_ref, mod_ref, gain_ref, oa_ref, ob_ref, oc_ref, od_ref, wg_ref, wb_ref, wo_ref, o_ref):
    ms = mod_ref[0]
    h = h_ref[0]
    xn = (_rms(h, gain_ref[...]) * (1.0 + ms[4:5]) + ms[3:4]).astype(BF16)
    merged = None
    for n, br_ref in enumerate((oa_ref, ob_ref, oc_ref, od_ref)):
        gate = jax.nn.sigmoid(jnp.dot(xn, wg_ref[:, n * D_MODEL:(n + 1) * D_MODEL], preferred_element_type=F32))
        term = gate * jnp.dot(br_ref[0], wb_ref[n], preferred_element_type=F32)
        merged = term if merged is None else merged + term
    o_ref[0] = h + ms[5:6] * _dot(merged, wo_ref[...])


def _merge(h, mod, gain, branches, w_gate, w_branch, w_out):
    bsz, t, d = h.shape
    tm = min(512, t)
    mb = mod.shape[0]
    mod_ix = (lambda b, i: (b, 0, 0)) if mb > 1 else (lambda b, i: (0, 0, 0))
    br_spec = pl.BlockSpec((1, tm, INNER), lambda b, i: (b, i, 0))
    return pl.pallas_call(
        _merge_body,
        out_shape=jax.ShapeDtypeStruct((bsz, t, d), F32),
        grid=(bsz, t // tm),
        in_specs=[pl.BlockSpec((1, tm, d), lambda b, i: (b, i, 0)),
                  pl.BlockSpec((1, MOD_ROWS, d), mod_ix),
                  pl.BlockSpec((1, d), lambda b, i: (0, 0)),
                  br_spec, br_spec, br_spec, br_spec,
                  _full((d, N_BRANCH * d)), _full((N_BRANCH, INNER, d)), _full((d, d))],
        out_specs=pl.BlockSpec((1, tm, d), lambda b, i: (b, i, 0)),
        compiler_params=_params(),
        name="merge",
    )(h, mod, gain, *branches, w_gate, w_branch, w_out)


def _ssd_body(*refs, t, has_init, emit_state):
    (z_ref, xs_ref, bc_ref, dt_ref, dtT_ref, cwx_ref, cbx_ref, cwbc_ref, cbbc_ref,
     dtb_row_ref, dtb_col_ref, alog_row_ref, alog_col_ref, dskip_ref, gnorm_ref) = refs[:15]
    pos = 15
    h0_ref = None
    if has_init:
        h0_ref = refs[pos]
        pos += 1
    o_ref = refs[pos]
    pos += 1
    st_ref = None
    if emit_state:
        st_ref = refs[pos]
        pos += 1
    y_sc, cum_sc, cm_sc, dh_sc, tot_sc, hst_sc = refs[pos:]

    cl = SSD_CHUNK
    nc = t // cl
    lane = _iota((cl, PAIR), 1)
    ii = _iota((cl, cl), 0)
    jj = _iota((cl, cl), 1)
    lo = _lo_tri(cl)
    up = _up_tri(cl)
    hp = lax.Precision.HIGHEST
    a_row = -jnp.exp(alog_row_ref[...])
    a_col = -jnp.exp(alog_col_ref[...])
    sel_f = _expand_heads(PAIR, 0, INNER)
    sel_b = _expand_heads(PAIR, N_HEADS, INNER)

    def intra(c, carry):
        r0 = pl.multiple_of(c * cl, cl)
        xs = _silu(_dwconv(_conv_window(xs_ref, r0, c, nc, cl, t), cwx_ref[...], cbx_ref[...], cl))
        bc = _silu(_dwconv(_conv_window(bc_ref, r0, c, nc, cl, t), cwbc_ref[...], cbbc_ref[...], cl))
        bm = bc[:, :PAIR]
        cm = bc[:, PAIR:]
        dt = _softplus(dt_ref[0, pl.ds(r0, cl), :] + dtb_row_ref[...])
        dtT = _softplus(dtT_ref[0, :, pl.ds(r0, cl)] + dtb_col_ref[...])
        da = dt * a_row
        daT = dtT * a_col
        cum = jnp.where(lane < N_HEADS, jnp.dot(lo, da, precision=hp, preferred_element_type=F32),
                        jnp.dot(up, da, precision=hp, preferred_element_type=F32))
        rowsel = _iota((2 * N_HEADS, cl), 0) < N_HEADS
        cumT = jnp.where(rowsel, jnp.dot(daT, up, precision=hp, preferred_element_type=F32),
                         jnp.dot(daT, lo, precision=hp, preferred_element_type=F32))
        cum_sc[pl.ds(r0, cl), :] = cum
        cm_sc[pl.ds(r0, cl), :] = cm
        cb = []
        for g in range(SSM_GROUPS):
            cg = jnp.where(lane // HEAD == g, cm, 0.0)
            cb.append(_dot_nt(cg, bm))
        ypairs = []
        for p in range(N_PAIRS):
            xp = xs[:, p * PAIR:(p + 1) * PAIR].astype(BF16)
            halves = []
            for a in range(2):
                h = 2 * p + a
                g = h // (N_HEADS // SSM_GROUPS)
                hb = N_HEADS + h
                sf = jnp.exp(jnp.where(jj <= ii, cum[:, h:h + 1] - cumT[h:h + 1, :], -jnp.inf)) * dtT[h:h + 1, :]
                sb = jnp.exp(jnp.where(jj >= ii, cum[:, hb:hb + 1] - cumT[hb:hb + 1, :], -jnp.inf)) * dtT[hb:hb + 1, :]
                s = (cb[g] * (sf + sb)).astype(BF16)
                halves.append(jnp.dot(s, xp, preferred_element_type=F32))
            ypairs.append(jnp.where(lane < HEAD, halves[0], halves[1]))
        y = jnp.concatenate(ypairs, axis=1) + dskip_ref[...] * xs
        y_sc[pl.ds(r0, cl), :] = y
        tot = jnp.where(lane[0:1] < N_HEADS, cum[cl - 1:cl, :], cum[0:1, :])
        tot_sc[c] = jnp.broadcast_to(tot, (HALO, PAIR))
        wexp = jnp.exp(tot - cum) * dt
        for d, sel in enumerate((sel_f, sel_b)):
            xw = xs * _dot_sel(wexp, sel)
            for p in range(N_PAIRS):
                g = p // (N_PAIRS // SSM_GROUPS)
                bg = jnp.where(lane // HEAD == g, bm, 0.0)
                dh_sc[c, d, p] = _dot_tn(xw[:, p * PAIR:(p + 1) * PAIR], bg)
        return carry

    lax.fori_loop(0, nc, intra, 0)

    rr = _iota((PAIR, PAIR), 0)
    for d in range(2):
        sel = sel_f if d == 0 else sel_b
        for p in range(N_PAIRS):
            g = p // (N_PAIRS // SSM_GROUPS)
            if has_init:
                blk = h0_ref[0, d, p * PAIR:(p + 1) * PAIR, :]
                z64 = jnp.zeros_like(blk)
                hst_sc[p] = jnp.concatenate([blk, z64] if g == 0 else [z64, blk], axis=1)
            else:
                hst_sc[p] = jnp.zeros((PAIR, PAIR), F32)

        def inter(k, carry, d=d, sel=sel):
            c = k if d == 0 else nc - 1 - k
            r0 = pl.multiple_of(c * cl, cl)
            ecum = _dot_sel(jnp.exp(cum_sc[pl.ds(r0, cl), :]), sel)
            cm = cm_sc[pl.ds(r0, cl), :]
            tot = tot_sc[c]
            dec = jnp.exp(tot[0:1, :])
            for p in range(N_PAIRS):
                g = p // (N_PAIRS // SSM_GROUPS)
                cg = jnp.where(lane // HEAD == g, cm, 0.0)
                hs = hst_sc[p]
                yi = _dot_nt(cg, hs)
                cols = pl.ds(p * PAIR, PAIR)
                y_sc[pl.ds(r0, cl), cols] = y_sc[pl.ds(r0, cl), cols] + yi * ecum[:, p * PAIR:(p + 1) * PAIR]
                h0i = d * N_HEADS + 2 * p
                dcol = jnp.where(rr < HEAD, dec[:, h0i:h0i + 1], dec[:, h0i + 1:h0i + 2])
                hst_sc[p] = hs * dcol + dh_sc[c, d, p]
            return carry

        lax.fori_loop(0, nc, inter, 0)
        if emit_state:
            for p in range(N_PAIRS):
                g = p // (N_PAIRS // SSM_GROUPS)
                st_ref[0, d, p * PAIR:(p + 1) * PAIR, :] = hst_sc[p][:, g * HEAD:(g + 1) * HEAD]

    def finish(c, carry):
        r0 = pl.multiple_of(c * cl, cl)
        v = y_sc[pl.ds(r0, cl), :] * _silu(z_ref[0, pl.ds(r0, cl), :])
        o_ref[0, pl.ds(r0, cl), :] = _rms(v, gnorm_ref[...]).astype(BF16)
        return carry

    lax.fori_loop(0, nc, finish, 0)


def _ssd(proj, projT, prm, h0, *, emit_state):
    bsz, t, _ = proj.shape
    nc = t // SSD_CHUNK
    has_init = h0 is not None
    seq = lambda w, j: pl.BlockSpec((1, t, w), lambda b: (b, 0, j), pipeline_mode=pl.Buffered(1))
    in_specs = [seq(INNER, COL_Z), seq(INNER, COL_XS), seq(2 * PAIR, COL_BC), seq(PAIR, COL_DT),
                pl.BlockSpec((1, 2 * N_HEADS, t), lambda b: (b, ROWT_DT, 0), pipeline_mode=pl.Buffered(1))]
    args = [proj, proj, proj, proj, projT]
    for name in ("cwx", "cbx", "cwbc", "cbbc", "dtb_row", "dtb_col", "alog_row", "alog_col", "dskip", "gnorm"):
        a = prm[name]
        in_specs.append(_full(a.shape))
        args.append(a)
    if has_init:
        in_specs.append(pl.BlockSpec((1, 2, INNER, HEAD), lambda b: (b, 0, 0, 0)))
        args.append(h0)
    out_shape = [jax.ShapeDtypeStruct((bsz, t, INNER), BF16)]
    out_specs = [pl.BlockSpec((1, t, INNER), lambda b: (b, 0, 0))]
    if emit_state:
        out_shape.append(jax.ShapeDtypeStruct((bsz, 2, INNER, HEAD), F32))
        out_specs.append(pl.BlockSpec((1, 2, INNER, HEAD), lambda b: (b, 0, 0, 0)))
    res = pl.pallas_call(
        functools.partial(_ssd_body, t=t, has_init=has_init, emit_state=emit_state),
        out_shape=out_shape,
        grid=(bsz,),
        in_specs=in_specs,
        out_specs=out_specs,
        scratch_shapes=[pltpu.VMEM((t, INNER), F32), pltpu.VMEM((t, PAIR), F32), pltpu.VMEM((t, PAIR), F32),
                        pltpu.VMEM((nc, 2, N_PAIRS, PAIR, PAIR), F32), pltpu.VMEM((nc, HALO, PAIR), F32),
                        pltpu.VMEM((N_PAIRS, PAIR, PAIR), F32)],
        compiler_params=_params(),
        name="ssd",
    )(*args)
    return (res[0], res[1]) if emit_state else (res[0], None)


def _attn_body(*refs, t, s_ctx, rope):
    q_ref, kk_ref, vv_ref, qg_ref, kg_ref = refs[:5]
    pos = 5
    if rope:
        cos_ref, sin_ref, ck_ref, cv_ref = refs[pos:pos + 4]
        pos += 4
    o_ref = refs[pos]
    pos += 1
    if not rope:
        knew_ref, vnew_ref = refs[pos:pos + 2]
        pos += 2
    q_sc, k_sc, v_sc = refs[pos:]

    rc = ROW_CHUNK
    lane = _iota((rc, PAIR), 1)
    first_half = (lane % (HEAD // 2)) < HEAD // 4

    def rot(x, cos, sin):
        partner = jnp.where(first_half, pltpu.roll(x, PAIR - HEAD // 4, 1), pltpu.roll(x, HEAD // 4, 1))
        return x * cos + partner * sin

    def normed(x, gain):
        return x * lax.rsqrt(_head_sumsq(x) * (1.0 / HEAD) + EPS) * gain

    if rope:
        k_sc[0:s_ctx, :] = ck_ref[0].astype(BF16)
        v_sc[0:s_ctx, :] = cv_ref[0].astype(BF16)

    def prep(c, carry):
        r0 = pl.multiple_of(c * rc, rc)
        if rope:
            cos = cos_ref[pl.ds(r0, rc), :]
            sin = sin_ref[pl.ds(r0, rc), :]
        for p in range(N_PAIRS):
            x = normed(q_ref[0, pl.ds(r0, rc), p * PAIR:(p + 1) * PAIR], qg_ref[...])
            if rope:
                x = rot(x, cos, sin)
            q_sc[pl.ds(r0, rc), p * PAIR:(p + 1) * PAIR] = (x * (HEAD ** -0.5)).astype(BF16)
        kn = []
        for g in range(KV_HEADS):
            x = normed(kk_ref[0, pl.ds(r0, rc), g * PAIR:(g + 1) * PAIR], kg_ref[...])
            kn.append(x)
            if rope:
                x = rot(x, cos, sin)
            k_sc[pl.ds(s_ctx + r0, rc), g * PAIR:(g + 1) * PAIR] = x.astype(BF16)
        vv = vv_ref[0, pl.ds(r0, rc), :]
        v_sc[pl.ds(s_ctx + r0, rc), :] = vv.astype(BF16)
        if not rope:
            knew_ref[0, pl.ds(r0, rc), :] = jnp.where(lane < HEAD, kn[0], kn[1])
            vnew_ref[0, pl.ds(r0, rc), :] = jnp.where(lane < HEAD, vv[:, :PAIR], vv[:, PAIR:])
        return carry

    lax.fori_loop(0, t // rc, prep, 0)

    tq = ATT_TQ
    lane_q = _iota((tq, PAIR), 1)

    pairs_per_group = N_PAIRS // KV_HEADS

    def attend(i, carry):
        r0 = pl.multiple_of(i * tq, tq)
        for g in range(KV_HEADS):
            tiles = []
            for p in range(g * pairs_per_group, (g + 1) * pairs_per_group):
                qp = q_sc[pl.ds(r0, tq), p * PAIR:(p + 1) * PAIR]
                for a in range(2):
                    tiles.append(jnp.where(lane_q // HEAD == a, qp, jnp.zeros_like(qp)))
            qs = jnp.concatenate(tiles, axis=0)
            kg = k_sc[:, g * PAIR:(g + 1) * PAIR]
            vg = v_sc[:, g * PAIR:(g + 1) * PAIR]
            s = lax.dot_general(qs, kg, (((1,), (1,)), ((), ())), preferred_element_type=F32)
            e = jnp.exp(s - jnp.max(s, axis=-1, keepdims=True))
            l = jnp.sum(e, axis=-1, keepdims=True)
            o = jnp.dot(e.astype(BF16), vg, preferred_element_type=F32) / l
            for j in range(pairs_per_group):
                p = g * pairs_per_group + j
                pair = jnp.where(lane_q < HEAD, o[(2 * j) * tq:(2 * j + 1) * tq], o[(2 * j + 1) * tq:(2 * j + 2) * tq])
                o_ref[0, pl.ds(r0, tq), p * PAIR:(p + 1) * PAIR] = pair.astype(BF16)
        return carry

    lax.fori_loop(0, t // tq, attend, 0)


def _attn(proj, prm, rope_tabs, ctx_kv):
    bsz, t, _ = proj.shape
    rope = rope_tabs is not None
    s_ctx = ctx_kv[0].shape[1] if rope else 0
    s_all = s_ctx + t
    seq = lambda w, j: pl.BlockSpec((1, t, w), lambda b: (b, 0, j), pipeline_mode=pl.Buffered(1))
    in_specs = [seq(INNER, COL_QB), seq(2 * PAIR, COL_KK), seq(2 * PAIR, COL_VV), _full((1, PAIR)), _full((1, PAIR))]
    args = [proj, proj, proj, prm["qgain"], prm["kgain"]]
    if rope:
        in_specs += [_full((t, PAIR)), _full((t, PAIR)),
                     pl.BlockSpec((1, s_ctx, 2 * PAIR), lambda b: (b, 0, 0)),
                     pl.BlockSpec((1, s_ctx, 2 * PAIR), lambda b: (b, 0, 0))]
        args += [rope_tabs[0], rope_tabs[1], ctx_kv[0], ctx_kv[1]]
    out_shape = [jax.ShapeDtypeStruct((bsz, t, INNER), BF16)]
    out_specs = [pl.BlockSpec((1, t, INNER), lambda b: (b, 0, 0))]
    if not rope:
        out_shape += [jax.ShapeDtypeStruct((bsz, t, PAIR), F32)] * 2
        out_specs += [pl.BlockSpec((1, t, PAIR), lambda b: (b, 0, 0))] * 2
    res = pl.pallas_call(
        functools.partial(_attn_body, t=t, s_ctx=s_ctx, rope=rope),
        out_shape=out_shape,
        grid=(bsz,),
        in_specs=in_specs,
        out_specs=out_specs,
        scratch_shapes=[pltpu.VMEM((t, INNER), BF16), pltpu.VMEM((s_all, 2 * PAIR), BF16),
                        pltpu.VMEM((s_all, 2 * PAIR), BF16)],
        compiler_params=_params(),
        name="attn",
    )(*args)
    return res


def _dn_body(*refs, t, has_init, emit_state):
    (q_ref, k_ref, v_ref, gate_ref, ba_ref, baT_ref, cwq_ref, cwk_ref, cwv_ref, cbq_ref, cbk_ref, cbv_ref,
     alog_row_ref, bias_row_ref, alog_col_ref, bias_col_ref, gnorm_ref) = refs[:17]
    pos = 17
    s0_ref = None
    if has_init:
        s0_ref = refs[pos]
        pos += 1
    o_ref = refs[pos]
    pos += 1
    st_ref = None
    if emit_state:
        st_ref = refs[pos]
        pos += 1
    qn_sc, kn_sc, vc_sc, bg_sc, bgT_sc, of_sc, ob_sc, s_sc = refs[pos:]

    rc = ROW_CHUNK
    cl = DN_CHUNK
    nc = t // cl
    nrc = t // rc
    hp = lax.Precision.HIGHEST
    nb = 2 * N_HEADS

    lane_rc = _iota((rc, PAIR), 1)
    row_t = _iota((2 * nb, rc), 0)

    def prep(c, carry):
        r0 = pl.multiple_of(c * rc, rc)
        for src, cw, cb, dst, norm in ((q_ref, cwq_ref, cbq_ref, qn_sc, True), (k_ref, cwk_ref, cbk_ref, kn_sc, True),
                                       (v_ref, cwv_ref, cbv_ref, vc_sc, False)):
            x = _silu(_dwconv(_conv_window(src, r0, c, nrc, rc, t), cw[...], cb[...], rc))
            if norm:
                tiles = []
                for p in range(N_PAIRS):
                    xp = x[:, p * PAIR:(p + 1) * PAIR]
                    tiles.append(xp * lax.rsqrt(_head_sumsq(xp) + EPS))
                x = jnp.concatenate(tiles, axis=1)
                if dst is qn_sc:
                    x = x * (HEAD ** -0.5)
            dst[pl.ds(r0, rc), :] = x
        ba = ba_ref[0, pl.ds(r0, rc), :]
        beta = jax.nn.sigmoid(ba)
        gdec = -jnp.exp(alog_row_ref[...]) * _softplus(ba + bias_row_ref[...])
        bg_sc[pl.ds(r0, rc), :] = jnp.where(lane_rc < nb, beta, gdec)
        baT = baT_ref[0, :, pl.ds(r0, rc)]
        betaT = jax.nn.sigmoid(baT)
        gdecT = -jnp.exp(alog_col_ref[...]) * _softplus(baT + bias_col_ref[...])
        bgT = jnp.where(row_t < nb, betaT, gdecT)
        for k in range(rc // cl):
            bgT_sc[c * (rc // cl) + k] = bgT[:, k * cl:(k + 1) * cl]
        return carry

    lax.fori_loop(0, nrc, prep, 0)

    rr = _iota((PAIR, PAIR), 0)
    cc = _iota((PAIR, PAIR), 1)
    same = (rr // HEAD) == (cc // HEAD)
    ii = rr % HEAD
    jj = cc % HEAD
    lane_c = _iota((cl, PAIR), 1)
    lo = _lo_tri(cl)
    up = _up_tri(cl)
    eye = (rr == cc).astype(F32)
    blk_base = same & (ii // DN_BASE == jj // DN_BASE)
    off_masks = []
    b = DN_BASE
    while b < cl:
        off_masks.append(same & (ii // (2 * b) == jj // (2 * b)) & (ii // b != jj // b))
        b *= 2

    for d in range(2):
        for p in range(N_PAIRS):
            if has_init:
                b0 = s0_ref[0, d, (2 * p) * HEAD:(2 * p + 1) * HEAD, :]
                b1 = s0_ref[0, d, (2 * p + 1) * HEAD:(2 * p + 2) * HEAD, :]
                z64 = jnp.zeros_like(b0)
                s_sc[d, p] = jnp.concatenate([jnp.concatenate([b0, z64], axis=1),
                                              jnp.concatenate([z64, b1], axis=1)], axis=0)
            else:
                s_sc[d, p] = jnp.zeros((PAIR, PAIR), F32)

    def stack2(x):
        return jnp.concatenate([jnp.where(lane_c < HEAD, x, 0.0), jnp.where(lane_c >= HEAD, x, 0.0)], axis=0)

    def col2(m, c0):
        return jnp.concatenate([m[:, c0:c0 + 1], m[:, c0 + 1:c0 + 2]], axis=0)

    systems = [(d, p) for d in range(2) for p in range(N_PAIRS)]
    offdiag = (ii != jj).astype(F32)

    def chunk(k, carry):
        dirs = []
        for d in range(2):
            c = k if d == 0 else nc - 1 - k
            r0 = pl.multiple_of(c * cl, cl)
            bg = bg_sc[pl.ds(r0, cl), :]
            bgT = bgT_sc[c]
            tri = lo if d == 0 else up
            triT = up if d == 0 else lo
            gc = jnp.dot(tri, bg, precision=hp, preferred_element_type=F32)
            gcT = jnp.dot(bgT, triT, precision=hp, preferred_element_type=F32)
            dirs.append((r0, bg, gc, gcT))
        st = []
        for d, p in systems:
            r0, bg, gc, gcT = dirs[d]
            last = cl - 1 if d == 0 else 0
            incl = (jj <= ii) if d == 0 else (jj >= ii)
            h0i = d * N_HEADS + 2 * p
            cbeta = col2(bg, h0i)
            cgc = col2(gc, nb + h0i)
            rgc = jnp.concatenate([gcT[nb + h0i:nb + h0i + 1, :], gcT[nb + h0i + 1:nb + h0i + 2, :]], axis=1)
            glast = jnp.where(rr[:, 0:1] < HEAD, gc[last:last + 1, nb + h0i:nb + h0i + 1],
                              gc[last:last + 1, nb + h0i + 1:nb + h0i + 2])
            decay = jnp.exp(jnp.where(same & incl, cgc - rgc, -jnp.inf))
            cols = pl.ds(p * PAIR, PAIR)
            ks = stack2(kn_sc[pl.ds(r0, cl), cols])
            qs = stack2(qn_sc[pl.ds(r0, cl), cols])
            vs = stack2(vc_sc[pl.ds(r0, cl), cols])
            egc = jnp.exp(cgc)
            st.append(dict(r0=r0, cols=cols, cbeta=cbeta, cgc=cgc, glast=glast, decay=decay, ks=ks, qs=qs,
                           egc=egc, rhs=jnp.concatenate([vs * cbeta, ks * (cbeta * egc)], axis=1)))
        for e in st:
            ksb = e["ks"].astype(BF16)
            e["gm"] = _dot_nt(ksb, ksb)
            e["am"] = _dot_nt(e["qs"], ksb)
        for e in st:
            e["m"] = e["gm"] * e["cbeta"] * e["decay"] * offdiag
            e["aq"] = e["am"] * e["decay"]
            e["n1"] = jnp.where(blk_base, e["m"], 0.0)
        for e in st:
            e["n2"] = _dot(e["n1"], e["n1"])
        for e in st:
            e["n4"] = _dot(e["n2"], e["n2"])
        for e in st:
            e["pm"] = eye + e["n2"] + e["n4"] + _dot(e["n2"], e["n4"])
        for e in st:
            e["x"] = e["pm"] - _dot(e["n1"], e["pm"])
        for off_mask in off_masks:
            for e in st:
                e["cx"] = _dot(jnp.where(off_mask, e["m"], 0.0), e["x"])
            for e in st:
                e["x"] = e["x"] - _dot(e["x"], e["cx"])
        for e in st:
            e["sol"] = _dot(e["x"], e["rhs"])
        for (d, p), e in zip(systems, st):
            e["s"] = s_sc[d, p]
            e["vnew"] = e["sol"][:, :PAIR] - _dot(e["sol"][:, PAIR:], e["s"])
        for (d, p), e in zip(systems, st):
            ost = _dot(e["qs"] * e["egc"], e["s"]) + _dot(e["aq"], e["vnew"])
            dst = of_sc if d == 0 else ob_sc
            dst[pl.ds(e["r0"], cl), e["cols"]] = ost[:cl] + ost[cl:]
            kd = e["ks"] * jnp.exp(e["glast"] - e["cgc"])
            s_sc[d, p] = e["s"] * jnp.exp(e["glast"]) + _dot_tn(kd, e["vnew"])
        return carry

    lax.fori_loop(0, nc, chunk, 0)

    if emit_state:
        for d in range(2):
            for p in range(N_PAIRS):
                s = s_sc[d, p]
                st_ref[0, d, (2 * p) * HEAD:(2 * p + 1) * HEAD, :] = s[:HEAD, :HEAD]
                st_ref[0, d, (2 * p + 1) * HEAD:(2 * p + 2) * HEAD, :] = s[HEAD:, HEAD:]

    def finish(c, carry):
        r0 = pl.multiple_of(c * rc, rc)
        for p in range(N_PAIRS):
            cols = pl.ds(p * PAIR, PAIR)
            o = of_sc[pl.ds(r0, rc), cols] + ob_sc[pl.ds(r0, rc), cols]
            on = o * lax.rsqrt(_head_sumsq(o) * (1.0 / HEAD) + EPS) * gnorm_ref[...]
            o_ref[0, pl.ds(r0, rc), cols] = (on * _silu(gate_ref[0, pl.ds(r0, rc), cols])).astype(BF16)
        return carry

    lax.fori_loop(0, nrc, finish, 0)


def _dn(proj, projT, prm, s0, *, emit_state):
    bsz, t, _ = proj.shape
    nc = t // DN_CHUNK
    has_init = s0 is not None
    seq = lambda w, j: pl.BlockSpec((1, t, w), lambda b: (b, 0, j), pipeline_mode=pl.Buffered(1))
    in_specs = [seq(INNER, COL_QC), seq(INNER, COL_KC), seq(INNER, COL_VC), seq(INNER, COL_GC), seq(PAIR, COL_BA),
                pl.BlockSpec((1, 4 * N_HEADS, t), lambda b: (b, ROWT_BA, 0), pipeline_mode=pl.Buffered(1))]
    args = [proj] * 5 + [projT]
    for name in ("cwq", "cwk", "cwv", "cbq", "cbk", "cbv", "alog_row", "bias_row", "alog_col", "bias_col", "gnorm"):
        a = prm[name]
        in_specs.append(_full(a.shape))
        args.append(a)
    if has_init:
        in_specs.append(pl.BlockSpec((1, 2, INNER, HEAD), lambda b: (b, 0, 0, 0)))
        args.append(s0)
    out_shape = [jax.ShapeDtypeStruct((bsz, t, INNER), BF16)]
    out_specs = [pl.BlockSpec((1, t, INNER), lambda b: (b, 0, 0))]
    if emit_state:
        out_shape.append(jax.ShapeDtypeStruct((bsz, 2, INNER, HEAD), F32))
        out_specs.append(pl.BlockSpec((1, 2, INNER, HEAD), lambda b: (b, 0, 0, 0)))
    res = pl.pallas_call(
        functools.partial(_dn_body, t=t, has_init=has_init, emit_state=emit_state),
        out_shape=out_shape,
        grid=(bsz,),
        in_specs=in_specs,
        out_specs=out_specs,
        scratch_shapes=[pltpu.VMEM((t, INNER), F32), pltpu.VMEM((t, INNER), F32), pltpu.VMEM((t, INNER), F32),
                        pltpu.VMEM((t, PAIR), F32), pltpu.VMEM((nc, 4 * N_HEADS, DN_CHUNK), F32),
                        pltpu.VMEM((t, INNER), F32), pltpu.VMEM((t, INNER), F32),
                        pltpu.VMEM((2, N_PAIRS, PAIR, PAIR), F32)],
        compiler_params=_params(),
        name="deltanet",
    )(*args)
    return (res[0], res[1]) if emit_state else (res[0], None)


def _lru_body(*refs, t, has_init, emit_state):
    x_ref, y_ref, cw_ref, cb_ref, wbd_ref, bias_ref, lam_ref = refs[:7]
    pos = 7
    h0_ref = None
    if has_init:
        h0_ref = refs[pos]
        pos += 1
    o_ref = refs[pos]
    pos += 1
    st_ref = None
    if emit_state:
        st_ref = refs[pos]
        pos += 1
    af_sc, uf_sc, ab_sc, ub_sc = refs[pos:]

    rc = ROW_CHUNK
    nrc = t // rc
    row8 = _iota((HALO, INNER), 0)
    sp_lam = _softplus(-lam_ref[...])

    def gates(c, carry):
        r0 = pl.multiple_of(c * rc, rc)
        xl_all = _dwconv(_conv_window(x_ref, r0, c, nrc, rc, t), cw_ref[...], cb_ref[...], rc)
        for p in range(N_PAIRS):
            cols = pl.ds(p * PAIR, PAIR)
            xl = xl_all[:, p * PAIR:(p + 1) * PAIR]
            pre = _dot(xl, wbd_ref[p]) + bias_ref[p]
            for d, (a_sc, u_sc) in enumerate(((af_sc, uf_sc), (ab_sc, ub_sc))):
                r = jax.nn.sigmoid(pre[:, d * PAIR:(d + 1) * PAIR])
                ig = jax.nn.sigmoid(pre[:, (2 + d) * PAIR:(3 + d) * PAIR])
                log_a = -LRU_C * r * sp_lam[d:d + 1, p * PAIR:(p + 1) * PAIR]
                a_sc[pl.ds(r0, rc), cols] = jnp.exp(log_a)
                u_sc[pl.ds(r0, rc), cols] = jnp.sqrt(-jnp.tanh(log_a) * (jnp.exp(2.0 * log_a) + 1.0)) * ig * xl
        return carry

    lax.fori_loop(0, nrc, gates, 0)

    if has_init:
        cf0 = h0_ref[0, 0:1, :]
        cb0 = h0_ref[0, 1:2, :]
    else:
        cf0 = jnp.zeros((1, INNER), F32)
        cb0 = cf0

    def scan(k, carry):
        cf, cb = carry
        rf = pl.multiple_of(k * HALO, HALO)
        a8 = af_sc[pl.ds(rf, HALO), :]
        b8 = uf_sc[pl.ds(rf, HALO), :]
        for s in (1, 2, 4):
            ok = row8 >= s
            b8 = jnp.where(ok, a8 * pltpu.roll(b8, s, 0) + b8, b8)
            a8 = jnp.where(ok, a8 * pltpu.roll(a8, s, 0), a8)
        hf = a8 * cf + b8
        uf_sc[pl.ds(rf, HALO), :] = hf
        rb = pl.multiple_of(t - HALO - k * HALO, HALO)
        a8 = ab_sc[pl.ds(rb, HALO), :]
        b8 = ub_sc[pl.ds(rb, HALO), :]
        for s in (1, 2, 4):
            ok = row8 < HALO - s
            b8 = jnp.where(ok, a8 * pltpu.roll(b8, HALO - s, 0) + b8, b8)
            a8 = jnp.where(ok, a8 * pltpu.roll(a8, HALO - s, 0), a8)
        hb = a8 * cb + b8
        ub_sc[pl.ds(rb, HALO), :] = hb
        return hf[HALO - 1:HALO, :], hb[0:1, :]

    cf, cb = lax.fori_loop(0, t // HALO, scan, (cf0, cb0))
    if emit_state:
        st_ref[0, 0:1, :] = cf
        st_ref[0, 1:2, :] = cb

    def finish(c, carry):
        r0 = pl.multiple_of(c * rc, rc)
        y = y_ref[0, pl.ds(r0, rc), :]
        gelu = 0.5 * y * (1.0 + jnp.tanh(math.sqrt(2.0 / math.pi) * (y + 0.044715 * (y * y * y))))
        o_ref[0, pl.ds(r0, rc), :] = ((uf_sc[pl.ds(r0, rc), :] + ub_sc[pl.ds(r0, rc), :]) * gelu).astype(BF16)
        return carry

    lax.fori_loop(0, nrc, finish, 0)


def _lru(proj, prm, h0, *, emit_state):
    bsz, t, _ = proj.shape
    has_init = h0 is not None
    seq = lambda w, j: pl.BlockSpec((1, t, w), lambda b: (b, 0, j), pipeline_mode=pl.Buffered(1))
    in_specs = [seq(INNER, COL_XD), seq(INNER, COL_YD)]
    args = [proj, proj]
    for name in ("cw", "cb", "wbd", "bias", "lam"):
        a = prm[name]
        in_specs.append(_full(a.shape))
        args.append(a)
    if has_init:
        in_specs.append(pl.BlockSpec((1, 2, INNER), lambda b: (b, 0, 0)))
        args.append(h0)
    out_shape = [jax.ShapeDtypeStruct((bsz, t, INNER), BF16)]
    out_specs = [pl.BlockSpec((1, t, INNER), lambda b: (b, 0, 0))]
    if emit_state:
        out_shape.append(jax.ShapeDtypeStruct((bsz, 2, INNER), F32))
        out_specs.append(pl.BlockSpec((1, 2, INNER), lambda b: (b, 0, 0)))
    res = pl.pallas_call(
        functools.partial(_lru_body, t=t, has_init=has_init, emit_state=emit_state),
        out_shape=out_shape,
        grid=(bsz,),
        in_specs=in_specs,
        out_specs=out_specs,
        scratch_shapes=[pltpu.VMEM((t, INNER), F32)] * 4,
        compiler_params=_params(),
        name="rglru",
    )(*args)
    return (res[0], res[1]) if emit_state else (res[0], None)


def _pad_cols(x, n):
    return jnp.pad(x, ((0, 0), (0, n - x.shape[1])))


def _layer_params(l, w_in, ssm_conv_w, ssm_conv_b, ssm_a_log, ssm_dt_bias, ssm_d, ssm_norm, attn_q_norm, attn_k_norm,
                  dn_conv_w, dn_conv_b, dn_a_log, dn_dt_bias, dn_norm, lru_conv_w, lru_conv_b, lru_w_a, lru_b_a,
                  lru_w_i, lru_b_i, lru_lambda):
    w = w_in[l]
    o = 0

    def take(n):
        nonlocal o
        part = w[:, o:o + n]
        o += n
        return part

    z_a, xs_a, bc_a, dt_a = take(INNER), take(INNER), take(2 * PAIR), take(2 * N_HEADS)
    q_b, k_b, v_b = take(INNER), take(PAIR), take(PAIR)
    q_c, k_c, v_c = take(INNER), take(INNER), take(INNER)
    beta_c, a_c, gate_c = take(2 * N_HEADS), take(2 * N_HEADS), take(INNER)
    x_d, y_d = take(INNER), take(INNER)
    gate_raw = take(N_BRANCH * D_MODEL)

    def dup(x):
        return jnp.concatenate([x[:, :HEAD], x[:, :HEAD], x[:, HEAD:], x[:, HEAD:]], axis=1)

    nh2 = 2 * N_HEADS
    ba_c = jnp.concatenate([beta_c, a_c], axis=1)
    w_proj = jnp.concatenate([z_a, xs_a, q_b, q_c, k_c, v_c, gate_c, x_d, y_d, bc_a, dup(k_b), dup(v_b),
                              _pad_cols(dt_a, PAIR), _pad_cols(ba_c, PAIR)], axis=1)
    assert w_proj.shape[1] == PROJ_N
    prm = {
        "w_proj": w_proj.astype(BF16),
        "w_projT": jnp.concatenate([ba_c, dt_a], axis=1).T.astype(BF16),
        "w_gate": gate_raw.astype(BF16),
    }
    cw, cb = ssm_conv_w[l], ssm_conv_b[l][None, :]
    dtb = ssm_dt_bias[l].reshape(1, nh2)
    alog = ssm_a_log[l].reshape(1, nh2)
    prm["ssd"] = {
        "cwx": cw[:, :INNER], "cbx": cb[:, :INNER], "cwbc": cw[:, INNER:], "cbbc": cb[:, INNER:],
        "dtb_row": _pad_cols(dtb, PAIR), "dtb_col": jnp.broadcast_to(dtb.T, (nh2, SSD_CHUNK)),
        "alog_row": _pad_cols(alog, PAIR), "alog_col": jnp.broadcast_to(alog.T, (nh2, SSD_CHUNK)),
        "dskip": jnp.repeat(ssm_d[l], HEAD)[None, :], "gnorm": ssm_norm[l][None, :],
    }
    prm["att"] = {"qgain": jnp.tile(attn_q_norm[l], 2)[None, :], "kgain": jnp.tile(attn_k_norm[l], 2)[None, :]}
    dcw, dcb = dn_conv_w[l], dn_conv_b[l][None, :]
    zeros16 = jnp.zeros((1, nh2), F32)
    d_alog = jnp.concatenate([zeros16, dn_a_log[l].reshape(1, nh2)], axis=1)
    d_bias = jnp.concatenate([zeros16, dn_dt_bias[l].reshape(1, nh2)], axis=1)
    prm["dn"] = {
        "cwq": dcw[:, :INNER], "cwk": dcw[:, INNER:2 * INNER], "cwv": dcw[:, 2 * INNER:],
        "cbq": dcb[:, :INNER], "cbk": dcb[:, INNER:2 * INNER], "cbv": dcb[:, 2 * INNER:],
        "alog_row": _pad_cols(d_alog, PAIR), "bias_row": _pad_cols(d_bias, PAIR),
        "alog_col": jnp.broadcast_to(d_alog.T, (2 * nh2, ROW_CHUNK)),
        "bias_col": jnp.broadcast_to(d_bias.T, (2 * nh2, ROW_CHUNK)),
        "gnorm": jnp.tile(dn_norm[l], 2)[None, :],
    }
    wa, wi = lru_w_a[l], lru_w_i[l]
    z64 = jnp.zeros((HEAD, HEAD), F32)
    wbd, bias = [], []
    for p in range(N_PAIRS):
        blocks = []
        for wsrc in (wa, wi):
            for d in range(2):
                top = jnp.concatenate([wsrc[d, 2 * p], z64], axis=1)
                bot = jnp.concatenate([z64, wsrc[d, 2 * p + 1]], axis=1)
                blocks.append(jnp.concatenate([top, bot], axis=0))
        wbd.append(jnp.concatenate(blocks, axis=1))
        bias.append(jnp.concatenate([lru_b_a[l][0, p * PAIR:(p + 1) * PAIR], lru_b_a[l][1, p * PAIR:(p + 1) * PAIR],
                                     lru_b_i[l][0, p * PAIR:(p + 1) * PAIR], lru_b_i[l][1, p * PAIR:(p + 1) * PAIR]])[None, :])
    prm["lru"] = {"cw": lru_conv_w[l], "cb": lru_conv_b[l][None, :], "wbd": jnp.stack(wbd).astype(BF16),
                  "bias": jnp.stack(bias), "lam": lru_lambda[l]}
    return prm


def _rope_tables(t):
    n_freq = HEAD // 4
    inv = ROPE_THETA ** (-jnp.arange(n_freq, dtype=F32) / n_freq)
    rows = t // GRID_W
    row = jnp.repeat(jnp.arange(rows, dtype=F32), GRID_W)
    col = jnp.tile(jnp.arange(GRID_W, dtype=F32), rows)
    ang_r = row[:, None] * inv
    ang_c = col[:, None] * inv
    ang = jnp.concatenate([ang_r, ang_r, ang_c, ang_c], axis=1)
    sign = jnp.tile(jnp.concatenate([-jnp.ones((n_freq,), F32), jnp.ones((n_freq,), F32)]), 2)
    cos = jnp.cos(ang)
    sin = jnp.sin(ang) * sign
    return jnp.tile(cos, (1, 2)), jnp.tile(sin, (1, 2))


def _dup_kv(x):
    return jnp.concatenate([x[:, :, 0], x[:, :, 0], x[:, :, 1], x[:, :, 1]], axis=-1)


def _trunk_layer(h, mod, lw, prm, rope_tabs, ctx, fin_gain, *, final):
    h = _ffn(h, mod, lw["norm_ffn1"], lw["ffn1_w13"], lw["ffn1_w2"], fin_gain, rows=(0, 1, 2), final=False)
    gain = lw["norm_mix"]
    emit = ctx is None
    proj, projT = _proj(h, mod, gain, prm["w_proj"], prm["w_projT"])
    if emit:
        o_a, st_ssm = _ssd(proj, projT, prm["ssd"], None, emit_state=True)
        o_b, k_new, v_new = _attn(proj, prm["att"], None, None)
        o_c, st_dn = _dn(proj, projT, prm["dn"], None, emit_state=True)
        o_d, st_lru = _lru(proj, prm["lru"], None, emit_state=True)
        new_ctx = (k_new, v_new, st_ssm, st_dn, st_lru)
    else:
        ck, cv, ssm0, dn0, lru0 = ctx
        o_a, _ = _ssd(proj, projT, prm["ssd"], ssm0, emit_state=False)
        (o_b,) = _attn(proj, prm["att"], rope_tabs, (ck, cv))
        o_c, _ = _dn(proj, projT, prm["dn"], dn0, emit_state=False)
        o_d, _ = _lru(proj, prm["lru"], lru0, emit_state=False)
        new_ctx = None
    h = _merge(h, mod, gain, (o_a, o_b, o_c, o_d), prm["w_gate"], lw["w_branch"], lw["w_out"])
    h = _ffn(h, mod, lw["norm_ffn2"], lw["ffn2_w13"], lw["ffn2_w2"], fin_gain, rows=(6, 7, 8), final=final)
    return h, new_ctx


def kernel(x_prompt, x_sample, cache_k, cache_v, state_ssm, state_delta, state_lru, c, c_ctx,
           w_ada, b_ada, norm_ffn1, ffn1_w13, ffn1_w2, norm_mix, w_in,
           ssm_conv_w, ssm_conv_b, ssm_a_log, ssm_dt_bias, ssm_d, ssm_norm,
           attn_q_norm, attn_k_norm,
           dn_conv_w, dn_conv_b, dn_a_log, dn_dt_bias, dn_norm,
           lru_conv_w, lru_conv_b, lru_w_a, lru_b_a, lru_w_i, lru_b_i, lru_lambda,
           w_branch, w_out, norm_ffn2, ffn2_w13, ffn2_w2, final_norm):
    depth = w_in.shape[0]
    bsz_p, t_p, _ = x_prompt.shape
    bsz_s, t_s, _ = x_sample.shape
    assert bsz_s + 1 <= MOD_ROWS
    cvec = jnp.concatenate([c, c_ctx[None], jnp.zeros((MOD_ROWS - bsz_s - 1, D_MODEL), F32)], axis=0)
    mod_all = _adaln(cvec, w_ada, b_ada)
    mod_all = mod_all.reshape(depth, MOD_ROWS, N_MOD, D_MODEL)
    mod_all = jnp.pad(mod_all, ((0, 0), (0, 0), (0, MOD_ROWS - N_MOD), (0, 0)))
    rope_tabs = _rope_tables(t_s)
    fin_gain = final_norm[None, :]

    hp, hs = x_prompt, x_sample
    ks, vs, ssm_s, dn_s, lru_s = [], [], [], [], []
    for l in range(depth):
        prm = _layer_params(l, w_in, ssm_conv_w, ssm_conv_b, ssm_a_log, ssm_dt_bias, ssm_d, ssm_norm,
                            attn_q_norm, attn_k_norm, dn_conv_w, dn_conv_b, dn_a_log, dn_dt_bias, dn_norm,
                            lru_conv_w, lru_conv_b, lru_w_a, lru_b_a, lru_w_i, lru_b_i, lru_lambda)
        lw = {
            "norm_ffn1": norm_ffn1[l][None, :], "ffn1_w13": ffn1_w13[l].astype(BF16), "ffn1_w2": ffn1_w2[l].astype(BF16),
            "norm_mix": norm_mix[l][None, :], "w_branch": w_branch[l].astype(BF16), "w_out": w_out[l].astype(BF16),
            "norm_ffn2": norm_ffn2[l][None, :], "ffn2_w13": ffn2_w13[l].astype(BF16), "ffn2_w2": ffn2_w2[l].astype(BF16),
        }
        final = l == depth - 1
        mod_lat = mod_all[l, :bsz_s]
        mod_ctx = mod_all[l, bsz_s:bsz_s + 1]
        hp, st = _trunk_layer(hp, mod_ctx, lw, prm, None, None, fin_gain, final=final)
        k_c, v_c, st_ssm, st_dn, st_lru = st
        ks.append(k_c.reshape(bsz_p, t_p, KV_HEADS, HEAD))
        vs.append(v_c.reshape(bsz_p, t_p, KV_HEADS, HEAD))
        ssm_s.append(st_ssm.reshape(bsz_p, 2, N_HEADS, HEAD, HEAD))
        dn_s.append(st_dn.reshape(bsz_p, 2, N_HEADS, HEAD, HEAD))
        lru_s.append(st_lru)
        ctx_l = (_dup_kv(cache_k[:, l]), _dup_kv(cache_v[:, l]),
                 state_ssm[:, l].reshape(bsz_s, 2, INNER, HEAD), state_delta[:, l].reshape(bsz_s, 2, INNER, HEAD),
                 state_lru[:, l])
        hs, _ = _trunk_layer(hs, mod_lat, lw, prm, rope_tabs, ctx_l, fin_gain, final=final)
    return (hp, hs, jnp.stack(ks, axis=1), jnp.stack(vs, axis=1), jnp.stack(ssm_s, axis=1),
            jnp.stack(dn_s, axis=1), jnp.stack(lru_s, axis=1))
```

```python
import functools
import math

import jax
import jax.numpy as jnp
from jax import lax
from jax.experimental import pallas as pl
from jax.experimental.pallas import tpu as pltpu

F32 = jnp.float32
BF16 = jnp.bfloat16

D_MODEL = 1024
D_FF = 2816
N_MOD = 9
MOD_ROWS = 16
EPS = 1e-6
GRID_W = 64
CONV_W = 4
CONV_LP = CONV_W // 2
HALO = 8
HEAD = 64
PAIR = 2 * HEAD
N_HEADS = 8
N_PAIRS = N_HEADS // 2
INNER = N_HEADS * HEAD
KV_HEADS = 2
SSM_GROUPS = 2
ROPE_THETA = 10000.0
LRU_C = 8.0
N_BRANCH = 4
SSD_CHUNK = 256
DN_CHUNK = 64
DN_BASE = 8
DN_GROUP = 4
ROW_CHUNK = 256
ATT_TQ = 64
ATT_SM_ROWS = 64
FF_CHUNK = 256
VMEM_LIMIT = 56 * 1024 * 1024

COL_Z, COL_XS, COL_QB, COL_QC, COL_KC, COL_VC, COL_GC, COL_XD, COL_YD = range(9)
COL_BC, COL_KK, COL_VV = 18, 19, 20
COL_DT, COL_BA = 42, 43
PROJ_N = 44 * PAIR
ROWT_BA, ROWT_DT = 0, 2
PROJ_T = 48


def _dot(a, b):
    return jnp.dot(a.astype(BF16), b.astype(BF16), preferred_element_type=F32)


def _dot_nt(a, b):
    return lax.dot_general(a.astype(BF16), b.astype(BF16), (((1,), (1,)), ((), ())),
                           preferred_element_type=F32)


def _dot_tn(a, b):
    return lax.dot_general(a.astype(BF16), b.astype(BF16), (((0,), (0,)), ((), ())),
                           preferred_element_type=F32)


def _split3(x):
    hi = x.astype(BF16)
    r1 = x - hi.astype(F32)
    mid = r1.astype(BF16)
    lo = (r1 - mid.astype(F32)).astype(BF16)
    return hi, mid, lo


def _dot_sel(x, sel):
    hi, mid, lo = _split3(x)
    s = sel.astype(BF16)
    return (jnp.dot(hi, s, preferred_element_type=F32) + jnp.dot(mid, s, preferred_element_type=F32)
            + jnp.dot(lo, s, preferred_element_type=F32))


def _sel_dot(sel, x):
    hi, mid, lo = _split3(x)
    s = sel.astype(BF16)
    return (jnp.dot(s, hi, preferred_element_type=F32) + jnp.dot(s, mid, preferred_element_type=F32)
            + jnp.dot(s, lo, preferred_element_type=F32))


def _iota(shape, axis):
    return lax.broadcasted_iota(jnp.int32, shape, axis)


def _silu(x):
    return x * jax.nn.sigmoid(x)


def _softplus(x):
    return jnp.maximum(x, 0.0) + jnp.log1p(jnp.exp(-jnp.abs(x)))


def _rms(x, g):
    return x * lax.rsqrt(jnp.mean(x * x, axis=-1, keepdims=True) + EPS) * g


def _head_ones():
    return (_iota((PAIR, PAIR), 0) // HEAD == _iota((PAIR, PAIR), 1) // HEAD).astype(F32)


def _head_sumsq(x):
    return _dot_sel(x * x, _head_ones())


def _lo_tri(n):
    return (_iota((n, n), 1) <= _iota((n, n), 0)).astype(F32)


def _up_tri(n):
    return (_iota((n, n), 1) >= _iota((n, n), 0)).astype(F32)


def _conv_window(ref, r0, c, nchunks, rows, t_total):
    cur = ref[0, pl.ds(r0, rows), :]
    prev_start = pl.multiple_of(jnp.maximum(r0 - HALO, 0), HALO)
    next_start = pl.multiple_of(jnp.minimum(r0 + rows, t_total - HALO), HALO)
    prev = jnp.where(c > 0, ref[0, pl.ds(prev_start, HALO), :], 0.0)
    nxt = jnp.where(c < nchunks - 1, ref[0, pl.ds(next_start, HALO), :], 0.0)
    return jnp.concatenate([prev, cur, nxt], axis=0)


def _dwconv(ext, w, b, rows):
    out = b
    for j in range(CONV_W):
        off = HALO - CONV_LP + j
        out = out + w[j:j + 1, :] * ext[off:off + rows, :]
    return out


def _expand_heads(n_src_rows, first, width):
    r = _iota((n_src_rows, width), 0)
    l = _iota((n_src_rows, width), 1)
    return (r - first == l // HEAD).astype(F32)


def _full(shape):
    zeros = (0,) * len(shape)
    return pl.BlockSpec(shape, lambda *_: zeros, pipeline_mode=pl.Buffered(1))


def _params():
    return pltpu.CompilerParams(vmem_limit_bytes=VMEM_LIMIT)


def _adaln_body(c_ref, w_ref, b_ref, o_ref):
    o_ref[0] = _dot(_silu(c_ref[...]), w_ref[0]) + b_ref[0]


def _adaln(cvec, w_ada, b_ada):
    depth = w_ada.shape[0]
    n = w_ada.shape[2]
    tn = n // N_MOD
    return pl.pallas_call(
        _adaln_body,
        out_shape=jax.ShapeDtypeStruct((depth, MOD_ROWS, n), F32),
        grid=(depth, N_MOD),
        in_specs=[pl.BlockSpec((MOD_ROWS, D_MODEL), lambda l, j: (0, 0)),
                  pl.BlockSpec((1, D_MODEL, tn), lambda l, j: (l, 0, j)),
                  pl.BlockSpec((1, 1, tn), lambda l, j: (l, 0, j))],
        out_specs=pl.BlockSpec((1, MOD_ROWS, tn), lambda l, j: (l, 0, j)),
        compiler_params=_params(),
        name="adaln",
    )(cvec, w_ada, b_ada.reshape(depth, 1, n))


def _ffn_body(h_ref, mod_ref, gain_ref, w13_ref, w2_ref, fin_ref, o_ref, acc_sc, *, rows, final):
    shift, scale, gate = rows
    ms = mod_ref[0]
    h = h_ref[0]
    xn = (_rms(h, gain_ref[...]) * (1.0 + ms[scale:scale + 1]) + ms[shift:shift + 1]).astype(BF16)
    for j in range(D_FF // FF_CHUNK):
        lo, hi = j * FF_CHUNK, (j + 1) * FF_CHUNK
        g = jnp.dot(xn, w13_ref[:, lo:hi], preferred_element_type=F32)
        u = jnp.dot(xn, w13_ref[:, D_FF + lo:D_FF + hi], preferred_element_type=F32)
        part = _dot(_silu(g) * u, w2_ref[lo:hi, :])
        if j == 0:
            acc_sc[...] = part
        else:
            acc_sc[...] += part
    hn = h + 0.5 * ms[gate:gate + 1] * acc_sc[...]
    o_ref[0] = _rms(hn, fin_ref[...]) if final else hn


def _ffn(h, mod, gain, w13, w2, fin_gain, *, rows, final):
    bsz, t, d = h.shape
    tm = min(512, t)
    mb = mod.shape[0]
    mod_ix = (lambda b, i: (b, 0, 0)) if mb > 1 else (lambda b, i: (0, 0, 0))
    return pl.pallas_call(
        functools.partial(_ffn_body, rows=rows, final=final),
        out_shape=jax.ShapeDtypeStruct((bsz, t, d), F32),
        grid=(bsz, t // tm),
        in_specs=[pl.BlockSpec((1, tm, d), lambda b, i: (b, i, 0)),
                  pl.BlockSpec((1, MOD_ROWS, d), mod_ix),
                  pl.BlockSpec((1, d), lambda b, i: (0, 0)),
                  _full((d, 2 * D_FF)), _full((D_FF, d)),
                  pl.BlockSpec((1, d), lambda b, i: (0, 0))],
        out_specs=pl.BlockSpec((1, tm, d), lambda b, i: (b, i, 0)),
        scratch_shapes=[pltpu.VMEM((tm, d), F32)],
        compiler_params=_params(),
        name="ffn",
    )(h, mod, gain, w13, w2, fin_gain)


def _proj_body(h_ref, mod_ref, gain_ref, w_ref, wt_ref, o_ref, ot_ref):
    ms = mod_ref[0]
    xn = (_rms(h_ref[0], gain_ref[...]) * (1.0 + ms[4:5]) + ms[3:4]).astype(BF16)
    ot_ref[0] = lax.dot_general(wt_ref[...], xn, (((1,), (1,)), ((), ())), preferred_element_type=F32)
    o_ref[0] = jnp.dot(xn, w_ref[...], preferred_element_type=F32)


def _proj(h, mod, gain, w, wt):
    bsz, t, d = h.shape
    n = w.shape[1]
    r = wt.shape[0]
    tm = min(256, t)
    mb = mod.shape[0]
    mod_ix = (lambda b, i: (b, 0, 0)) if mb > 1 else (lambda b, i: (0, 0, 0))
    return pl.pallas_call(
        _proj_body,
        out_shape=[jax.ShapeDtypeStruct((bsz, t, n), F32), jax.ShapeDtypeStruct((bsz, r, t), F32)],
        grid=(bsz, t // tm),
        in_specs=[pl.BlockSpec((1, tm, d), lambda b, i: (b, i, 0)),
                  pl.BlockSpec((1, MOD_ROWS, d), mod_ix),
                  pl.BlockSpec((1, d), lambda b, i: (0, 0)),
                  _full((d, n)), _full((r, d))],
        out_specs=[pl.BlockSpec((1, tm, n), lambda b, i: (b, i, 0)),
                   pl.BlockSpec((1, r, tm), lambda b, i: (b, 0, i))],
        compiler_params=_params(),
        name="proj",
    )(h, mod, gain, w, wt)


def _merge_body(h_ref, mod_ref, gain_ref, oa_ref, ob_ref, oc_ref, od_ref, wg_ref, wb_ref, wo_ref, o_ref):
    ms = mod_ref[0]
    h = h_ref[0]
    xn = (_rms(h, gain_ref[...]) * (1.0 + ms[4:5]) + ms[3:4]).astype(BF16)
    merged = None
    for n, br_ref in enumerate((oa_ref, ob_ref, oc_ref, od_ref)):
        gate = jax.nn.sigmoid(jnp.dot(xn, wg_ref[:, n * D_MODEL:(n + 1) * D_MODEL], preferred_element_type=F32))
        term = gate * jnp.dot(br_ref[0], wb_ref[n], preferred_element_type=F32)
        merged = term if merged is None else merged + term
    o_ref[0] = h + ms[5:6] * _dot(merged, wo_ref[...])


def _merge(h, mod, gain, branches, w_gate, w_branch, w_out):
    bsz, t, d = h.shape
    tm = min(512, t)
    mb = mod.shape[0]
    mod_ix = (lambda b, i: (b, 0, 0)) if mb > 1 else (lambda b, i: (0, 0, 0))
    br_spec = pl.BlockSpec((1, tm, INNER), lambda b, i: (b, i, 0))
    return pl.pallas_call(
        _merge_body,
        out_shape=jax.ShapeDtypeStruct((bsz, t, d), F32),
        grid=(bsz, t // tm),
        in_specs=[pl.BlockSpec((1, tm, d), lambda b, i: (b, i, 0)),
                  pl.BlockSpec((1, MOD_ROWS, d), mod_ix),
                  pl.BlockSpec((1, d), lambda b, i: (0, 0)),
                  br_spec, br_spec, br_spec, br_spec,
                  _full((d, N_BRANCH * d)), _full((N_BRANCH, INNER, d)), _full((d, d))],
        out_specs=pl.BlockSpec((1, tm, d), lambda b, i: (b, i, 0)),
        compiler_params=_params(),
        name="merge",
    )(h, mod, gain, *branches, w_gate, w_branch, w_out)


def _ssd_body(*refs, t, has_init, emit_state):
    (z_ref, xs_ref, bc_ref, dt_ref, dtT_ref, cwx_ref, cbx_ref, cwbc_ref, cbbc_ref,
     dtb_row_ref, dtb_col_ref, alog_row_ref, alog_col_ref, dskip_ref, gnorm_ref) = refs[:15]
    pos = 15
    h0_ref = None
    if has_init:
        h0_ref = refs[pos]
        pos += 1
    o_ref = refs[pos]
    pos += 1
    st_ref = None
    if emit_state:
        st_ref = refs[pos]
        pos += 1
    y_sc, cum_sc, cm_sc, dh_sc, tot_sc, hst_sc = refs[pos:]

    cl = SSD_CHUNK
    nc = t // cl
    lane = _iota((cl, PAIR), 1)
    ii = _iota((cl, cl), 0)
    jj = _iota((cl, cl), 1)
    lo = _lo_tri(cl)
    up = _up_tri(cl)
    hp = lax.Precision.HIGHEST
    a_row = -jnp.exp(alog_row_ref[...])
    a_col = -jnp.exp(alog_col_ref[...])
    sel_f = _expand_heads(PAIR, 0, INNER)
    sel_b = _expand_heads(PAIR, N_HEADS, INNER)

    def intra(c, carry):
        r0 = pl.multiple_of(c * cl, cl)
        xs = _silu(_dwconv(_conv_window(xs_ref, r0, c, nc, cl, t), cwx_ref[...], cbx_ref[...], cl))
        bc = _silu(_dwconv(_conv_window(bc_ref, r0, c, nc, cl, t), cwbc_ref[...], cbbc_ref[...], cl))
        bm = bc[:, :PAIR]
        cm = bc[:, PAIR:]
        dt = _softplus(dt_ref[0, pl.ds(r0, cl), :] + dtb_row_ref[...])
        dtT = _softplus(dtT_ref[0, :, pl.ds(r0, cl)] + dtb_col_ref[...])
        da = dt * a_row
        daT = dtT * a_col
        cum = jnp.where(lane < N_HEADS, jnp.dot(lo, da, precision=hp, preferred_element_type=F32),
                        jnp.dot(up, da, precision=hp, preferred_element_type=F32))
        rowsel = _iota((2 * N_HEADS, cl), 0) < N_HEADS
        cumT = jnp.where(rowsel, jnp.dot(daT, up, precision=hp, preferred_element_type=F32),
                         jnp.dot(daT, lo, precision=hp, preferred_element_type=F32))
        cum_sc[pl.ds(r0, cl), :] = cum
        cm_sc[pl.ds(r0, cl), :] = cm
        cb = []
        for g in range(SSM_GROUPS):
            cg = jnp.where(lane // HEAD == g, cm, 0.0)
            cb.append(_dot_nt(cg, bm))
        ypairs = []
        for p in range(N_PAIRS):
            xp = xs[:, p * PAIR:(p + 1) * PAIR].astype(BF16)
            halves = []
            for a in range(2):
                h = 2 * p + a
                g = h // (N_HEADS // SSM_GROUPS)
                hb = N_HEADS + h
                sf = jnp.exp(jnp.where(jj <= ii, cum[:, h:h + 1] - cumT[h:h + 1, :], -jnp.inf)) * dtT[h:h + 1, :]
                sb = jnp.exp(jnp.where(jj >= ii, cum[:, hb:hb + 1] - cumT[hb:hb + 1, :], -jnp.inf)) * dtT[hb:hb + 1, :]
                s = (cb[g] * (sf + sb)).astype(BF16)
                halves.append(jnp.dot(s, xp, preferred_element_type=F32))
            ypairs.append(jnp.where(lane < HEAD, halves[0], halves[1]))
        y = jnp.concatenate(ypairs, axis=1) + dskip_ref[...] * xs
        y_sc[pl.ds(r0, cl), :] = y
        tot = jnp.where(lane[0:1] < N_HEADS, cum[cl - 1:cl, :], cum[0:1, :])
        tot_sc[c] = jnp.broadcast_to(tot, (HALO, PAIR))
        wexp = jnp.exp(tot - cum) * dt
        for d, sel in enumerate((sel_f, sel_b)):
            xw = xs * _dot_sel(wexp, sel)
            for p in range(N_PAIRS):
                g = p // (N_PAIRS // SSM_GROUPS)
                bg = jnp.where(lane // HEAD == g, bm, 0.0)
                dh_sc[c, d, p] = _dot_tn(xw[:, p * PAIR:(p + 1) * PAIR], bg)
        return carry

    lax.fori_loop(0, nc, intra, 0)

    rr = _iota((PAIR, PAIR), 0)
    for d in range(2):
        sel = sel_f if d == 0 else sel_b
        for p in range(N_PAIRS):
            g = p // (N_PAIRS // SSM_GROUPS)
            if has_init:
                blk = h0_ref[0, d, p * PAIR:(p + 1) * PAIR, :]
                z64 = jnp.zeros_like(blk)
                hst_sc[p] = jnp.concatenate([blk, z64] if g == 0 else [z64, blk], axis=1)
            else:
                hst_sc[p] = jnp.zeros((PAIR, PAIR), F32)

        def inter(k, carry, d=d, sel=sel):
            c = k if d == 0 else nc - 1 - k
            r0 = pl.multiple_of(c * cl, cl)
            ecum = _dot_sel(jnp.exp(cum_sc[pl.ds(r0, cl), :]), sel)
            cm = cm_sc[pl.ds(r0, cl), :]
            tot = tot_sc[c]
            dec = jnp.exp(tot[0:1, :])
            for p in range(N_PAIRS):
                g = p // (N_PAIRS // SSM_GROUPS)
                cg = jnp.where(lane // HEAD == g, cm, 0.0)
                hs = hst_sc[p]
                yi = _dot_nt(cg, hs)
                cols = pl.ds(p * PAIR, PAIR)
                y_sc[pl.ds(r0, cl), cols] = y_sc[pl.ds(r0, cl), cols] + yi * ecum[:, p * PAIR:(p + 1) * PAIR]
                h0i = d * N_HEADS + 2 * p
                dcol = jnp.where(rr < HEAD, dec[:, h0i:h0i + 1], dec[:, h0i + 1:h0i + 2])
                hst_sc[p] = hs * dcol + dh_sc[c, d, p]
            return carry

        lax.fori_loop(0, nc, inter, 0)
        if emit_state:
            for p in range(N_PAIRS):
                g = p // (N_PAIRS // SSM_GROUPS)
                st_ref[0, d, p * PAIR:(p + 1) * PAIR, :] = hst_sc[p][:, g * HEAD:(g + 1) * HEAD]

    def finish(c, carry):
        r0 = pl.multiple_of(c * cl, cl)
        v = y_sc[pl.ds(r0, cl), :] * _silu(z_ref[0, pl.ds(r0, cl), :])
        o_ref[0, pl.ds(r0, cl), :] = _rms(v, gnorm_ref[...]).astype(BF16)
        return carry

    lax.fori_loop(0, nc, finish, 0)


def _ssd(proj, projT, prm, h0, *, emit_state):
    bsz, t, _ = proj.shape
    nc = t // SSD_CHUNK
    has_init = h0 is not None
    seq = lambda w, j: pl.BlockSpec((1, t, w), lambda b: (b, 0, j), pipeline_mode=pl.Buffered(1))
    in_specs = [seq(INNER, COL_Z), seq(INNER, COL_XS), seq(2 * PAIR, COL_BC), seq(PAIR, COL_DT),
                pl.BlockSpec((1, 2 * N_HEADS, t), lambda b: (b, ROWT_DT, 0), pipeline_mode=pl.Buffered(1))]
    args = [proj, proj, proj, proj, projT]
    for name in ("cwx", "cbx", "cwbc", "cbbc", "dtb_row", "dtb_col", "alog_row", "alog_col", "dskip", "gnorm"):
        a = prm[name]
        in_specs.append(_full(a.shape))
        args.append(a)
    if has_init:
        in_specs.append(pl.BlockSpec((1, 2, INNER, HEAD), lambda b: (b, 0, 0, 0)))
        args.append(h0)
    out_shape = [jax.ShapeDtypeStruct((bsz, t, INNER), BF16)]
    out_specs = [pl.BlockSpec((1, t, INNER), lambda b: (b, 0, 0))]
    if emit_state:
        out_shape.append(jax.ShapeDtypeStruct((bsz, 2, INNER, HEAD), F32))
        out_specs.append(pl.BlockSpec((1, 2, INNER, HEAD), lambda b: (b, 0, 0, 0)))
    res = pl.pallas_call(
        functools.partial(_ssd_body, t=t, has_init=has_init, emit_state=emit_state),
        out_shape=out_shape,
        grid=(bsz,),
        in_specs=in_specs,
        out_specs=out_specs,
        scratch_shapes=[pltpu.VMEM((t, INNER), F32), pltpu.VMEM((t, PAIR), F32), pltpu.VMEM((t, PAIR), F32),
                        pltpu.VMEM((nc, 2, N_PAIRS, PAIR, PAIR), F32), pltpu.VMEM((nc, HALO, PAIR), F32),
                        pltpu.VMEM((N_PAIRS, PAIR, PAIR), F32)],
        compiler_params=_params(),
        name="ssd",
    )(*args)
    return (res[0], res[1]) if emit_state else (res[0], None)


def _attn_body(*refs, t, s_ctx, rope):
    q_ref, kk_ref, vv_ref, qg_ref, kg_ref = refs[:5]
    pos = 5
    if rope:
        cos_ref, sin_ref, ck_ref, cv_ref = refs[pos:pos + 4]
        pos += 4
    o_ref = refs[pos]
    pos += 1
    if not rope:
        knew_ref, vnew_ref = refs[pos:pos + 2]
        pos += 2
    q_sc, k_sc, v_sc, s_sc, e_sc, l_sc = refs[pos:]

    rc = ROW_CHUNK
    lane = _iota((rc, PAIR), 1)
    first_half = (lane % (HEAD // 2)) < HEAD // 4

    def rot(x, cos, sin):
        partner = jnp.where(first_half, pltpu.roll(x, PAIR - HEAD // 4, 1), pltpu.roll(x, HEAD // 4, 1))
        return x * cos + partner * sin

    def normed(x, gain):
        return x * lax.rsqrt(_head_sumsq(x) * (1.0 / HEAD) + EPS) * gain

    if rope:
        k_sc[0:s_ctx, :] = ck_ref[0].astype(BF16)
        v_sc[0:s_ctx, :] = cv_ref[0].astype(BF16)

    def prep(c, carry):
        r0 = pl.multiple_of(c * rc, rc)
        if rope:
            cos = cos_ref[pl.ds(r0, rc), :]
            sin = sin_ref[pl.ds(r0, rc), :]
        for p in range(N_PAIRS):
            x = normed(q_ref[0, pl.ds(r0, rc), p * PAIR:(p + 1) * PAIR], qg_ref[...])
            if rope:
                x = rot(x, cos, sin)
            q_sc[pl.ds(r0, rc), p * PAIR:(p + 1) * PAIR] = (x * (HEAD ** -0.5)).astype(BF16)
        kn = []
        for g in range(KV_HEADS):
            x = normed(kk_ref[0, pl.ds(r0, rc), g * PAIR:(g + 1) * PAIR], kg_ref[...])
            kn.append(x)
            if rope:
                x = rot(x, cos, sin)
            k_sc[pl.ds(s_ctx + r0, rc), g * PAIR:(g + 1) * PAIR] = x.astype(BF16)
        vv = vv_ref[0, pl.ds(r0, rc), :]
        v_sc[pl.ds(s_ctx + r0, rc), :] = vv.astype(BF16)
        if not rope:
            knew_ref[0, pl.ds(r0, rc), :] = jnp.where(lane < HEAD, kn[0], kn[1])
            vnew_ref[0, pl.ds(r0, rc), :] = jnp.where(lane < HEAD, vv[:, :PAIR], vv[:, PAIR:])
        return carry

    lax.fori_loop(0, t // rc, prep, 0)

    tq = ATT_TQ
    lane_q = _iota((tq, PAIR), 1)

    pairs_per_group = N_PAIRS // KV_HEADS

    n_tiles = t // tq

    def scores(i, g, dst):
        r0 = i * tq if isinstance(i, int) else pl.multiple_of(i * tq, tq)
        tiles = []
        for p in range(g * pairs_per_group, (g + 1) * pairs_per_group):
            qp = q_sc[pl.ds(r0, tq), p * PAIR:(p + 1) * PAIR]
            for a in range(2):
                tiles.append(jnp.where(lane_q // HEAD == a, qp, jnp.zeros_like(qp)))
        qs = jnp.concatenate(tiles, axis=0)
        dst[...] = lax.dot_general(qs, k_sc[:, g * PAIR:(g + 1) * PAIR], (((1,), (1,)), ((), ())),
                                   preferred_element_type=F32)

    def softmax(src, e_dst, l_dst):
        rows_u, s_all = src.shape
        rb = ATT_SM_ROWS
        for r in range(rows_u // rb):
            rows = slice(r * rb, (r + 1) * rb)
            m = src[rows, 0:PAIR]
            for c in range(1, s_all // PAIR):
                m = jnp.maximum(m, src[rows, c * PAIR:(c + 1) * PAIR])
            mb = jnp.broadcast_to(jnp.max(m, axis=-1, keepdims=True), (rb, PAIR))
            acc = None
            for c in range(s_all // PAIR):
                e = jnp.exp(src[rows, c * PAIR:(c + 1) * PAIR] - mb)
                acc = e if acc is None else acc + e
                e_dst[rows, c * PAIR:(c + 1) * PAIR] = e.astype(BF16)
            l_dst[rows, :] = jnp.broadcast_to(jnp.sum(acc, axis=-1, keepdims=True), (rb, PAIR))

    def values(i, g, e_src, l_src):
        r0 = i * tq if isinstance(i, int) else pl.multiple_of(i * tq, tq)
        o = jnp.dot(e_src[...], v_sc[:, g * PAIR:(g + 1) * PAIR], preferred_element_type=F32) / l_src[...]
        for j in range(pairs_per_group):
            p = g * pairs_per_group + j
            pair = jnp.where(lane_q < HEAD, o[(2 * j) * tq:(2 * j + 1) * tq], o[(2 * j + 1) * tq:(2 * j + 2) * tq])
            o_ref[0, pl.ds(r0, tq), p * PAIR:(p + 1) * PAIR] = pair.astype(BF16)

    def step(i, slot):
        for g in range(KV_HEADS):
            scores(jnp.minimum(i + 1, n_tiles - 1), g, s_sc.at[1 - slot, g])
            softmax(s_sc.at[slot, g], e_sc.at[slot, g], l_sc.at[slot, g])
            values(jnp.maximum(i - 1, 0), g, e_sc.at[1 - slot, g], l_sc.at[1 - slot, g])

    for g in range(KV_HEADS):
        scores(0, g, s_sc.at[0, g])
        e_sc[1, g] = jnp.zeros(e_sc.shape[2:], BF16)
        l_sc[1, g] = jnp.ones(l_sc.shape[2:], F32)

    def attend(j, carry):
        step(2 * j, 0)
        step(2 * j + 1, 1)
        return carry

    assert n_tiles % 2 == 0
    lax.fori_loop(0, n_tiles // 2, attend, 0)
    for g in range(KV_HEADS):
        values(n_tiles - 1, g, e_sc.at[1, g], l_sc.at[1, g])


def _attn(proj, prm, rope_tabs, ctx_kv):
    bsz, t, _ = proj.shape
    rope = rope_tabs is not None
    s_ctx = ctx_kv[0].shape[1] if rope else 0
    s_all = s_ctx + t
    rows_u = (N_HEADS // KV_HEADS) * ATT_TQ
    seq = lambda w, j: pl.BlockSpec((1, t, w), lambda b: (b, 0, j), pipeline_mode=pl.Buffered(1))
    in_specs = [seq(INNER, COL_QB), seq(2 * PAIR, COL_KK), seq(2 * PAIR, COL_VV), _full((1, PAIR)), _full((1, PAIR))]
    args = [proj, proj, proj, prm["qgain"], prm["kgain"]]
    if rope:
        in_specs += [_full((t, PAIR)), _full((t, PAIR)),
                     pl.BlockSpec((1, s_ctx, 2 * PAIR), lambda b: (b, 0, 0)),
                     pl.BlockSpec((1, s_ctx, 2 * PAIR), lambda b: (b, 0, 0))]
        args += [rope_tabs[0], rope_tabs[1], ctx_kv[0], ctx_kv[1]]
    out_shape = [jax.ShapeDtypeStruct((bsz, t, INNER), BF16)]
    out_specs = [pl.BlockSpec((1, t, INNER), lambda b: (b, 0, 0))]
    if not rope:
        out_shape += [jax.ShapeDtypeStruct((bsz, t, PAIR), F32)] * 2
        out_specs += [pl.BlockSpec((1, t, PAIR), lambda b: (b, 0, 0))] * 2
    res = pl.pallas_call(
        functools.partial(_attn_body, t=t, s_ctx=s_ctx, rope=rope),
        out_shape=out_shape,
        grid=(bsz,),
        in_specs=in_specs,
        out_specs=out_specs,
        scratch_shapes=[pltpu.VMEM((t, INNER), BF16), pltpu.VMEM((s_all, 2 * PAIR), BF16),
                        pltpu.VMEM((s_all, 2 * PAIR), BF16),
                        pltpu.VMEM((2, KV_HEADS, rows_u, s_all), F32), pltpu.VMEM((2, KV_HEADS, rows_u, s_all), BF16),
                        pltpu.VMEM((2, KV_HEADS, rows_u, PAIR), F32)],
        compiler_params=_params(),
        name="attn",
    )(*args)
    return res


def _dn_body(*refs, t, has_init, emit_state):
    (q_ref, k_ref, v_ref, gate_ref, ba_ref, baT_ref, cwq_ref, cwk_ref, cwv_ref, cbq_ref, cbk_ref, cbv_ref,
     alog_row_ref, bias_row_ref, alog_col_ref, bias_col_ref, gnorm_ref) = refs[:17]
    pos = 17
    s0_ref = None
    if has_init:
        s0_ref = refs[pos]
        pos += 1
    o_ref = refs[pos]
    pos += 1
    st_ref = None
    if emit_state:
        st_ref = refs[pos]
        pos += 1
    qn_sc, kn_sc, vc_sc, bg_sc, bgT_sc, of_sc, ob_sc, s_sc = refs[pos:]

    rc = ROW_CHUNK
    cl = DN_CHUNK
    nc = t // cl
    nrc = t // rc
    hp = lax.Precision.HIGHEST
    nb = 2 * N_HEADS

    lane_rc = _iota((rc, PAIR), 1)
    row_t = _iota((2 * nb, rc), 0)

    def prep(c, carry):
        r0 = pl.multiple_of(c * rc, rc)
        for src, cw, cb, dst, norm in ((q_ref, cwq_ref, cbq_ref, qn_sc, True), (k_ref, cwk_ref, cbk_ref, kn_sc, True),
                                       (v_ref, cwv_ref, cbv_ref, vc_sc, False)):
            x = _silu(_dwconv(_conv_window(src, r0, c, nrc, rc, t), cw[...], cb[...], rc))
            if norm:
                tiles = []
                for p in range(N_PAIRS):
                    xp = x[:, p * PAIR:(p + 1) * PAIR]
                    tiles.append(xp * lax.rsqrt(_head_sumsq(xp) + EPS))
                x = jnp.concatenate(tiles, axis=1)
                if dst is qn_sc:
                    x = x * (HEAD ** -0.5)
            dst[pl.ds(r0, rc), :] = x
        ba = ba_ref[0, pl.ds(r0, rc), :]
        beta = jax.nn.sigmoid(ba)
        gdec = -jnp.exp(alog_row_ref[...]) * _softplus(ba + bias_row_ref[...])
        bg_sc[pl.ds(r0, rc), :] = jnp.where(lane_rc < nb, beta, gdec)
        baT = baT_ref[0, :, pl.ds(r0, rc)]
        betaT = jax.nn.sigmoid(baT)
        gdecT = -jnp.exp(alog_col_ref[...]) * _softplus(baT + bias_col_ref[...])
        bgT = jnp.where(row_t < nb, betaT, gdecT)
        for k in range(rc // cl):
            bgT_sc[c * (rc // cl) + k] = bgT[:, k * cl:(k + 1) * cl]
        return carry

    lax.fori_loop(0, nrc, prep, 0)

    rr = _iota((PAIR, PAIR), 0)
    cc = _iota((PAIR, PAIR), 1)
    same = (rr // HEAD) == (cc // HEAD)
    lane_c = _iota((cl, PAIR), 1)
    ii = _iota((cl, PAIR), 0)
    jj = lane_c % HEAD
    first = lane_c < HEAD
    lo = _lo_tri(cl)
    up = _up_tri(cl)
    eye = (ii == jj).astype(F32)
    offdiag = (ii != jj).astype(F32)
    blk_base = ii // DN_BASE == jj // DN_BASE
    off_masks = []
    b = DN_BASE
    while b < cl:
        off_masks.append((ii // (2 * b) == jj // (2 * b)) & (ii // b != jj // b))
        b *= 2

    for d in range(2):
        for p in range(N_PAIRS):
            if has_init:
                b0 = s0_ref[0, d, (2 * p) * HEAD:(2 * p + 1) * HEAD, :]
                b1 = s0_ref[0, d, (2 * p + 1) * HEAD:(2 * p + 2) * HEAD, :]
                z64 = jnp.zeros_like(b0)
                s_sc[d, p] = jnp.concatenate([jnp.concatenate([b0, z64], axis=1),
                                              jnp.concatenate([z64, b1], axis=1)], axis=0)
            else:
                s_sc[d, p] = jnp.zeros((PAIR, PAIR), F32)

    def bdiag(x):
        xb = x.astype(BF16)
        zero = jnp.zeros_like(xb)
        return jnp.concatenate([jnp.where(first, xb, zero), jnp.where(first, zero, xb)], axis=0)

    def pdot(x, y_bd):
        return jnp.dot(x.astype(BF16), y_bd, preferred_element_type=F32)

    def cols2(m, c0):
        return jnp.where(first, m[:, c0:c0 + 1], m[:, c0 + 1:c0 + 2])

    group = min(DN_GROUP, nc)
    systems =[(d, j, p) for j in range(group) for d in range(2) for p in range(N_PAIRS)]

    def chunk(k, carry):
        dirs = {}
        for d in range(2):
            for j in range(group):
                c = k * group + j if d == 0 else nc - 1 - (k * group + j)
                r0 = pl.multiple_of(c * cl, cl)
                bg = bg_sc[pl.ds(r0, cl), :]
                bgT = bgT_sc[c]
                tri = lo if d == 0 else up
                triT = up if d == 0 else lo
                gc = jnp.dot(tri, bg, precision=hp, preferred_element_type=F32)
                gcT = jnp.dot(bgT, triT, precision=hp, preferred_element_type=F32)
                dirs[d, j] = (r0, bg, gc, gcT)
        st = []
        for d, j, p in systems:
            r0, bg, gc, gcT = dirs[d, j]
            last = cl - 1 if d == 0 else 0
            incl = (jj <= ii) if d == 0 else (jj >= ii)
            h0i = d * N_HEADS + 2 * p
            beta = cols2(bg, h0i)
            gcm = cols2(gc, nb + h0i)
            rgc = jnp.concatenate([gcT[nb + h0i:nb + h0i + 1, :], gcT[nb + h0i + 1:nb + h0i + 2, :]], axis=1)
            glast = jnp.where(first[0:1], gc[last:last + 1, nb + h0i:nb + h0i + 1],
                              gc[last:last + 1, nb + h0i + 1:nb + h0i + 2])
            decay = jnp.exp(jnp.where(incl, gcm - rgc, -jnp.inf))
            cols = pl.ds(p * PAIR, PAIR)
            kp = kn_sc[pl.ds(r0, cl), cols]
            qp = qn_sc[pl.ds(r0, cl), cols]
            vp = vc_sc[pl.ds(r0, cl), cols]
            egc = jnp.exp(gcm)
            kb = kp * beta
            st.append(dict(r0=r0, cols=cols, glast=glast, decay=decay, kp=kp, kb=kb, qp=qp, qg=qp * egc,
                           kd=kp * jnp.exp(glast - gcm), rhs_u=vp * beta, rhs_w=kb * egc))
        for e in st:
            k_bd = bdiag(e["kp"])
            e["gm"] = lax.dot_general(e["kb"].astype(BF16), k_bd, (((1,), (1,)), ((), ())), preferred_element_type=F32)
            e["am"] = lax.dot_general(e["qp"].astype(BF16), k_bd, (((1,), (1,)), ((), ())), preferred_element_type=F32)
        for e in st:
            e["m"] = e["gm"] * e["decay"] * offdiag
            e["aq"] = e["am"] * e["decay"]
            e["n1"] = jnp.where(blk_base, e["m"], 0.0)
        for e in st:
            e["n2"] = pdot(e["n1"], bdiag(e["n1"]))
        for e in st:
            e["n2_bd"] = bdiag(e["n2"])
            e["n4"] = pdot(e["n2"], e["n2_bd"])
        for e in st:
            e["pm"] = eye + e["n2"] + e["n4"] + pdot(e["n4"], e["n2_bd"])
        for e in st:
            e["x"] = e["pm"] - pdot(e["n1"], bdiag(e["pm"]))
        for off_mask in off_masks:
            for e in st:
                e["cx"] = pdot(jnp.where(off_mask, e["m"], 0.0), bdiag(e["x"]))
            for e in st:
                e["x"] = e["x"] - pdot(e["x"], bdiag(e["cx"]))
        for e in st:
            rhs_bd = jnp.concatenate([bdiag(e["rhs_u"]), bdiag(e["rhs_w"])], axis=1)
            e["sol"] = pdot(e["x"], rhs_bd)
        state = {(d, p): s_sc[d, p] for d in range(2) for p in range(N_PAIRS)}
        for j in range(group):
            cur = [((d, p), e) for (d, jj_, p), e in zip(systems, st) if jj_ == j]
            for key, e in cur:
                e["s_bd"] = state[key].astype(BF16)
                e["vnew"] = e["sol"][:, :PAIR] - pdot(e["sol"][:, PAIR:], e["s_bd"])
            for (d, p), e in cur:
                dst = of_sc if d == 0 else ob_sc
                dst[pl.ds(e["r0"], cl), e["cols"]] = pdot(e["qg"], e["s_bd"]) + pdot(e["aq"], bdiag(e["vnew"]))
                upd = jnp.where(same, _dot_tn(e["kd"], e["vnew"]), 0.0)
                gl_rows = jnp.where(rr[:, 0:1] < HEAD, e["glast"][:, 0:1], e["glast"][:, HEAD:HEAD + 1])
                state[d, p] = state[d, p] * jnp.exp(gl_rows) + upd
        for (d, p), s in state.items():
            s_sc[d, p] = s
        return carry

    lax.fori_loop(0, nc // group, chunk, 0)

    if emit_state:
        for d in range(2):
            for p in range(N_PAIRS):
                s = s_sc[d, p]
                st_ref[0, d, (2 * p) * HEAD:(2 * p + 1) * HEAD, :] = s[:HEAD, :HEAD]
                st_ref[0, d, (2 * p + 1) * HEAD:(2 * p + 2) * HEAD, :] = s[HEAD:, HEAD:]

    def finish(c, carry):
        r0 = pl.multiple_of(c * rc, rc)
        for p in range(N_PAIRS):
            cols = pl.ds(p * PAIR, PAIR)
            o = of_sc[pl.ds(r0, rc), cols] + ob_sc[pl.ds(r0, rc), cols]
            on = o * lax.rsqrt(_head_sumsq(o) * (1.0 / HEAD) + EPS) * gnorm_ref[...]
            o_ref[0, pl.ds(r0, rc), cols] = (on * _silu(gate_ref[0, pl.ds(r0, rc), cols])).astype(BF16)
        return carry

    lax.fori_loop(0, nrc, finish, 0)


def _dn(proj, projT, prm, s0, *, emit_state):
    bsz, t, _ = proj.shape
    nc = t // DN_CHUNK
    has_init = s0 is not None
    seq = lambda w, j: pl.BlockSpec((1, t, w), lambda b: (b, 0, j), pipeline_mode=pl.Buffered(1))
    in_specs = [seq(INNER, COL_QC), seq(INNER, COL_KC), seq(INNER, COL_VC), seq(INNER, COL_GC), seq(PAIR, COL_BA),
                pl.BlockSpec((1, 4 * N_HEADS, t), lambda b: (b, ROWT_BA, 0), pipeline_mode=pl.Buffered(1))]
    args = [proj] * 5 + [projT]
    for name in ("cwq", "cwk", "cwv", "cbq", "cbk", "cbv", "alog_row", "bias_row", "alog_col", "bias_col", "gnorm"):
        a = prm[name]
        in_specs.append(_full(a.shape))
        args.append(a)
    if has_init:
        in_specs.append(pl.BlockSpec((1, 2, INNER, HEAD), lambda b: (b, 0, 0, 0)))
        args.append(s0)
    out_shape = [jax.ShapeDtypeStruct((bsz, t, INNER), BF16)]
    out_specs = [pl.BlockSpec((1, t, INNER), lambda b: (b, 0, 0))]
    if emit_state:
        out_shape.append(jax.ShapeDtypeStruct((bsz, 2, INNER, HEAD), F32))
        out_specs.append(pl.BlockSpec((1, 2, INNER, HEAD), lambda b: (b, 0, 0, 0)))
    res = pl.pallas_call(
        functools.partial(_dn_body, t=t, has_init=has_init, emit_state=emit_state),
        out_shape=out_shape,
        grid=(bsz,),
        in_specs=in_specs,
        out_specs=out_specs,
        scratch_shapes=[pltpu.VMEM((t, INNER), F32), pltpu.VMEM((t, INNER), F32), pltpu.VMEM((t, INNER), F32),
                        pltpu.VMEM((t, PAIR), F32), pltpu.VMEM((nc, 4 * N_HEADS, DN_CHUNK), F32),
                        pltpu.VMEM((t, INNER), F32), pltpu.VMEM((t, INNER), F32),
                        pltpu.VMEM((2, N_PAIRS, PAIR, PAIR), F32)],
        compiler_params=_params(),
        name="deltanet",
    )(*args)
    return (res[0], res[1]) if emit_state else (res[0], None)


def _lru_body(*refs, t, has_init, emit_state):
    x_ref, y_ref, cw_ref, cb_ref, wbd_ref, bias_ref, lam_ref = refs[:7]
    pos = 7
    h0_ref = None
    if has_init:
        h0_ref = refs[pos]
        pos += 1
    o_ref = refs[pos]
    pos += 1
    st_ref = None
    if emit_state:
        st_ref = refs[pos]
        pos += 1
    af_sc, uf_sc, ab_sc, ub_sc = refs[pos:]

    rc = ROW_CHUNK
    nrc = t // rc
    row8 = _iota((HALO, INNER), 0)
    sp_lam = _softplus(-lam_ref[...])

    def gates(c, carry):
        r0 = pl.multiple_of(c * rc, rc)
        xl_all = _dwconv(_conv_window(x_ref, r0, c, nrc, rc, t), cw_ref[...], cb_ref[...], rc)
        for p in range(N_PAIRS):
            cols = pl.ds(p * PAIR, PAIR)
            xl = xl_all[:, p * PAIR:(p + 1) * PAIR]
            pre = _dot(xl, wbd_ref[p]) + bias_ref[p]
            for d, (a_sc, u_sc) in enumerate(((af_sc, uf_sc), (ab_sc, ub_sc))):
                r = jax.nn.sigmoid(pre[:, d * PAIR:(d + 1) * PAIR])
                ig = jax.nn.sigmoid(pre[:, (2 + d) * PAIR:(3 + d) * PAIR])
                log_a = -LRU_C * r * sp_lam[d:d + 1, p * PAIR:(p + 1) * PAIR]
                a_sc[pl.ds(r0, rc), cols] = jnp.exp(log_a)
                u_sc[pl.ds(r0, rc), cols] = jnp.sqrt(-jnp.tanh(log_a) * (jnp.exp(2.0 * log_a) + 1.0)) * ig * xl
        return carry

    lax.fori_loop(0, nrc, gates, 0)

    if has_init:
        cf0 = h0_ref[0, 0:1, :]
        cb0 = h0_ref[0, 1:2, :]
    else:
        cf0 = jnp.zeros((1, INNER), F32)
        cb0 = cf0

    def scan(k, carry):
        cf, cb = carry
        rf = pl.multiple_of(k * HALO, HALO)
        a8 = af_sc[pl.ds(rf, HALO), :]
        b8 = uf_sc[pl.ds(rf, HALO), :]
        for s in (1, 2, 4):
            ok = row8 >= s
            b8 = jnp.where(ok, a8 * pltpu.roll(b8, s, 0) + b8, b8)
            a8 = jnp.where(ok, a8 * pltpu.roll(a8, s, 0), a8)
        hf = a8 * cf + b8
        uf_sc[pl.ds(rf, HALO), :] = hf
        rb = pl.multiple_of(t - HALO - k * HALO, HALO)
        a8 = ab_sc[pl.ds(rb, HALO), :]
        b8 = ub_sc[pl.ds(rb, HALO), :]
        for s in (1, 2, 4):
            ok = row8 < HALO - s
            b8 = jnp.where(ok, a8 * pltpu.roll(b8, HALO - s, 0) + b8, b8)
            a8 = jnp.where(ok, a8 * pltpu.roll(a8, HALO - s, 0), a8)
        hb = a8 * cb + b8
        ub_sc[pl.ds(rb, HALO), :] = hb
        return hf[HALO - 1:HALO, :], hb[0:1, :]

    cf, cb = lax.fori_loop(0, t // HALO, scan, (cf0, cb0))
    if emit_state:
        st_ref[0, 0:1, :] = cf
        st_ref[0, 1:2, :] = cb

    def finish(c, carry):
        r0 = pl.multiple_of(c * rc, rc)
        y = y_ref[0, pl.ds(r0, rc), :]
        gelu = 0.5 * y * (1.0 + jnp.tanh(math.sqrt(2.0 / math.pi) * (y + 0.044715 * (y * y * y))))
        o_ref[0, pl.ds(r0, rc), :] = ((uf_sc[pl.ds(r0, rc), :] + ub_sc[pl.ds(r0, rc), :]) * gelu).astype(BF16)
        return carry

    lax.fori_loop(0, nrc, finish, 0)


def _lru(proj, prm, h0, *, emit_state):
    bsz, t, _ = proj.shape
    has_init = h0 is not None
    seq = lambda w, j: pl.BlockSpec((1, t, w), lambda b: (b, 0, j), pipeline_mode=pl.Buffered(1))
    in_specs = [seq(INNER, COL_XD), seq(INNER, COL_YD)]
    args = [proj, proj]
    for name in ("cw", "cb", "wbd", "bias", "lam"):
        a = prm[name]
        in_specs.append(_full(a.shape))
        args.append(a)
    if has_init:
        in_specs.append(pl.BlockSpec((1, 2, INNER), lambda b: (b, 0, 0)))
        args.append(h0)
    out_shape = [jax.ShapeDtypeStruct((bsz, t, INNER), BF16)]
    out_specs = [pl.BlockSpec((1, t, INNER), lambda b: (b, 0, 0))]
    if emit_state:
        out_shape.append(jax.ShapeDtypeStruct((bsz, 2, INNER), F32))
        out_specs.append(pl.BlockSpec((1, 2, INNER), lambda b: (b, 0, 0)))
    res = pl.pallas_call(
        functools.partial(_lru_body, t=t, has_init=has_init, emit_state=emit_state),
        out_shape=out_shape,
        grid=(bsz,),
        in_specs=in_specs,
        out_specs=out_specs,
        scratch_shapes=[pltpu.VMEM((t, INNER), F32)] * 4,
        compiler_params=_params(),
        name="rglru",
    )(*args)
    return (res[0], res[1]) if emit_state else (res[0], None)


def _pad_cols(x, n):
    return jnp.pad(x, ((0, 0), (0, n - x.shape[1])))


def _layer_params(l, w_in, ssm_conv_w, ssm_conv_b, ssm_a_log, ssm_dt_bias, ssm_d, ssm_norm, attn_q_norm, attn_k_norm,
                  dn_conv_w, dn_conv_b, dn_a_log, dn_dt_bias, dn_norm, lru_conv_w, lru_conv_b, lru_w_a, lru_b_a,
                  lru_w_i, lru_b_i, lru_lambda):
    w = w_in[l]
    o = 0

    def take(n):
        nonlocal o
        part = w[:, o:o + n]
        o += n
        return part

    z_a, xs_a, bc_a, dt_a = take(INNER), take(INNER), take(2 * PAIR), take(2 * N_HEADS)
    q_b, k_b, v_b = take(INNER), take(PAIR), take(PAIR)
    q_c, k_c, v_c = take(INNER), take(INNER), take(INNER)
    beta_c, a_c, gate_c = take(2 * N_HEADS), take(2 * N_HEADS), take(INNER)
    x_d, y_d = take(INNER), take(INNER)
    gate_raw = take(N_BRANCH * D_MODEL)

    def dup(x):
        return jnp.concatenate([x[:, :HEAD], x[:, :HEAD], x[:, HEAD:], x[:, HEAD:]], axis=1)

    nh2 = 2 * N_HEADS
    ba_c = jnp.concatenate([beta_c, a_c], axis=1)
    w_proj = jnp.concatenate([z_a, xs_a, q_b, q_c, k_c, v_c, gate_c, x_d, y_d, bc_a, dup(k_b), dup(v_b),
                              _pad_cols(dt_a, PAIR), _pad_cols(ba_c, PAIR)], axis=1)
    assert w_proj.shape[1] == PROJ_N
    prm = {
        "w_proj": w_proj.astype(BF16),
        "w_projT": jnp.concatenate([ba_c, dt_a], axis=1).T.astype(BF16),
        "w_gate": gate_raw.astype(BF16),
    }
    cw, cb = ssm_conv_w[l], ssm_conv_b[l][None, :]
    dtb = ssm_dt_bias[l].reshape(1, nh2)
    alog = ssm_a_log[l].reshape(1, nh2)
    prm["ssd"] = {
        "cwx": cw[:, :INNER], "cbx": cb[:, :INNER], "cwbc": cw[:, INNER:], "cbbc": cb[:, INNER:],
        "dtb_row": _pad_cols(dtb, PAIR), "dtb_col": jnp.broadcast_to(dtb.T, (nh2, SSD_CHUNK)),
        "alog_row": _pad_cols(alog, PAIR), "alog_col": jnp.broadcast_to(alog.T, (nh2, SSD_CHUNK)),
        "dskip": jnp.repeat(ssm_d[l], HEAD)[None, :], "gnorm": ssm_norm[l][None, :],
    }
    prm["att"] = {"qgain": jnp.tile(attn_q_norm[l], 2)[None, :], "kgain": jnp.tile(attn_k_norm[l], 2)[None, :]}
    dcw, dcb = dn_conv_w[l], dn_conv_b[l][None, :]
    zeros16 = jnp.zeros((1, nh2), F32)
    d_alog = jnp.concatenate([zeros16, dn_a_log[l].reshape(1, nh2)], axis=1)
    d_bias = jnp.concatenate([zeros16, dn_dt_bias[l].reshape(1, nh2)], axis=1)
    prm["dn"] = {
        "cwq": dcw[:, :INNER], "cwk": dcw[:, INNER:2 * INNER], "cwv": dcw[:, 2 * INNER:],
        "cbq": dcb[:, :INNER], "cbk": dcb[:, INNER:2 * INNER], "cbv": dcb[:, 2 * INNER:],
        "alog_row": _pad_cols(d_alog, PAIR), "bias_row": _pad_cols(d_bias, PAIR),
        "alog_col": jnp.broadcast_to(d_alog.T, (2 * nh2, ROW_CHUNK)),
        "bias_col": jnp.broadcast_to(d_bias.T, (2 * nh2, ROW_CHUNK)),
        "gnorm": jnp.tile(dn_norm[l], 2)[None, :],
    }
    wa, wi = lru_w_a[l], lru_w_i[l]
    z64 = jnp.zeros((HEAD, HEAD), F32)
    wbd, bias = [], []
    for p in range(N_PAIRS):
        blocks = []
        for wsrc in (wa, wi):
            for d in range(2):
                top = jnp.concatenate([wsrc[d, 2 * p], z64], axis=1)
                bot = jnp.concatenate([z64, wsrc[d, 2 * p + 1]], axis=1)
                blocks.append(jnp.concatenate([top, bot], axis=0))
        wbd.append(jnp.concatenate(blocks, axis=1))
        bias.append(jnp.concatenate([lru_b_a[l][0, p * PAIR:(p + 1) * PAIR], lru_b_a[l][1, p * PAIR:(p + 1) * PAIR],
                                     lru_b_i[l][0, p * PAIR:(p + 1) * PAIR], lru_b_i[l][1, p * PAIR:(p + 1) * PAIR]])[None, :])
    prm["lru"] = {"cw": lru_conv_w[l], "cb": lru_conv_b[l][None, :], "wbd": jnp.stack(wbd).astype(BF16),
                  "bias": jnp.stack(bias), "lam": lru_lambda[l]}
    return prm


def _rope_tables(t):
    n_freq = HEAD // 4
    inv = ROPE_THETA ** (-jnp.arange(n_freq, dtype=F32) / n_freq)
    rows = t // GRID_W
    row = jnp.repeat(jnp.arange(rows, dtype=F32), GRID_W)
    col = jnp.tile(jnp.arange(GRID_W, dtype=F32), rows)
    ang_r = row[:, None] * inv
    ang_c = col[:, None] * inv
    ang = jnp.concatenate([ang_r, ang_r, ang_c, ang_c], axis=1)
    sign = jnp.tile(jnp.concatenate([-jnp.ones((n_freq,), F32), jnp.ones((n_freq,), F32)]), 2)
    cos = jnp.cos(ang)
    sin = jnp.sin(ang) * sign
    return jnp.tile(cos, (1, 2)), jnp.tile(sin, (1, 2))


def _dup_kv(x):
    return jnp.concatenate([x[:, :, 0], x[:, :, 0], x[:, :, 1], x[:, :, 1]], axis=-1)


def _trunk_layer(h, mod, lw, prm, rope_tabs, ctx, fin_gain, *, final):
    h = _ffn(h, mod, lw["norm_ffn1"], lw["ffn1_w13"], lw["ffn1_w2"], fin_gain, rows=(0, 1, 2), final=False)
    gain = lw["norm_mix"]
    emit = ctx is None
    proj, projT = _proj(h, mod, gain, prm["w_proj"], prm["w_projT"])
    if emit:
        o_a, st_ssm = _ssd(proj, projT, prm["ssd"], None, emit_state=True)
        o_b, k_new, v_new = _attn(proj, prm["att"], None, None)
        o_c, st_dn = _dn(proj, projT, prm["dn"], None, emit_state=True)
        o_d, st_lru = _lru(proj, prm["lru"], None, emit_state=True)
        new_ctx = (k_new, v_new, st_ssm, st_dn, st_lru)
    else:
        ck, cv, ssm0, dn0, lru0 = ctx
        o_a, _ = _ssd(proj, projT, prm["ssd"], ssm0, emit_state=False)
        (o_b,) = _attn(proj, prm["att"], rope_tabs, (ck, cv))
        o_c, _ = _dn(proj, projT, prm["dn"], dn0, emit_state=False)
        o_d, _ = _lru(proj, prm["lru"], lru0, emit_state=False)
        new_ctx = None
    h = _merge(h, mod, gain, (o_a, o_b, o_c, o_d), prm["w_gate"], lw["w_branch"], lw["w_out"])
    h = _ffn(h, mod, lw["norm_ffn2"], lw["ffn2_w13"], lw["ffn2_w2"], fin_gain, rows=(6, 7, 8), final=final)
    return h, new_ctx


def kernel(x_prompt, x_sample, cache_k, cache_v, state_ssm, state_delta, state_lru, c, c_ctx,
           w_ada, b_ada, norm_ffn1, ffn1_w13, ffn1_w2, norm_mix, w_in,
           ssm_conv_w, ssm_conv_b, ssm_a_log, ssm_dt_bias, ssm_d, ssm_norm,
           attn_q_norm, attn_k_norm,
           dn_conv_w, dn_conv_b, dn_a_log, dn_dt_bias, dn_norm,
           lru_conv_w, lru_conv_b, lru_w_a, lru_b_a, lru_w_i, lru_b_i, lru_lambda,
           w_branch, w_out, norm_ffn2, ffn2_w13, ffn2_w2, final_norm):
    depth = w_in.shape[0]
    bsz_p, t_p, _ = x_prompt.shape
    bsz_s, t_s, _ = x_sample.shape
    assert bsz_s + 1 <= MOD_ROWS
    cvec = jnp.concatenate([c, c_ctx[None], jnp.zeros((MOD_ROWS - bsz_s - 1, D_MODEL), F32)], axis=0)
    mod_all = _adaln(cvec, w_ada, b_ada)
    mod_all = mod_all.reshape(depth, MOD_ROWS, N_MOD, D_MODEL)
    mod_all = jnp.pad(mod_all, ((0, 0), (0, 0), (0, MOD_ROWS - N_MOD), (0, 0)))
    rope_tabs = _rope_tables(t_s)
    fin_gain = final_norm[None, :]

    hp, hs = x_prompt, x_sample
    ks, vs, ssm_s, dn_s, lru_s = [], [], [], [], []
    for l in range(depth):
        prm = _layer_params(l, w_in, ssm_conv_w, ssm_conv_b, ssm_a_log, ssm_dt_bias, ssm_d, ssm_norm,
                            attn_q_norm, attn_k_norm, dn_conv_w, dn_conv_b, dn_a_log, dn_dt_bias, dn_norm,
                            lru_conv_w, lru_conv_b, lru_w_a, lru_b_a, lru_w_i, lru_b_i, lru_lambda)
        lw = {
            "norm_ffn1": norm_ffn1[l][None, :], "ffn1_w13": ffn1_w13[l].astype(BF16), "ffn1_w2": ffn1_w2[l].astype(BF16),
            "norm_mix": norm_mix[l][None, :], "w_branch": w_branch[l].astype(BF16), "w_out": w_out[l].astype(BF16),
            "norm_ffn2": norm_ffn2[l][None, :], "ffn2_w13": ffn2_w13[l].astype(BF16), "ffn2_w2": ffn2_w2[l].astype(BF16),
        }
        final = l == depth - 1
        mod_lat = mod_all[l, :bsz_s]
        mod_ctx = mod_all[l, bsz_s:bsz_s + 1]
        hp, st = _trunk_layer(hp, mod_ctx, lw, prm, None, None, fin_gain, final=final)
        k_c, v_c, st_ssm, st_dn, st_lru = st
        ks.append(k_c.reshape(bsz_p, t_p, KV_HEADS, HEAD))
        vs.append(v_c.reshape(bsz_p, t_p, KV_HEADS, HEAD))
        ssm_s.append(st_ssm.reshape(bsz_p, 2, N_HEADS, HEAD, HEAD))
        dn_s.append(st_dn.reshape(bsz_p, 2, N_HEADS, HEAD, HEAD))
        lru_s.append(st_lru)
        ctx_l = (_dup_kv(cache_k[:, l]), _dup_kv(cache_v[:, l]),
                 state_ssm[:, l].reshape(bsz_s, 2, INNER, HEAD), state_delta[:, l].reshape(bsz_s, 2, INNER, HEAD),
                 state_lru[:, l])
        hs, _ = _trunk_layer(hs, mod_lat, lw, prm, rope_tabs, ctx_l, fin_gain, final=final)
    return (hp, hs, jnp.stack(ks, axis=1), jnp.stack(vs, axis=1), jnp.stack(ssm_s, axis=1),
            jnp.stack(dn_s, axis=1), jnp.stack(lru_s, axis=1))
```

```python
import functools
import math

import jax
import jax.numpy as jnp
from jax import lax
from jax.experimental import pallas as pl
from jax.experimental.pallas import tpu as pltpu

F32 = jnp.float32
BF16 = jnp.bfloat16

D_MODEL = 1024
D_FF = 2816
N_MOD = 9
MOD_ROWS = 16
EPS = 1e-6
GRID_W = 64
CONV_W = 4
CONV_LP = CONV_W // 2
HALO = 8
HEAD = 64
PAIR = 2 * HEAD
N_HEADS = 8
N_PAIRS = N_HEADS // 2
INNER = N_HEADS * HEAD
KV_HEADS = 2
SSM_GROUPS = 2
ROPE_THETA = 10000.0
LRU_C = 8.0
N_BRANCH = 4
SSD_CHUNK = 256
DN_CHUNK = 64
DN_BASE = 8
DN_GROUP = 4
ROW_CHUNK = 256
ATT_TQ = 64
ATT_SM_ROWS = 64
FF_CHUNK = 256
VMEM_LIMIT = 56 * 1024 * 1024

COL_Z, COL_XS, COL_QB, COL_QC, COL_KC, COL_VC, COL_GC, COL_XD, COL_YD = range(9)
COL_BC, COL_KK, COL_VV = 18, 19, 20
COL_DT, COL_BA = 42, 43
PROJ_N = 44 * PAIR
ROWT_BA, ROWT_DT = 0, 2
PROJ_T = 48


def _dot(a, b):
    return jnp.dot(a.astype(BF16), b.astype(BF16), preferred_element_type=F32)


def _dot_nt(a, b):
    return lax.dot_general(a.astype(BF16), b.astype(BF16), (((1,), (1,)), ((), ())),
                           preferred_element_type=F32)


def _dot_tn(a, b):
    return lax.dot_general(a.astype(BF16), b.astype(BF16), (((0,), (0,)), ((), ())),
                           preferred_element_type=F32)


def _split3(x):
    hi = x.astype(BF16)
    r1 = x - hi.astype(F32)
    mid = r1.astype(BF16)
    lo = (r1 - mid.astype(F32)).astype(BF16)
    return hi, mid, lo


def _dot_sel(x, sel):
    hi, mid, lo = _split3(x)
    s = sel.astype(BF16)
    return (jnp.dot(hi, s, preferred_element_type=F32) + jnp.dot(mid, s, preferred_element_type=F32)
            + jnp.dot(lo, s, preferred_element_type=F32))


def _sel_dot(sel, x):
    hi, mid, lo = _split3(x)
    s = sel.astype(BF16)
    return (jnp.dot(s, hi, preferred_element_type=F32) + jnp.dot(s, mid, preferred_element_type=F32)
            + jnp.dot(s, lo, preferred_element_type=F32))


def _iota(shape, axis):
    return lax.broadcasted_iota(jnp.int32, shape, axis)


def _silu(x):
    return x * jax.nn.sigmoid(x)


def _softplus(x):
    return jnp.maximum(x, 0.0) + jnp.log1p(jnp.exp(-jnp.abs(x)))


def _rms(x, g):
    return x * lax.rsqrt(jnp.mean(x * x, axis=-1, keepdims=True) + EPS) * g


def _head_ones():
    return (_iota((PAIR, PAIR), 0) // HEAD == _iota((PAIR, PAIR), 1) // HEAD).astype(F32)


def _head_sumsq(x):
    x2 = x * x
    hi = x2.astype(BF16)
    lo = (x2 - hi.astype(F32)).astype(BF16)
    ones = _head_ones().astype(BF16)
    return jnp.dot(hi, ones, preferred_element_type=F32) + jnp.dot(lo, ones, preferred_element_type=F32)


def _lo_tri(n):
    return (_iota((n, n), 1) <= _iota((n, n), 0)).astype(F32)


def _up_tri(n):
    return (_iota((n, n), 1) >= _iota((n, n), 0)).astype(F32)


def _conv_window(ref, r0, c, nchunks, rows, t_total):
    cur = ref[0, pl.ds(r0, rows), :]
    prev_start = pl.multiple_of(jnp.maximum(r0 - HALO, 0), HALO)
    next_start = pl.multiple_of(jnp.minimum(r0 + rows, t_total - HALO), HALO)
    prev = jnp.where(c > 0, ref[0, pl.ds(prev_start, HALO), :], 0.0)
    nxt = jnp.where(c < nchunks - 1, ref[0, pl.ds(next_start, HALO), :], 0.0)
    return jnp.concatenate([prev, cur, nxt], axis=0)


def _dwconv(ext, w, b, rows):
    out = b
    n = ext.shape[0]
    for j in range(CONV_W):
        sh = (CONV_LP - j) % n
        win = ext if sh == 0 else pltpu.roll(ext, sh, 0)
        out = out + w[j:j + 1, :] * win[HALO:HALO + rows, :]
    return out


def _expand_heads(n_src_rows, first, width):
    r = _iota((n_src_rows, width), 0)
    l = _iota((n_src_rows, width), 1)
    return (r - first == l // HEAD).astype(F32)


def _full(shape):
    zeros = (0,) * len(shape)
    return pl.BlockSpec(shape, lambda *_: zeros, pipeline_mode=pl.Buffered(1))


def _params():
    return pltpu.CompilerParams(vmem_limit_bytes=VMEM_LIMIT)


def _adaln_body(c_ref, w_ref, b_ref, o_ref):
    o_ref[0] = _dot(_silu(c_ref[...]), w_ref[0]) + b_ref[0]


def _adaln(cvec, w_ada, b_ada):
    depth = w_ada.shape[0]
    n = w_ada.shape[2]
    tn = n // N_MOD
    return pl.pallas_call(
        _adaln_body,
        out_shape=jax.ShapeDtypeStruct((depth, MOD_ROWS, n), F32),
        grid=(depth, N_MOD),
        in_specs=[pl.BlockSpec((MOD_ROWS, D_MODEL), lambda l, j: (0, 0)),
                  pl.BlockSpec((1, D_MODEL, tn), lambda l, j: (l, 0, j)),
                  pl.BlockSpec((1, 1, tn), lambda l, j: (l, 0, j))],
        out_specs=pl.BlockSpec((1, MOD_ROWS, tn), lambda l, j: (l, 0, j)),
        compiler_params=_params(),
        name="adaln",
    )(cvec, w_ada, b_ada.reshape(depth, 1, n))


def _ffn_body(h_ref, mod_ref, gain_ref, w13_ref, w2_ref, fin_ref, o_ref, acc_sc, *, rows, final):
    shift, scale, gate = rows
    ms = mod_ref[0]
    h = h_ref[0]
    xn = (_rms(h, gain_ref[...]) * (1.0 + ms[scale:scale + 1]) + ms[shift:shift + 1]).astype(BF16)
    for j in range(D_FF // FF_CHUNK):
        lo, hi = j * FF_CHUNK, (j + 1) * FF_CHUNK
        g = jnp.dot(xn, w13_ref[:, lo:hi], preferred_element_type=F32)
        u = jnp.dot(xn, w13_ref[:, D_FF + lo:D_FF + hi], preferred_element_type=F32)
        part = _dot(_silu(g) * u, w2_ref[lo:hi, :])
        if j == 0:
            acc_sc[...] = part
        else:
            acc_sc[...] += part
    hn = h + 0.5 * ms[gate:gate + 1] * acc_sc[...]
    o_ref[0] = _rms(hn, fin_ref[...]) if final else hn


def _ffn(h, mod, gain, w13, w2, fin_gain, *, rows, final):
    bsz, t, d = h.shape
    tm = min(512, t)
    mb = mod.shape[0]
    mod_ix = (lambda b, i: (b, 0, 0)) if mb > 1 else (lambda b, i: (0, 0, 0))
    return pl.pallas_call(
        functools.partial(_ffn_body, rows=rows, final=final),
        out_shape=jax.ShapeDtypeStruct((bsz, t, d), F32),
        grid=(bsz, t // tm),
        in_specs=[pl.BlockSpec((1, tm, d), lambda b, i: (b, i, 0)),
                  pl.BlockSpec((1, MOD_ROWS, d), mod_ix),
                  pl.BlockSpec((1, d), lambda b, i: (0, 0)),
                  _full((d, 2 * D_FF)), _full((D_FF, d)),
                  pl.BlockSpec((1, d), lambda b, i: (0, 0))],
        out_specs=pl.BlockSpec((1, tm, d), lambda b, i: (b, i, 0)),
        scratch_shapes=[pltpu.VMEM((tm, d), F32)],
        compiler_params=_params(),
        name="ffn",
    )(h, mod, gain, w13, w2, fin_gain)


def _proj_body(h_ref, mod_ref, gain_ref, w_ref, wt_ref, o_ref, ot_ref):
    ms = mod_ref[0]
    xn = (_rms(h_ref[0], gain_ref[...]) * (1.0 + ms[4:5]) + ms[3:4]).astype(BF16)
    ot_ref[0] = lax.dot_general(wt_ref[...], xn, (((1,), (1,)), ((), ())), preferred_element_type=F32)
    o_ref[0] = jnp.dot(xn, w_ref[...], preferred_element_type=F32)


def _proj(h, mod, gain, w, wt):
    bsz, t, d = h.shape
    n = w.shape[1]
    r = wt.shape[0]
    tm = min(256, t)
    mb = mod.shape[0]
    mod_ix = (lambda b, i: (b, 0, 0)) if mb > 1 else (lambda b, i: (0, 0, 0))
    return pl.pallas_call(
        _proj_body,
        out_shape=[jax.ShapeDtypeStruct((bsz, t, n), F32), jax.ShapeDtypeStruct((bsz, r, t), F32)],
        grid=(bsz, t // tm),
        in_specs=[pl.BlockSpec((1, tm, d), lambda b, i: (b, i, 0)),
                  pl.BlockSpec((1, MOD_ROWS, d), mod_ix),
                  pl.BlockSpec((1, d), lambda b, i: (0, 0)),
                  _full((d, n)), _full((r, d))],
        out_specs=[pl.BlockSpec((1, tm, n), lambda b, i: (b, i, 0)),
                   pl.BlockSpec((1, r, tm), lambda b, i: (b, 0, i))],
        compiler_params=_params(),
        name="proj",
    )(h, mod, gain, w, wt)


def _merge_body(h_ref, mod_ref, gain_ref, oa_ref, ob_ref, oc_ref, od_ref, wg_ref, wb_ref, wo_ref, o_ref):
    ms = mod_ref[0]
    h = h_ref[0]
    xn = (_rms(h, gain_ref[...]) * (1.0 + ms[4:5]) + ms[3:4]).astype(BF16)
    merged = None
    for n, br_ref in enumerate((oa_ref, ob_ref, oc_ref, od_ref)):
        gate = jax.nn.sigmoid(jnp.dot(xn, wg_ref[:, n * D_MODEL:(n + 1) * D_MODEL], preferred_element_type=F32))
        term = gate * jnp.dot(br_ref[0], wb_ref[n], preferred_element_type=F32)
        merged = term if merged is None else merged + term
    o_ref[0] = h + ms[5:6] * _dot(merged, wo_ref[...])


def _merge(h, mod, gain, branches, w_gate, w_branch, w_out):
    bsz, t, d = h.shape
    tm = min(512, t)
    mb = mod.shape[0]
    mod_ix = (lambda b, i: (b, 0, 0)) if mb > 1 else (lambda b, i: (0, 0, 0))
    br_spec = pl.BlockSpec((1, tm, INNER), lambda b, i: (b, i, 0))
    return pl.pallas_call(
        _merge_body,
        out_shape=jax.ShapeDtypeStruct((bsz, t, d), F32),
        grid=(bsz, t // tm),
        in_specs=[pl.BlockSpec((1, tm, d), lambda b, i: (b, i, 0)),
                  pl.BlockSpec((1, MOD_ROWS, d), mod_ix),
                  pl.BlockSpec((1, d), lambda b, i: (0, 0)),
                  br_spec, br_spec, br_spec, br_spec,
                  _full((d, N_BRANCH * d)), _full((N_BRANCH, INNER, d)), _full((d, d))],
        out_specs=pl.BlockSpec((1, tm, d), lambda b, i: (b, i, 0)),
        compiler_params=_params(),
        name="merge",
    )(h, mod, gain, *branches, w_gate, w_branch, w_out)


def _ssd_body(*refs, t, has_init, emit_state):
    (z_ref, xs_ref, bc_ref, dt_ref, dtT_ref, cwx_ref, cbx_ref, cwbc_ref, cbbc_ref,
     dtb_row_ref, dtb_col_ref, alog_row_ref, alog_col_ref, dskip_ref, gnorm_ref) = refs[:15]
    pos = 15
    h0_ref = None
    if has_init:
        h0_ref = refs[pos]
        pos += 1
    o_ref = refs[pos]
    pos += 1
    st_ref = None
    if emit_state:
        st_ref = refs[pos]
        pos += 1
    y_sc, cum_sc, cm_sc, dh_sc, tot_sc, hst_sc = refs[pos:]

    cl = SSD_CHUNK
    nc = t // cl
    lane = _iota((cl, PAIR), 1)
    ii = _iota((cl, cl), 0)
    jj = _iota((cl, cl), 1)
    lo = _lo_tri(cl)
    up = _up_tri(cl)
    hp = lax.Precision.HIGHEST
    a_row = -jnp.exp(alog_row_ref[...])
    a_col = -jnp.exp(alog_col_ref[...])
    sel_f = _expand_heads(PAIR, 0, INNER)
    sel_b = _expand_heads(PAIR, N_HEADS, INNER)

    def intra(c, carry):
        r0 = pl.multiple_of(c * cl, cl)
        xs = _silu(_dwconv(_conv_window(xs_ref, r0, c, nc, cl, t), cwx_ref[...], cbx_ref[...], cl))
        bc = _silu(_dwconv(_conv_window(bc_ref, r0, c, nc, cl, t), cwbc_ref[...], cbbc_ref[...], cl))
        bm = bc[:, :PAIR]
        cm = bc[:, PAIR:]
        dt = _softplus(dt_ref[0, pl.ds(r0, cl), :] + dtb_row_ref[...])
        dtT = _softplus(dtT_ref[0, :, pl.ds(r0, cl)] + dtb_col_ref[...])
        da = dt * a_row
        daT = dtT * a_col
        cum = jnp.where(lane < N_HEADS, _sel_dot(lo, da), _sel_dot(up, da))
        rowsel = _iota((2 * N_HEADS, cl), 0) < N_HEADS
        cumT = jnp.where(rowsel, _dot_sel(daT, up), _dot_sel(daT, lo))
        cum_sc[pl.ds(r0, cl), :] = cum
        cm_sc[pl.ds(r0, cl), :] = cm
        cb = []
        for g in range(SSM_GROUPS):
            cg = jnp.where(lane // HEAD == g, cm, 0.0)
            cb.append(_dot_nt(cg, bm))
        ypairs = []
        for p in range(N_PAIRS):
            xp = xs[:, p * PAIR:(p + 1) * PAIR].astype(BF16)
            halves = []
            for a in range(2):
                h = 2 * p + a
                g = h // (N_HEADS // SSM_GROUPS)
                hb = N_HEADS + h
                expo = jnp.where(jj <= ii, cum[:, h:h + 1] - cumT[h:h + 1, :], cum[:, hb:hb + 1] - cumT[hb:hb + 1, :])
                dtf = dtT[h:h + 1, :]
                dtb = dtT[hb:hb + 1, :]
                wdt = jnp.where(jj < ii, dtf, jnp.where(jj > ii, dtb, dtf + dtb))
                s = (cb[g] * jnp.exp(expo) * wdt).astype(BF16)
                halves.append(jnp.dot(s, xp, preferred_element_type=F32))
            ypairs.append(jnp.where(lane < HEAD, halves[0], halves[1]))
        y = jnp.concatenate(ypairs, axis=1) + dskip_ref[...] * xs
        y_sc[pl.ds(r0, cl), :] = y
        tot = jnp.where(lane[0:1] < N_HEADS, cum[cl - 1:cl, :], cum[0:1, :])
        tot_sc[c] = jnp.broadcast_to(tot, (HALO, PAIR))
        wexp = jnp.exp(tot - cum) * dt
        for d, sel in enumerate((sel_f, sel_b)):
            xw = xs * _dot_sel(wexp, sel)
            for p in range(N_PAIRS):
                g = p // (N_PAIRS // SSM_GROUPS)
                bg = jnp.where(lane // HEAD == g, bm, 0.0)
                dh_sc[c, d, p] = _dot_tn(xw[:, p * PAIR:(p + 1) * PAIR], bg)
        return carry

    lax.fori_loop(0, nc, intra, 0)

    rr = _iota((PAIR, PAIR), 0)
    for d in range(2):
        sel = sel_f if d == 0 else sel_b
        for p in range(N_PAIRS):
            g = p // (N_PAIRS // SSM_GROUPS)
            if has_init:
                blk = h0_ref[0, d, p * PAIR:(p + 1) * PAIR, :]
                z64 = jnp.zeros_like(blk)
                hst_sc[p] = jnp.concatenate([blk, z64] if g == 0 else [z64, blk], axis=1)
            else:
                hst_sc[p] = jnp.zeros((PAIR, PAIR), F32)

        def inter(k, carry, d=d, sel=sel):
            c = k if d == 0 else nc - 1 - k
            r0 = pl.multiple_of(c * cl, cl)
            ecum = _dot_sel(jnp.exp(cum_sc[pl.ds(r0, cl), :]), sel)
            cm = cm_sc[pl.ds(r0, cl), :]
            tot = tot_sc[c]
            dec = jnp.exp(tot[0:1, :])
            for p in range(N_PAIRS):
                g = p // (N_PAIRS // SSM_GROUPS)
                cg = jnp.where(lane // HEAD == g, cm, 0.0)
                hs = hst_sc[p]
                yi = _dot_nt(cg, hs)
                cols = pl.ds(p * PAIR, PAIR)
                y_sc[pl.ds(r0, cl), cols] = y_sc[pl.ds(r0, cl), cols] + yi * ecum[:, p * PAIR:(p + 1) * PAIR]
                h0i = d * N_HEADS + 2 * p
                dcol = jnp.where(rr < HEAD, dec[:, h0i:h0i + 1], dec[:, h0i + 1:h0i + 2])
                hst_sc[p] = hs * dcol + dh_sc[c, d, p]
            return carry

        lax.fori_loop(0, nc, inter, 0)
        if emit_state:
            for p in range(N_PAIRS):
                g = p // (N_PAIRS // SSM_GROUPS)
                st_ref[0, d, p * PAIR:(p + 1) * PAIR, :] = hst_sc[p][:, g * HEAD:(g + 1) * HEAD]

    def finish(c, carry):
        r0 = pl.multiple_of(c * cl, cl)
        v = y_sc[pl.ds(r0, cl), :] * _silu(z_ref[0, pl.ds(r0, cl), :])
        o_ref[0, pl.ds(r0, cl), :] = _rms(v, gnorm_ref[...]).astype(BF16)
        return carry

    lax.fori_loop(0, nc, finish, 0)


def _ssd(proj, projT, prm, h0, *, emit_state):
    bsz, t, _ = proj.shape
    nc = t // SSD_CHUNK
    has_init = h0 is not None
    seq = lambda w, j: pl.BlockSpec((1, t, w), lambda b: (b, 0, j))
    in_specs = [seq(INNER, COL_Z), seq(INNER, COL_XS), seq(2 * PAIR, COL_BC), seq(PAIR, COL_DT),
                pl.BlockSpec((1, 2 * N_HEADS, t), lambda b: (b, ROWT_DT, 0))]
    args = [proj, proj, proj, proj, projT]
    for name in ("cwx", "cbx", "cwbc", "cbbc", "dtb_row", "dtb_col", "alog_row", "alog_col", "dskip", "gnorm"):
        a = prm[name]
        in_specs.append(_full(a.shape))
        args.append(a)
    if has_init:
        in_specs.append(pl.BlockSpec((1, 2, INNER, HEAD), lambda b: (b, 0, 0, 0)))
        args.append(h0)
    out_shape = [jax.ShapeDtypeStruct((bsz, t, INNER), BF16)]
    out_specs = [pl.BlockSpec((1, t, INNER), lambda b: (b, 0, 0))]
    if emit_state:
        out_shape.append(jax.ShapeDtypeStruct((bsz, 2, INNER, HEAD), F32))
        out_specs.append(pl.BlockSpec((1, 2, INNER, HEAD), lambda b: (b, 0, 0, 0)))
    res = pl.pallas_call(
        functools.partial(_ssd_body, t=t, has_init=has_init, emit_state=emit_state),
        out_shape=out_shape,
        grid=(bsz,),
        in_specs=in_specs,
        out_specs=out_specs,
        scratch_shapes=[pltpu.VMEM((t, INNER), F32), pltpu.VMEM((t, PAIR), F32), pltpu.VMEM((t, PAIR), F32),
                        pltpu.VMEM((nc, 2, N_PAIRS, PAIR, PAIR), F32), pltpu.VMEM((nc, HALO, PAIR), F32),
                        pltpu.VMEM((N_PAIRS, PAIR, PAIR), F32)],
        compiler_params=_params(),
        name="ssd",
    )(*args)
    return (res[0], res[1]) if emit_state else (res[0], None)


def _attn_body(*refs, t, s_ctx, rope):
    q_ref, kk_ref, vv_ref, qg_ref, kg_ref = refs[:5]
    pos = 5
    if rope:
        cos_ref, sin_ref, ck_ref, cv_ref = refs[pos:pos + 4]
        pos += 4
    o_ref = refs[pos]
    pos += 1
    if not rope:
        knew_ref, vnew_ref = refs[pos:pos + 2]
        pos += 2
    q_sc, k_sc, v_sc, s_sc, e_sc, l_sc = refs[pos:]

    rc = ROW_CHUNK
    lane = _iota((rc, PAIR), 1)
    first_half = (lane % (HEAD // 2)) < HEAD // 4

    def rot(x, cos, sin):
        partner = jnp.where(first_half, pltpu.roll(x, PAIR - HEAD // 4, 1), pltpu.roll(x, HEAD // 4, 1))
        return x * cos + partner * sin

    def normed(x, gain):
        return x * lax.rsqrt(_head_sumsq(x) * (1.0 / HEAD) + EPS) * gain

    if rope:
        k_sc[0:s_ctx, :] = ck_ref[0].astype(BF16)
        v_sc[0:s_ctx, :] = cv_ref[0].astype(BF16)

    def prep(c, carry):
        r0 = pl.multiple_of(c * rc, rc)
        if rope:
            cos = cos_ref[pl.ds(r0, rc), :]
            sin = sin_ref[pl.ds(r0, rc), :]
        for p in range(N_PAIRS):
            x = normed(q_ref[0, pl.ds(r0, rc), p * PAIR:(p + 1) * PAIR], qg_ref[...])
            if rope:
                x = rot(x, cos, sin)
            q_sc[pl.ds(r0, rc), p * PAIR:(p + 1) * PAIR] = (x * (HEAD ** -0.5)).astype(BF16)
        kn = []
        for g in range(KV_HEADS):
            x = normed(kk_ref[0, pl.ds(r0, rc), g * PAIR:(g + 1) * PAIR], kg_ref[...])
            kn.append(x)
            if rope:
                x = rot(x, cos, sin)
            k_sc[pl.ds(s_ctx + r0, rc), g * PAIR:(g + 1) * PAIR] = x.astype(BF16)
        vv = vv_ref[0, pl.ds(r0, rc), :]
        v_sc[pl.ds(s_ctx + r0, rc), :] = vv.astype(BF16)
        if not rope:
            knew_ref[0, pl.ds(r0, rc), :] = jnp.where(lane < HEAD, kn[0], kn[1])
            vnew_ref[0, pl.ds(r0, rc), :] = jnp.where(lane < HEAD, vv[:, :PAIR], vv[:, PAIR:])
        return carry

    lax.fori_loop(0, t // rc, prep, 0)

    tq = ATT_TQ
    lane_q = _iota((tq, PAIR), 1)

    pairs_per_group = N_PAIRS // KV_HEADS

    n_tiles = t // tq

    def scores(i, g, dst):
        r0 = i * tq if isinstance(i, int) else pl.multiple_of(i * tq, tq)
        tiles = []
        for p in range(g * pairs_per_group, (g + 1) * pairs_per_group):
            qp = q_sc[pl.ds(r0, tq), p * PAIR:(p + 1) * PAIR]
            for a in range(2):
                tiles.append(jnp.where(lane_q // HEAD == a, qp, jnp.zeros_like(qp)))
        qs = jnp.concatenate(tiles, axis=0)
        dst[...] = lax.dot_general(qs, k_sc[:, g * PAIR:(g + 1) * PAIR], (((1,), (1,)), ((), ())),
                                   preferred_element_type=F32)

    def softmax(src, e_dst, l_dst):
        rows_u, s_all = src.shape
        rb = ATT_SM_ROWS
        for r in range(rows_u // rb):
            rows = slice(r * rb, (r + 1) * rb)
            m = src[rows, 0:PAIR]
            for c in range(1, s_all // PAIR):
                m = jnp.maximum(m, src[rows, c * PAIR:(c + 1) * PAIR])
            mb = jnp.broadcast_to(jnp.max(m, axis=-1, keepdims=True), (rb, PAIR))
            acc = None
            for c in range(s_all // PAIR):
                e = jnp.exp(src[rows, c * PAIR:(c + 1) * PAIR] - mb)
                acc = e if acc is None else acc + e
                e_dst[rows, c * PAIR:(c + 1) * PAIR] = e.astype(BF16)
            l_dst[rows, :] = jnp.broadcast_to(jnp.sum(acc, axis=-1, keepdims=True), (rb, PAIR))

    def values(i, g, e_src, l_src):
        r0 = i * tq if isinstance(i, int) else pl.multiple_of(i * tq, tq)
        o = jnp.dot(e_src[...], v_sc[:, g * PAIR:(g + 1) * PAIR], preferred_element_type=F32) / l_src[...]
        for j in range(pairs_per_group):
            p = g * pairs_per_group + j
            pair = jnp.where(lane_q < HEAD, o[(2 * j) * tq:(2 * j + 1) * tq], o[(2 * j + 1) * tq:(2 * j + 2) * tq])
            o_ref[0, pl.ds(r0, tq), p * PAIR:(p + 1) * PAIR] = pair.astype(BF16)

    def step(i, slot):
        for g in range(KV_HEADS):
            scores(jnp.minimum(i + 1, n_tiles - 1), g, s_sc.at[1 - slot, g])
            softmax(s_sc.at[slot, g], e_sc.at[slot, g], l_sc.at[slot, g])
            values(jnp.maximum(i - 1, 0), g, e_sc.at[1 - slot, g], l_sc.at[1 - slot, g])

    for g in range(KV_HEADS):
        scores(0, g, s_sc.at[0, g])
        e_sc[1, g] = jnp.zeros(e_sc.shape[2:], BF16)
        l_sc[1, g] = jnp.ones(l_sc.shape[2:], F32)

    def attend(j, carry):
        step(2 * j, 0)
        step(2 * j + 1, 1)
        return carry

    assert n_tiles % 2 == 0
    lax.fori_loop(0, n_tiles // 2, attend, 0)
    for g in range(KV_HEADS):
        values(n_tiles - 1, g, e_sc.at[1, g], l_sc.at[1, g])


def _attn(proj, prm, rope_tabs, ctx_kv):
    bsz, t, _ = proj.shape
    rope = rope_tabs is not None
    s_ctx = ctx_kv[0].shape[1] if rope else 0
    s_all = s_ctx + t
    rows_u = (N_HEADS // KV_HEADS) * ATT_TQ
    seq = lambda w, j: pl.BlockSpec((1, t, w), lambda b: (b, 0, j), pipeline_mode=pl.Buffered(1))
    in_specs = [seq(INNER, COL_QB), seq(2 * PAIR, COL_KK), seq(2 * PAIR, COL_VV), _full((1, PAIR)), _full((1, PAIR))]
    args = [proj, proj, proj, prm["qgain"], prm["kgain"]]
    if rope:
        in_specs += [_full((t, PAIR)), _full((t, PAIR)),
                     pl.BlockSpec((1, s_ctx, 2 * PAIR), lambda b: (b, 0, 0)),
                     pl.BlockSpec((1, s_ctx, 2 * PAIR), lambda b: (b, 0, 0))]
        args += [rope_tabs[0], rope_tabs[1], ctx_kv[0], ctx_kv[1]]
    out_shape = [jax.ShapeDtypeStruct((bsz, t, INNER), BF16)]
    out_specs = [pl.BlockSpec((1, t, INNER), lambda b: (b, 0, 0))]
    if not rope:
        out_shape += [jax.ShapeDtypeStruct((bsz, t, PAIR), F32)] * 2
        out_specs += [pl.BlockSpec((1, t, PAIR), lambda b: (b, 0, 0))] * 2
    res = pl.pallas_call(
        functools.partial(_attn_body, t=t, s_ctx=s_ctx, rope=rope),
        out_shape=out_shape,
        grid=(bsz,),
        in_specs=in_specs,
        out_specs=out_specs,
        scratch_shapes=[pltpu.VMEM((t, INNER), BF16), pltpu.VMEM((s_all, 2 * PAIR), BF16),
                        pltpu.VMEM((s_all, 2 * PAIR), BF16),
                        pltpu.VMEM((2, KV_HEADS, rows_u, s_all), F32), pltpu.VMEM((2, KV_HEADS, rows_u, s_all), BF16),
                        pltpu.VMEM((2, KV_HEADS, rows_u, PAIR), F32)],
        compiler_params=_params(),
        name="attn",
    )(*args)
    return res


def _dn_body(*refs, t, has_init, emit_state):
    (q_ref, k_ref, v_ref, gate_ref, ba_ref, baT_ref, cwq_ref, cwk_ref, cwv_ref, cbq_ref, cbk_ref, cbv_ref,
     alog_row_ref, bias_row_ref, alog_col_ref, bias_col_ref, gnorm_ref) = refs[:17]
    pos = 17
    s0_ref = None
    if has_init:
        s0_ref = refs[pos]
        pos += 1
    o_ref = refs[pos]
    pos += 1
    st_ref = None
    if emit_state:
        st_ref = refs[pos]
        pos += 1
    qn_sc, kn_sc, vc_sc, bg_sc, bgT_sc, of_sc, ob_sc, s_sc = refs[pos:]

    rc = ROW_CHUNK
    cl = DN_CHUNK
    nc = t // cl
    nrc = t // rc
    hp = lax.Precision.HIGHEST
    nb = 2 * N_HEADS

    lane_rc = _iota((rc, PAIR), 1)
    row_t = _iota((2 * nb, rc), 0)

    def prep(c, carry):
        r0 = pl.multiple_of(c * rc, rc)
        for src, cw, cb, dst, norm in ((q_ref, cwq_ref, cbq_ref, qn_sc, True), (k_ref, cwk_ref, cbk_ref, kn_sc, True),
                                       (v_ref, cwv_ref, cbv_ref, vc_sc, False)):
            x = _silu(_dwconv(_conv_window(src, r0, c, nrc, rc, t), cw[...], cb[...], rc))
            if norm:
                tiles = []
                for p in range(N_PAIRS):
                    xp = x[:, p * PAIR:(p + 1) * PAIR]
                    tiles.append(xp * lax.rsqrt(_head_sumsq(xp) + EPS))
                x = jnp.concatenate(tiles, axis=1)
                if dst is qn_sc:
                    x = x * (HEAD ** -0.5)
            dst[pl.ds(r0, rc), :] = x
        ba = ba_ref[0, pl.ds(r0, rc), :]
        beta = jax.nn.sigmoid(ba)
        gdec = -jnp.exp(alog_row_ref[...]) * _softplus(ba + bias_row_ref[...])
        bg_sc[pl.ds(r0, rc), :] = jnp.where(lane_rc < nb, beta, gdec)
        baT = baT_ref[0, :, pl.ds(r0, rc)]
        betaT = jax.nn.sigmoid(baT)
        gdecT = -jnp.exp(alog_col_ref[...]) * _softplus(baT + bias_col_ref[...])
        bgT = jnp.where(row_t < nb, betaT, gdecT)
        for k in range(rc // cl):
            bgT_sc[c * (rc // cl) + k] = bgT[:, k * cl:(k + 1) * cl]
        return carry

    lax.fori_loop(0, nrc, prep, 0)

    rr = _iota((PAIR, PAIR), 0)
    cc = _iota((PAIR, PAIR), 1)
    same = (rr // HEAD) == (cc // HEAD)
    lane_c = _iota((cl, PAIR), 1)
    ii = _iota((cl, PAIR), 0)
    jj = lane_c % HEAD
    first = lane_c < HEAD
    lo = _lo_tri(cl)
    up = _up_tri(cl)
    eye = (ii == jj).astype(F32)
    offdiag = (ii != jj).astype(F32)
    blk_base = ii // DN_BASE == jj // DN_BASE
    off_masks = []
    b = DN_BASE
    while b < cl:
        off_masks.append((ii // (2 * b) == jj // (2 * b)) & (ii // b != jj // b))
        b *= 2

    for d in range(2):
        for p in range(N_PAIRS):
            if has_init:
                b0 = s0_ref[0, d, (2 * p) * HEAD:(2 * p + 1) * HEAD, :]
                b1 = s0_ref[0, d, (2 * p + 1) * HEAD:(2 * p + 2) * HEAD, :]
                z64 = jnp.zeros_like(b0)
                s_sc[d, p] = jnp.concatenate([jnp.concatenate([b0, z64], axis=1),
                                              jnp.concatenate([z64, b1], axis=1)], axis=0)
            else:
                s_sc[d, p] = jnp.zeros((PAIR, PAIR), F32)

    def bdiag(x):
        xb = x.astype(BF16)
        zero = jnp.zeros_like(xb)
        return jnp.concatenate([jnp.where(first, xb, zero), jnp.where(first, zero, xb)], axis=0)

    def pdot(x, y_bd):
        return jnp.dot(x.astype(BF16), y_bd, preferred_element_type=F32)

    def cols2(m, c0):
        return jnp.where(first, m[:, c0:c0 + 1], m[:, c0 + 1:c0 + 2])

    group = min(DN_GROUP, nc)
    systems =[(d, j, p) for j in range(group) for d in range(2) for p in range(N_PAIRS)]

    def chunk(k, carry):
        dirs = {}
        for d in range(2):
            for j in range(group):
                c = k * group + j if d == 0 else nc - 1 - (k * group + j)
                r0 = pl.multiple_of(c * cl, cl)
                bg = bg_sc[pl.ds(r0, cl), :]
                bgT = bgT_sc[c]
                tri = lo if d == 0 else up
                triT = up if d == 0 else lo
                gc = _sel_dot(tri, bg)
                gcT = _dot_sel(bgT, triT)
                dirs[d, j] = (r0, bg, gc, gcT)
        st = []
        for d, j, p in systems:
            r0, bg, gc, gcT = dirs[d, j]
            last = cl - 1 if d == 0 else 0
            incl = (jj <= ii) if d == 0 else (jj >= ii)
            h0i = d * N_HEADS + 2 * p
            beta = cols2(bg, h0i)
            gcm = cols2(gc, nb + h0i)
            rgc = jnp.concatenate([gcT[nb + h0i:nb + h0i + 1, :], gcT[nb + h0i + 1:nb + h0i + 2, :]], axis=1)
            glast = jnp.where(first[0:1], gc[last:last + 1, nb + h0i:nb + h0i + 1],
                              gc[last:last + 1, nb + h0i + 1:nb + h0i + 2])
            decay = jnp.exp(jnp.where(incl, gcm - rgc, -jnp.inf))
            cols = pl.ds(p * PAIR, PAIR)
            kp = kn_sc[pl.ds(r0, cl), cols]
            qp = qn_sc[pl.ds(r0, cl), cols]
            vp = vc_sc[pl.ds(r0, cl), cols]
            egc = jnp.exp(gcm)
            kb = kp * beta
            st.append(dict(r0=r0, cols=cols, glast=glast, decay=decay, kp=kp, kb=kb, qp=qp, qg=qp * egc,
                           kd=kp * jnp.exp(glast - gcm), rhs_u=vp * beta, rhs_w=kb * egc))
        for e in st:
            k_bd = bdiag(e["kp"])
            e["gm"] = lax.dot_general(e["kb"].astype(BF16), k_bd, (((1,), (1,)), ((), ())), preferred_element_type=F32)
            e["am"] = lax.dot_general(e["qp"].astype(BF16), k_bd, (((1,), (1,)), ((), ())), preferred_element_type=F32)
        for e in st:
            e["m"] = e["gm"] * e["decay"] * offdiag
            e["aq"] = e["am"] * e["decay"]
            e["n1"] = jnp.where(blk_base, e["m"], 0.0)
        for e in st:
            e["n2"] = pdot(e["n1"], bdiag(e["n1"]))
        for e in st:
            e["n2_bd"] = bdiag(e["n2"])
            e["n4"] = pdot(e["n2"], e["n2_bd"])
        for e in st:
            e["pm"] = eye + e["n2"] + e["n4"] + pdot(e["n4"], e["n2_bd"])
        for e in st:
            e["x"] = e["pm"] - pdot(e["n1"], bdiag(e["pm"]))
        for off_mask in off_masks:
            for e in st:
                e["cx"] = pdot(jnp.where(off_mask, e["m"], 0.0), bdiag(e["x"]))
            for e in st:
                e["x"] = e["x"] - pdot(e["x"], bdiag(e["cx"]))
        for e in st:
            rhs_bd = jnp.concatenate([bdiag(e["rhs_u"]), bdiag(e["rhs_w"])], axis=1)
            e["sol"] = pdot(e["x"], rhs_bd)
        state = {(d, p): s_sc[d, p] for d in range(2) for p in range(N_PAIRS)}
        for j in range(group):
            cur = [((d, p), e) for (d, jj_, p), e in zip(systems, st) if jj_ == j]
            for key, e in cur:
                e["s_bd"] = state[key].astype(BF16)
                e["vnew"] = e["sol"][:, :PAIR] - pdot(e["sol"][:, PAIR:], e["s_bd"])
            for (d, p), e in cur:
                dst = of_sc if d == 0 else ob_sc
                dst[pl.ds(e["r0"], cl), e["cols"]] = pdot(e["qg"], e["s_bd"]) + pdot(e["aq"], bdiag(e["vnew"]))
                upd = jnp.where(same, _dot_tn(e["kd"], e["vnew"]), 0.0)
                gl_rows = jnp.where(rr[:, 0:1] < HEAD, e["glast"][:, 0:1], e["glast"][:, HEAD:HEAD + 1])
                state[d, p] = state[d, p] * jnp.exp(gl_rows) + upd
        for (d, p), s in state.items():
            s_sc[d, p] = s
        return carry

    lax.fori_loop(0, nc // group, chunk, 0)

    if emit_state:
        for d in range(2):
            for p in range(N_PAIRS):
                s = s_sc[d, p]
                st_ref[0, d, (2 * p) * HEAD:(2 * p + 1) * HEAD, :] = s[:HEAD, :HEAD]
                st_ref[0, d, (2 * p + 1) * HEAD:(2 * p + 2) * HEAD, :] = s[HEAD:, HEAD:]

    def finish(c, carry):
        r0 = pl.multiple_of(c * rc, rc)
        for p in range(N_PAIRS):
            cols = pl.ds(p * PAIR, PAIR)
            o = of_sc[pl.ds(r0, rc), cols] + ob_sc[pl.ds(r0, rc), cols]
            on = o * lax.rsqrt(_head_sumsq(o) * (1.0 / HEAD) + EPS) * gnorm_ref[...]
            o_ref[0, pl.ds(r0, rc), cols] = (on * _silu(gate_ref[0, pl.ds(r0, rc), cols])).astype(BF16)
        return carry

    lax.fori_loop(0, nrc, finish, 0)


def _dn(proj, projT, prm, s0, *, emit_state):
    bsz, t, _ = proj.shape
    nc = t // DN_CHUNK
    has_init = s0 is not None
    seq = lambda w, j: pl.BlockSpec((1, t, w), lambda b: (b, 0, j), pipeline_mode=pl.Buffered(1))
    in_specs = [seq(INNER, COL_QC), seq(INNER, COL_KC), seq(INNER, COL_VC), seq(INNER, COL_GC), seq(PAIR, COL_BA),
                pl.BlockSpec((1, 4 * N_HEADS, t), lambda b: (b, ROWT_BA, 0), pipeline_mode=pl.Buffered(1))]
    args = [proj] * 5 + [projT]
    for name in ("cwq", "cwk", "cwv", "cbq", "cbk", "cbv", "alog_row", "bias_row", "alog_col", "bias_col", "gnorm"):
        a = prm[name]
        in_specs.append(_full(a.shape))
        args.append(a)
    if has_init:
        in_specs.append(pl.BlockSpec((1, 2, INNER, HEAD), lambda b: (b, 0, 0, 0)))
        args.append(s0)
    out_shape = [jax.ShapeDtypeStruct((bsz, t, INNER), BF16)]
    out_specs = [pl.BlockSpec((1, t, INNER), lambda b: (b, 0, 0))]
    if emit_state:
        out_shape.append(jax.ShapeDtypeStruct((bsz, 2, INNER, HEAD), F32))
        out_specs.append(pl.BlockSpec((1, 2, INNER, HEAD), lambda b: (b, 0, 0, 0)))
    res = pl.pallas_call(
        functools.partial(_dn_body, t=t, has_init=has_init, emit_state=emit_state),
        out_shape=out_shape,
        grid=(bsz,),
        in_specs=in_specs,
        out_specs=out_specs,
        scratch_shapes=[pltpu.VMEM((t, INNER), F32), pltpu.VMEM((t, INNER), F32), pltpu.VMEM((t, INNER), F32),
                        pltpu.VMEM((t, PAIR), F32), pltpu.VMEM((nc, 4 * N_HEADS, DN_CHUNK), F32),
                        pltpu.VMEM((t, INNER), F32), pltpu.VMEM((t, INNER), F32),
                        pltpu.VMEM((2, N_PAIRS, PAIR, PAIR), F32)],
        compiler_params=_params(),
        name="deltanet",
    )(*args)
    return (res[0], res[1]) if emit_state else (res[0], None)


def _lru_body(*refs, t, has_init, emit_state):
    x_ref, y_ref, cw_ref, cb_ref, wbd_ref, bias_ref, lam_ref = refs[:7]
    pos = 7
    h0_ref = None
    if has_init:
        h0_ref = refs[pos]
        pos += 1
    o_ref = refs[pos]
    pos += 1
    st_ref = None
    if emit_state:
        st_ref = refs[pos]
        pos += 1
    af_sc, uf_sc, ab_sc, ub_sc = refs[pos:]

    rc = ROW_CHUNK
    nrc = t // rc
    row8 = _iota((HALO, INNER), 0)
    sp_lam = _softplus(-lam_ref[...])

    def gates(c, carry):
        r0 = pl.multiple_of(c * rc, rc)
        xl_all = _dwconv(_conv_window(x_ref, r0, c, nrc, rc, t), cw_ref[...], cb_ref[...], rc)
        for p in range(N_PAIRS):
            cols = pl.ds(p * PAIR, PAIR)
            xl = xl_all[:, p * PAIR:(p + 1) * PAIR]
            pre = _dot(xl, wbd_ref[p]) + bias_ref[p]
            for d, (a_sc, u_sc) in enumerate(((af_sc, uf_sc), (ab_sc, ub_sc))):
                r = jax.nn.sigmoid(pre[:, d * PAIR:(d + 1) * PAIR])
                ig = jax.nn.sigmoid(pre[:, (2 + d) * PAIR:(3 + d) * PAIR])
                log_a = -LRU_C * r * sp_lam[d:d + 1, p * PAIR:(p + 1) * PAIR]
                a = jnp.exp(log_a)
                a_sc[pl.ds(r0, rc), cols] = a
                u_sc[pl.ds(r0, rc), cols] = jnp.sqrt((1.0 - a) * (1.0 + a)) * ig * xl
        return carry

    lax.fori_loop(0, nrc, gates, 0)

    if has_init:
        cf0 = h0_ref[0, 0:1, :]
        cb0 = h0_ref[0, 1:2, :]
    else:
        cf0 = jnp.zeros((1, INNER), F32)
        cb0 = cf0

    def scan(k, carry):
        cf, cb = carry
        rf = pl.multiple_of(k * HALO, HALO)
        a8 = af_sc[pl.ds(rf, HALO), :]
        b8 = uf_sc[pl.ds(rf, HALO), :]
        for s in (1, 2, 4):
            ok = row8 >= s
            b8 = jnp.where(ok, a8 * pltpu.roll(b8, s, 0) + b8, b8)
            a8 = jnp.where(ok, a8 * pltpu.roll(a8, s, 0), a8)
        hf = a8 * cf + b8
        uf_sc[pl.ds(rf, HALO), :] = hf
        rb = pl.multiple_of(t - HALO - k * HALO, HALO)
        a8 = ab_sc[pl.ds(rb, HALO), :]
        b8 = ub_sc[pl.ds(rb, HALO), :]
        for s in (1, 2, 4):
            ok = row8 < HALO - s
            b8 = jnp.where(ok, a8 * pltpu.roll(b8, HALO - s, 0) + b8, b8)
            a8 = jnp.where(ok, a8 * pltpu.roll(a8, HALO - s, 0), a8)
        hb = a8 * cb + b8
        ub_sc[pl.ds(rb, HALO), :] = hb
        return hf[HALO - 1:HALO, :], hb[0:1, :]

    cf, cb = lax.fori_loop(0, t // HALO, scan, (cf0, cb0))
    if emit_state:
        st_ref[0, 0:1, :] = cf
        st_ref[0, 1:2, :] = cb

    def finish(c, carry):
        r0 = pl.multiple_of(c * rc, rc)
        y = y_ref[0, pl.ds(r0, rc), :]
        gelu = 0.5 * y * (1.0 + jnp.tanh(math.sqrt(2.0 / math.pi) * (y + 0.044715 * (y * y * y))))
        o_ref[0, pl.ds(r0, rc), :] = ((uf_sc[pl.ds(r0, rc), :] + ub_sc[pl.ds(r0, rc), :]) * gelu).astype(BF16)
        return carry

    lax.fori_loop(0, nrc, finish, 0)


def _lru(proj, prm, h0, *, emit_state):
    bsz, t, _ = proj.shape
    has_init = h0 is not None
    seq = lambda w, j: pl.BlockSpec((1, t, w), lambda b: (b, 0, j))
    in_specs = [seq(INNER, COL_XD), seq(INNER, COL_YD)]
    args = [proj, proj]
    for name in ("cw", "cb", "wbd", "bias", "lam"):
        a = prm[name]
        in_specs.append(_full(a.shape))
        args.append(a)
    if has_init:
        in_specs.append(pl.BlockSpec((1, 2, INNER), lambda b: (b, 0, 0)))
        args.append(h0)
    out_shape = [jax.ShapeDtypeStruct((bsz, t, INNER), BF16)]
    out_specs = [pl.BlockSpec((1, t, INNER), lambda b: (b, 0, 0))]
    if emit_state:
        out_shape.append(jax.ShapeDtypeStruct((bsz, 2, INNER), F32))
        out_specs.append(pl.BlockSpec((1, 2, INNER), lambda b: (b, 0, 0)))
    res = pl.pallas_call(
        functools.partial(_lru_body, t=t, has_init=has_init, emit_state=emit_state),
        out_shape=out_shape,
        grid=(bsz,),
        in_specs=in_specs,
        out_specs=out_specs,
        scratch_shapes=[pltpu.VMEM((t, INNER), F32)] * 4,
        compiler_params=_params(),
        name="rglru",
    )(*args)
    return (res[0], res[1]) if emit_state else (res[0], None)


def _pad_cols(x, n):
    return jnp.pad(x, ((0, 0), (0, n - x.shape[1])))


def _layer_params(l, w_in, ssm_conv_w, ssm_conv_b, ssm_a_log, ssm_dt_bias, ssm_d, ssm_norm, attn_q_norm, attn_k_norm,
                  dn_conv_w, dn_conv_b, dn_a_log, dn_dt_bias, dn_norm, lru_conv_w, lru_conv_b, lru_w_a, lru_b_a,
                  lru_w_i, lru_b_i, lru_lambda):
    w = w_in[l]
    o = 0

    def take(n):
        nonlocal o
        part = w[:, o:o + n]
        o += n
        return part

    z_a, xs_a, bc_a, dt_a = take(INNER), take(INNER), take(2 * PAIR), take(2 * N_HEADS)
    q_b, k_b, v_b = take(INNER), take(PAIR), take(PAIR)
    q_c, k_c, v_c = take(INNER), take(INNER), take(INNER)
    beta_c, a_c, gate_c = take(2 * N_HEADS), take(2 * N_HEADS), take(INNER)
    x_d, y_d = take(INNER), take(INNER)
    gate_raw = take(N_BRANCH * D_MODEL)

    def dup(x):
        return jnp.concatenate([x[:, :HEAD], x[:, :HEAD], x[:, HEAD:], x[:, HEAD:]], axis=1)

    nh2 = 2 * N_HEADS
    ba_c = jnp.concatenate([beta_c, a_c], axis=1)
    w_proj = jnp.concatenate([z_a, xs_a, q_b, q_c, k_c, v_c, gate_c, x_d, y_d, bc_a, dup(k_b), dup(v_b),
                              _pad_cols(dt_a, PAIR), _pad_cols(ba_c, PAIR)], axis=1)
    assert w_proj.shape[1] == PROJ_N
    prm = {
        "w_proj": w_proj.astype(BF16),
        "w_projT": jnp.concatenate([ba_c, dt_a], axis=1).T.astype(BF16),
        "w_gate": gate_raw.astype(BF16),
    }
    cw, cb = ssm_conv_w[l], ssm_conv_b[l][None, :]
    dtb = ssm_dt_bias[l].reshape(1, nh2)
    alog = ssm_a_log[l].reshape(1, nh2)
    prm["ssd"] = {
        "cwx": cw[:, :INNER], "cbx": cb[:, :INNER], "cwbc": cw[:, INNER:], "cbbc": cb[:, INNER:],
        "dtb_row": _pad_cols(dtb, PAIR), "dtb_col": jnp.broadcast_to(dtb.T, (nh2, SSD_CHUNK)),
        "alog_row": _pad_cols(alog, PAIR), "alog_col": jnp.broadcast_to(alog.T, (nh2, SSD_CHUNK)),
        "dskip": jnp.repeat(ssm_d[l], HEAD)[None, :], "gnorm": ssm_norm[l][None, :],
    }
    prm["att"] = {"qgain": jnp.tile(attn_q_norm[l], 2)[None, :], "kgain": jnp.tile(attn_k_norm[l], 2)[None, :]}
    dcw, dcb = dn_conv_w[l], dn_conv_b[l][None, :]
    zeros16 = jnp.zeros((1, nh2), F32)
    d_alog = jnp.concatenate([zeros16, dn_a_log[l].reshape(1, nh2)], axis=1)
    d_bias = jnp.concatenate([zeros16, dn_dt_bias[l].reshape(1, nh2)], axis=1)
    prm["dn"] = {
        "cwq": dcw[:, :INNER], "cwk": dcw[:, INNER:2 * INNER], "cwv": dcw[:, 2 * INNER:],
        "cbq": dcb[:, :INNER], "cbk": dcb[:, INNER:2 * INNER], "cbv": dcb[:, 2 * INNER:],
        "alog_row": _pad_cols(d_alog, PAIR), "bias_row": _pad_cols(d_bias, PAIR),
        "alog_col": jnp.broadcast_to(d_alog.T, (2 * nh2, ROW_CHUNK)),
        "bias_col": jnp.broadcast_to(d_bias.T, (2 * nh2, ROW_CHUNK)),
        "gnorm": jnp.tile(dn_norm[l], 2)[None, :],
    }
    wa, wi = lru_w_a[l], lru_w_i[l]
    z64 = jnp.zeros((HEAD, HEAD), F32)
    wbd, bias = [], []
    for p in range(N_PAIRS):
        blocks = []
        for wsrc in (wa, wi):
            for d in range(2):
                top = jnp.concatenate([wsrc[d, 2 * p], z64], axis=1)
                bot = jnp.concatenate([z64, wsrc[d, 2 * p + 1]], axis=1)
                blocks.append(jnp.concatenate([top, bot], axis=0))
        wbd.append(jnp.concatenate(blocks, axis=1))
        bias.append(jnp.concatenate([lru_b_a[l][0, p * PAIR:(p + 1) * PAIR], lru_b_a[l][1, p * PAIR:(p + 1) * PAIR],
                                     lru_b_i[l][0, p * PAIR:(p + 1) * PAIR], lru_b_i[l][1, p * PAIR:(p + 1) * PAIR]])[None, :])
    prm["lru"] = {"cw": lru_conv_w[l], "cb": lru_conv_b[l][None, :], "wbd": jnp.stack(wbd).astype(BF16),
                  "bias": jnp.stack(bias), "lam": lru_lambda[l]}
    return prm


def _rope_tables(t):
    n_freq = HEAD // 4
    inv = ROPE_THETA ** (-jnp.arange(n_freq, dtype=F32) / n_freq)
    rows = t // GRID_W
    row = jnp.repeat(jnp.arange(rows, dtype=F32), GRID_W)
    col = jnp.tile(jnp.arange(GRID_W, dtype=F32), rows)
    ang_r = row[:, None] * inv
    ang_c = col[:, None] * inv
    ang = jnp.concatenate([ang_r, ang_r, ang_c, ang_c], axis=1)
    sign = jnp.tile(jnp.concatenate([-jnp.ones((n_freq,), F32), jnp.ones((n_freq,), F32)]), 2)
    cos = jnp.cos(ang)
    sin = jnp.sin(ang) * sign
    return jnp.tile(cos, (1, 2)), jnp.tile(sin, (1, 2))


def _dup_kv(x):
    return jnp.concatenate([x[:, :, 0], x[:, :, 0], x[:, :, 1], x[:, :, 1]], axis=-1)


def _trunk_layer(h, mod, lw, prm, rope_tabs, ctx, fin_gain, *, final):
    h = _ffn(h, mod, lw["norm_ffn1"], lw["ffn1_w13"], lw["ffn1_w2"], fin_gain, rows=(0, 1, 2), final=False)
    gain = lw["norm_mix"]
    emit = ctx is None
    proj, projT = _proj(h, mod, gain, prm["w_proj"], prm["w_projT"])
    if emit:
        o_a, st_ssm = _ssd(proj, projT, prm["ssd"], None, emit_state=True)
        o_b, k_new, v_new = _attn(proj, prm["att"], None, None)
        o_c, st_dn = _dn(proj, projT, prm["dn"], None, emit_state=True)
        o_d, st_lru = _lru(proj, prm["lru"], None, emit_state=True)
        new_ctx = (k_new, v_new, st_ssm, st_dn, st_lru)
    else:
        ck, cv, ssm0, dn0, lru0 = ctx
        o_a, _ = _ssd(proj, projT, prm["ssd"], ssm0, emit_state=False)
        (o_b,) = _attn(proj, prm["att"], rope_tabs, (ck, cv))
        o_c, _ = _dn(proj, projT, prm["dn"], dn0, emit_state=False)
        o_d, _ = _lru(proj, prm["lru"], lru0, emit_state=False)
        new_ctx = None
    h = _merge(h, mod, gain, (o_a, o_b, o_c, o_d), prm["w_gate"], lw["w_branch"], lw["w_out"])
    h = _ffn(h, mod, lw["norm_ffn2"], lw["ffn2_w13"], lw["ffn2_w2"], fin_gain, rows=(6, 7, 8), final=final)
    return h, new_ctx


def kernel(x_prompt, x_sample, cache_k, cache_v, state_ssm, state_delta, state_lru, c, c_ctx,
           w_ada, b_ada, norm_ffn1, ffn1_w13, ffn1_w2, norm_mix, w_in,
           ssm_conv_w, ssm_conv_b, ssm_a_log, ssm_dt_bias, ssm_d, ssm_norm,
           attn_q_norm, attn_k_norm,
           dn_conv_w, dn_conv_b, dn_a_log, dn_dt_bias, dn_norm,
           lru_conv_w, lru_conv_b, lru_w_a, lru_b_a, lru_w_i, lru_b_i, lru_lambda,
           w_branch, w_out, norm_ffn2, ffn2_w13, ffn2_w2, final_norm):
    depth = w_in.shape[0]
    bsz_p, t_p, _ = x_prompt.shape
    bsz_s, t_s, _ = x_sample.shape
    assert bsz_s + 1 <= MOD_ROWS
    cvec = jnp.concatenate([c, c_ctx[None], jnp.zeros((MOD_ROWS - bsz_s - 1, D_MODEL), F32)], axis=0)
    mod_all = _adaln(cvec, w_ada, b_ada)
    mod_all = mod_all.reshape(depth, MOD_ROWS, N_MOD, D_MODEL)
    mod_all = jnp.pad(mod_all, ((0, 0), (0, 0), (0, MOD_ROWS - N_MOD), (0, 0)))
    rope_tabs = _rope_tables(t_s)
    fin_gain = final_norm[None, :]

    hp, hs = x_prompt, x_sample
    ks, vs, ssm_s, dn_s, lru_s = [], [], [], [], []
    for l in range(depth):
        prm = _layer_params(l, w_in, ssm_conv_w, ssm_conv_b, ssm_a_log, ssm_dt_bias, ssm_d, ssm_norm,
                            attn_q_norm, attn_k_norm, dn_conv_w, dn_conv_b, dn_a_log, dn_dt_bias, dn_norm,
                            lru_conv_w, lru_conv_b, lru_w_a, lru_b_a, lru_w_i, lru_b_i, lru_lambda)
        lw = {
            "norm_ffn1": norm_ffn1[l][None, :], "ffn1_w13": ffn1_w13[l].astype(BF16), "ffn1_w2": ffn1_w2[l].astype(BF16),
            "norm_mix": norm_mix[l][None, :], "w_branch": w_branch[l].astype(BF16), "w_out": w_out[l].astype(BF16),
            "norm_ffn2": norm_ffn2[l][None, :], "ffn2_w13": ffn2_w13[l].astype(BF16), "ffn2_w2": ffn2_w2[l].astype(BF16),
        }
        final = l == depth - 1
        mod_lat = mod_all[l, :bsz_s]
        mod_ctx = mod_all[l, bsz_s:bsz_s + 1]
        hp, st = _trunk_layer(hp, mod_ctx, lw, prm, None, None, fin_gain, final=final)
        k_c, v_c, st_ssm, st_dn, st_lru = st
        ks.append(k_c.reshape(bsz_p, t_p, KV_HEADS, HEAD))
        vs.append(v_c.reshape(bsz_p, t_p, KV_HEADS, HEAD))
        ssm_s.append(st_ssm.reshape(bsz_p, 2, N_HEADS, HEAD, HEAD))
        dn_s.append(st_dn.reshape(bsz_p, 2, N_HEADS, HEAD, HEAD))
        lru_s.append(st_lru)
        ctx_l = (_dup_kv(cache_k[:, l]), _dup_kv(cache_v[:, l]),
                 state_ssm[:, l].reshape(bsz_s, 2, INNER, HEAD), state_delta[:, l].reshape(bsz_s, 2, INNER, HEAD),
                 state_lru[:, l])
        hs, _ = _trunk_layer(hs, mod_lat, lw, prm, rope_tabs, ctx_l, fin_gain, final=final)
    return (hp, hs, jnp.stack(ks, axis=1), jnp.stack(vs, axis=1), jnp.stack(ssm_s, axis=1),
            jnp.stack(dn_s, axis=1), jnp.stack(lru_s, axis=1))
```

```python
import functools
import math

import jax
import jax.numpy as jnp
from jax import lax
from jax.experimental import pallas as pl
from jax.experimental.pallas import tpu as pltpu

F32 = jnp.float32
BF16 = jnp.bfloat16

D_MODEL = 1024
D_FF = 2816
N_MOD = 9
MOD_ROWS = 16
EPS = 1e-6
GRID_W = 64
CONV_W = 4
CONV_LP = CONV_W // 2
HALO = 8
HEAD = 64
PAIR = 2 * HEAD
N_HEADS = 8
N_PAIRS = N_HEADS // 2
INNER = N_HEADS * HEAD
KV_HEADS = 2
SSM_GROUPS = 2
ROPE_THETA = 10000.0
LRU_C = 8.0
LOG2E = 1.4426950408889634
N_BRANCH = 4
SSD_CHUNK = 256
DN_CHUNK = 64
DN_BASE = 8
DN_GROUP = 4
ROW_CHUNK = 256
ATT_TQ = 64
ATT_SM_ROWS = 64
FF_CHUNK = 256
VMEM_LIMIT = 56 * 1024 * 1024

COL_Z, COL_XS, COL_QB, COL_QC, COL_KC, COL_VC, COL_GC, COL_XD, COL_YD = range(9)
COL_BC, COL_KK, COL_VV = 18, 19, 20
COL_DT, COL_BA = 42, 43
PROJ_N = 44 * PAIR
ROWT_BA, ROWT_DT = 0, 2
PROJ_T = 48


def _dot(a, b):
    return jnp.dot(a.astype(BF16), b.astype(BF16), preferred_element_type=F32)


def _dot_nt(a, b):
    return lax.dot_general(a.astype(BF16), b.astype(BF16), (((1,), (1,)), ((), ())),
                           preferred_element_type=F32)


def _dot_tn(a, b):
    return lax.dot_general(a.astype(BF16), b.astype(BF16), (((0,), (0,)), ((), ())),
                           preferred_element_type=F32)


def _split3(x):
    hi = x.astype(BF16)
    r1 = x - hi.astype(F32)
    mid = r1.astype(BF16)
    lo = (r1 - mid.astype(F32)).astype(BF16)
    return hi, mid, lo


def _dot_sel(x, sel):
    hi, mid, lo = _split3(x)
    s = sel.astype(BF16)
    return (jnp.dot(hi, s, preferred_element_type=F32) + jnp.dot(mid, s, preferred_element_type=F32)
            + jnp.dot(lo, s, preferred_element_type=F32))


def _sel_dot(sel, x):
    hi, mid, lo = _split3(x)
    s = sel.astype(BF16)
    return (jnp.dot(s, hi, preferred_element_type=F32) + jnp.dot(s, mid, preferred_element_type=F32)
            + jnp.dot(s, lo, preferred_element_type=F32))


def _iota(shape, axis):
    return lax.broadcasted_iota(jnp.int32, shape, axis)


def _silu(x):
    return x * jax.nn.sigmoid(x)


def _softplus(x):
    return jnp.maximum(x, 0.0) + jnp.log1p(jnp.exp(-jnp.abs(x)))


def _rms(x, g):
    return x * lax.rsqrt(jnp.mean(x * x, axis=-1, keepdims=True) + EPS) * g


def _head_ones():
    return (_iota((PAIR, PAIR), 0) // HEAD == _iota((PAIR, PAIR), 1) // HEAD).astype(F32)


def _head_sumsq(x):
    x2 = x * x
    hi = x2.astype(BF16)
    lo = (x2 - hi.astype(F32)).astype(BF16)
    ones = _head_ones().astype(BF16)
    return jnp.dot(hi, ones, preferred_element_type=F32) + jnp.dot(lo, ones, preferred_element_type=F32)


def _lo_tri(n):
    return (_iota((n, n), 1) <= _iota((n, n), 0)).astype(F32)


def _up_tri(n):
    return (_iota((n, n), 1) >= _iota((n, n), 0)).astype(F32)


def _conv_window(ref, r0, c, nchunks, rows, t_total):
    cur = ref[0, pl.ds(r0, rows), :]
    prev_start = pl.multiple_of(jnp.maximum(r0 - HALO, 0), HALO)
    next_start = pl.multiple_of(jnp.minimum(r0 + rows, t_total - HALO), HALO)
    prev = jnp.where(c > 0, ref[0, pl.ds(prev_start, HALO), :], 0.0)
    nxt = jnp.where(c < nchunks - 1, ref[0, pl.ds(next_start, HALO), :], 0.0)
    return jnp.concatenate([prev, cur, nxt], axis=0)


def _dwconv(ext, w, b, rows):
    out = b
    n = ext.shape[0]
    for j in range(CONV_W):
        sh = (CONV_LP - j) % n
        win = ext if sh == 0 else pltpu.roll(ext, sh, 0)
        out = out + w[j:j + 1, :] * win[HALO:HALO + rows, :]
    return out


def _expand_heads(n_src_rows, first, width):
    r = _iota((n_src_rows, width), 0)
    l = _iota((n_src_rows, width), 1)
    return (r - first == l // HEAD).astype(F32)


def _full(shape):
    zeros = (0,) * len(shape)
    return pl.BlockSpec(shape, lambda *_: zeros, pipeline_mode=pl.Buffered(1))


def _params():
    return pltpu.CompilerParams(vmem_limit_bytes=VMEM_LIMIT)


def _adaln_body(c_ref, w_ref, b_ref, o_ref):
    o_ref[0] = _dot(_silu(c_ref[...]), w_ref[0]) + b_ref[0]


def _adaln(cvec, w_ada, b_ada):
    depth = w_ada.shape[0]
    n = w_ada.shape[2]
    tn = n // N_MOD
    return pl.pallas_call(
        _adaln_body,
        out_shape=jax.ShapeDtypeStruct((depth, MOD_ROWS, n), F32),
        grid=(depth, N_MOD),
        in_specs=[pl.BlockSpec((MOD_ROWS, D_MODEL), lambda l, j: (0, 0)),
                  pl.BlockSpec((1, D_MODEL, tn), lambda l, j: (l, 0, j)),
                  pl.BlockSpec((1, 1, tn), lambda l, j: (l, 0, j))],
        out_specs=pl.BlockSpec((1, MOD_ROWS, tn), lambda l, j: (l, 0, j)),
        compiler_params=_params(),
        name="adaln",
    )(cvec, w_ada, b_ada.reshape(depth, 1, n))


def _ffn_body(h_ref, mod_ref, gain_ref, w13_ref, w2_ref, fin_ref, o_ref, acc_sc, *, rows, final):
    shift, scale, gate = rows
    ms = mod_ref[0]
    h = h_ref[0]
    xn = (_rms(h, gain_ref[...]) * (1.0 + ms[scale:scale + 1]) + ms[shift:shift + 1]).astype(BF16)
    for j in range(D_FF // FF_CHUNK):
        lo, hi = j * FF_CHUNK, (j + 1) * FF_CHUNK
        g = jnp.dot(xn, w13_ref[:, lo:hi], preferred_element_type=F32)
        u = jnp.dot(xn, w13_ref[:, D_FF + lo:D_FF + hi], preferred_element_type=F32)
        part = _dot(_silu(g) * u, w2_ref[lo:hi, :])
        if j == 0:
            acc_sc[...] = part
        else:
            acc_sc[...] += part
    hn = h + 0.5 * ms[gate:gate + 1] * acc_sc[...]
    o_ref[0] = _rms(hn, fin_ref[...]) if final else hn


def _ffn(h, mod, gain, w13, w2, fin_gain, *, rows, final):
    bsz, t, d = h.shape
    tm = min(512, t)
    mb = mod.shape[0]
    mod_ix = (lambda b, i: (b, 0, 0)) if mb > 1 else (lambda b, i: (0, 0, 0))
    return pl.pallas_call(
        functools.partial(_ffn_body, rows=rows, final=final),
        out_shape=jax.ShapeDtypeStruct((bsz, t, d), F32),
        grid=(bsz, t // tm),
        in_specs=[pl.BlockSpec((1, tm, d), lambda b, i: (b, i, 0)),
                  pl.BlockSpec((1, MOD_ROWS, d), mod_ix),
                  pl.BlockSpec((1, d), lambda b, i: (0, 0)),
                  _full((d, 2 * D_FF)), _full((D_FF, d)),
                  pl.BlockSpec((1, d), lambda b, i: (0, 0))],
        out_specs=pl.BlockSpec((1, tm, d), lambda b, i: (b, i, 0)),
        scratch_shapes=[pltpu.VMEM((tm, d), F32)],
        compiler_params=_params(),
        name="ffn",
    )(h, mod, gain, w13, w2, fin_gain)


def _proj_body(h_ref, mod_ref, gain_ref, w_ref, wt_ref, o_ref, ot_ref):
    ms = mod_ref[0]
    xn = (_rms(h_ref[0], gain_ref[...]) * (1.0 + ms[4:5]) + ms[3:4]).astype(BF16)
    ot_ref[0] = lax.dot_general(wt_ref[...], xn, (((1,), (1,)), ((), ())), preferred_element_type=F32)
    o_ref[0] = jnp.dot(xn, w_ref[...], preferred_element_type=F32)


def _proj(h, mod, gain, w, wt):
    bsz, t, d = h.shape
    n = w.shape[1]
    r = wt.shape[0]
    tm = min(512, t)
    mb = mod.shape[0]
    mod_ix = (lambda b, i: (b, 0, 0)) if mb > 1 else (lambda b, i: (0, 0, 0))
    return pl.pallas_call(
        _proj_body,
        out_shape=[jax.ShapeDtypeStruct((bsz, t, n), F32), jax.ShapeDtypeStruct((bsz, r, t), F32)],
        grid=(bsz, t // tm),
        in_specs=[pl.BlockSpec((1, tm, d), lambda b, i: (b, i, 0)),
                  pl.BlockSpec((1, MOD_ROWS, d), mod_ix),
                  pl.BlockSpec((1, d), lambda b, i: (0, 0)),
                  _full((d, n)), _full((r, d))],
        out_specs=[pl.BlockSpec((1, tm, n), lambda b, i: (b, i, 0)),
                   pl.BlockSpec((1, r, tm), lambda b, i: (b, 0, i))],
        compiler_params=_params(),
        name="proj",
    )(h, mod, gain, w, wt)


def _merge_body(h_ref, mod_ref, gain_ref, oa_ref, ob_ref, oc_ref, od_ref, wg_ref, wb_ref, wo_ref, o_ref):
    ms = mod_ref[0]
    h = h_ref[0]
    xn = (_rms(h, gain_ref[...]) * (1.0 + ms[4:5]) + ms[3:4]).astype(BF16)
    merged = None
    for n, br_ref in enumerate((oa_ref, ob_ref, oc_ref, od_ref)):
        gate = jax.nn.sigmoid(jnp.dot(xn, wg_ref[:, n * D_MODEL:(n + 1) * D_MODEL], preferred_element_type=F32))
        term = gate * jnp.dot(br_ref[0], wb_ref[n], preferred_element_type=F32)
        merged = term if merged is None else merged + term
    o_ref[0] = h + ms[5:6] * _dot(merged, wo_ref[...])


def _merge(h, mod, gain, branches, w_gate, w_branch, w_out):
    bsz, t, d = h.shape
    tm = min(512, t)
    mb = mod.shape[0]
    mod_ix = (lambda b, i: (b, 0, 0)) if mb > 1 else (lambda b, i: (0, 0, 0))
    br_spec = pl.BlockSpec((1, tm, INNER), lambda b, i: (b, i, 0))
    return pl.pallas_call(
        _merge_body,
        out_shape=jax.ShapeDtypeStruct((bsz, t, d), F32),
        grid=(bsz, t // tm),
        in_specs=[pl.BlockSpec((1, tm, d), lambda b, i: (b, i, 0)),
                  pl.BlockSpec((1, MOD_ROWS, d), mod_ix),
                  pl.BlockSpec((1, d), lambda b, i: (0, 0)),
                  br_spec, br_spec, br_spec, br_spec,
                  _full((d, N_BRANCH * d)), _full((N_BRANCH, INNER, d)), _full((d, d))],
        out_specs=pl.BlockSpec((1, tm, d), lambda b, i: (b, i, 0)),
        compiler_params=_params(),
        name="merge",
    )(h, mod, gain, *branches, w_gate, w_branch, w_out)


def _ssd_body(*refs, t, has_init, emit_state):
    (z_ref, xs_ref, bc_ref, dt_ref, dtT_ref, cwx_ref, cbx_ref, cwbc_ref, cbbc_ref,
     dtb_row_ref, dtb_col_ref, alog_row_ref, alog_col_ref, dskip_ref, gnorm_ref) = refs[:15]
    pos = 15
    h0_ref = None
    if has_init:
        h0_ref = refs[pos]
        pos += 1
    o_ref = refs[pos]
    pos += 1
    st_ref = None
    if emit_state:
        st_ref = refs[pos]
        pos += 1
    y_sc, cum_sc, cm_sc, dh_sc, tot_sc, hst_sc = refs[pos:]

    cl = SSD_CHUNK
    nc = t // cl
    lane = _iota((cl, PAIR), 1)
    ii = _iota((cl, cl), 0)
    jj = _iota((cl, cl), 1)
    lo = _lo_tri(cl)
    up = _up_tri(cl)
    hp = lax.Precision.HIGHEST
    a_row = -jnp.exp(alog_row_ref[...])
    a_col = -jnp.exp(alog_col_ref[...])
    sel_f = _expand_heads(PAIR, 0, INNER)
    sel_b = _expand_heads(PAIR, N_HEADS, INNER)

    def intra(c, carry):
        r0 = pl.multiple_of(c * cl, cl)
        xs = _silu(_dwconv(_conv_window(xs_ref, r0, c, nc, cl, t), cwx_ref[...], cbx_ref[...], cl))
        bc = _silu(_dwconv(_conv_window(bc_ref, r0, c, nc, cl, t), cwbc_ref[...], cbbc_ref[...], cl))
        bm = bc[:, :PAIR]
        cm = bc[:, PAIR:]
        dt = _softplus(dt_ref[0, pl.ds(r0, cl), :] + dtb_row_ref[...])
        dtT = _softplus(dtT_ref[0, :, pl.ds(r0, cl)] + dtb_col_ref[...])
        da = dt * a_row
        daT = dtT * a_col
        cum = jnp.where(lane < N_HEADS, _sel_dot(lo, da), _sel_dot(up, da))
        rowsel = _iota((2 * N_HEADS, cl), 0) < N_HEADS
        cumT = jnp.where(rowsel, _dot_sel(daT, up), _dot_sel(daT, lo))
        cum2 = cum * LOG2E
        cumT2 = cumT * LOG2E
        cum_sc[pl.ds(r0, cl), :] = cum
        cm_sc[pl.ds(r0, cl), :] = cm
        cb = []
        for g in range(SSM_GROUPS):
            cg = jnp.where(lane // HEAD == g, cm, 0.0)
            cb.append(_dot_nt(cg, bm))
        ypairs = []
        for p in range(N_PAIRS):
            xp = xs[:, p * PAIR:(p + 1) * PAIR].astype(BF16)
            halves = []
            for a in range(2):
                h = 2 * p + a
                g = h // (N_HEADS // SSM_GROUPS)
                hb = N_HEADS + h
                expo = jnp.where(jj <= ii, cum2[:, h:h + 1] - cumT2[h:h + 1, :], cum2[:, hb:hb + 1] - cumT2[hb:hb + 1, :])
                dtf = dtT[h:h + 1, :]
                dtb = dtT[hb:hb + 1, :]
                wdt = jnp.where(jj < ii, dtf, jnp.where(jj > ii, dtb, dtf + dtb))
                s = (cb[g] * jnp.exp2(expo) * wdt).astype(BF16)
                halves.append(jnp.dot(s, xp, preferred_element_type=F32))
            ypairs.append(jnp.where(lane < HEAD, halves[0], halves[1]))
        y = jnp.concatenate(ypairs, axis=1) + dskip_ref[...] * xs
        y_sc[pl.ds(r0, cl), :] = y
        tot = jnp.where(lane[0:1] < N_HEADS, cum[cl - 1:cl, :], cum[0:1, :])
        tot_sc[c] = jnp.broadcast_to(tot, (HALO, PAIR))
        wexp = jnp.exp(tot - cum) * dt
        for d, sel in enumerate((sel_f, sel_b)):
            xw = xs * _dot_sel(wexp, sel)
            for p in range(N_PAIRS):
                g = p // (N_PAIRS // SSM_GROUPS)
                bg = jnp.where(lane // HEAD == g, bm, 0.0)
                dh_sc[c, d, p] = _dot_tn(xw[:, p * PAIR:(p + 1) * PAIR], bg)
        return carry

    lax.fori_loop(0, nc, intra, 0)

    rr = _iota((PAIR, PAIR), 0)
    for d in range(2):
        for p in range(N_PAIRS):
            g = p // (N_PAIRS // SSM_GROUPS)
            if has_init:
                blk = h0_ref[0, d, p * PAIR:(p + 1) * PAIR, :]
                z64 = jnp.zeros_like(blk)
                hst_sc[d, p] = jnp.concatenate([blk, z64] if g == 0 else [z64, blk], axis=1)
            else:
                hst_sc[d, p] = jnp.zeros((PAIR, PAIR), F32)

    def inter(k, carry):
        for d in range(2):
            sel = sel_f if d == 0 else sel_b
            c = k if d == 0 else nc - 1 - k
            r0 = pl.multiple_of(c * cl, cl)
            ecum = _dot_sel(jnp.exp(cum_sc[pl.ds(r0, cl), :]), sel)
            cm = cm_sc[pl.ds(r0, cl), :]
            tot = tot_sc[c]
            dec = jnp.exp(tot[0:1, :])
            for p in range(N_PAIRS):
                g = p // (N_PAIRS // SSM_GROUPS)
                cg = jnp.where(lane // HEAD == g, cm, 0.0)
                hs = hst_sc[d, p]
                yi = _dot_nt(cg, hs)
                cols = pl.ds(p * PAIR, PAIR)
                y_sc[pl.ds(r0, cl), cols] = y_sc[pl.ds(r0, cl), cols] + yi * ecum[:, p * PAIR:(p + 1) * PAIR]
                h0i = d * N_HEADS + 2 * p
                dcol = jnp.where(rr < HEAD, dec[:, h0i:h0i + 1], dec[:, h0i + 1:h0i + 2])
                hst_sc[d, p] = hs * dcol + dh_sc[c, d, p]
        return carry

    lax.fori_loop(0, nc, inter, 0)
    if emit_state:
        for d in range(2):
            for p in range(N_PAIRS):
                g = p // (N_PAIRS // SSM_GROUPS)
                st_ref[0, d, p * PAIR:(p + 1) * PAIR, :] = hst_sc[d, p][:, g * HEAD:(g + 1) * HEAD]

    def finish(c, carry):
        r0 = pl.multiple_of(c * cl, cl)
        v = y_sc[pl.ds(r0, cl), :] * _silu(z_ref[0, pl.ds(r0, cl), :])
        o_ref[0, pl.ds(r0, cl), :] = _rms(v, gnorm_ref[...]).astype(BF16)
        return carry

    lax.fori_loop(0, nc, finish, 0)


def _ssd(proj, projT, prm, h0, *, emit_state):
    bsz, t, _ = proj.shape
    nc = t // SSD_CHUNK
    has_init = h0 is not None
    seq = lambda w, j: pl.BlockSpec((1, t, w), lambda b: (b, 0, j))
    in_specs = [seq(INNER, COL_Z), seq(INNER, COL_XS), seq(2 * PAIR, COL_BC), seq(PAIR, COL_DT),
                pl.BlockSpec((1, 2 * N_HEADS, t), lambda b: (b, ROWT_DT, 0))]
    args = [proj, proj, proj, proj, projT]
    for name in ("cwx", "cbx", "cwbc", "cbbc", "dtb_row", "dtb_col", "alog_row", "alog_col", "dskip", "gnorm"):
        a = prm[name]
        in_specs.append(_full(a.shape))
        args.append(a)
    if has_init:
        in_specs.append(pl.BlockSpec((1, 2, INNER, HEAD), lambda b: (b, 0, 0, 0)))
        args.append(h0)
    out_shape = [jax.ShapeDtypeStruct((bsz, t, INNER), BF16)]
    out_specs = [pl.BlockSpec((1, t, INNER), lambda b: (b, 0, 0))]
    if emit_state:
        out_shape.append(jax.ShapeDtypeStruct((bsz, 2, INNER, HEAD), F32))
        out_specs.append(pl.BlockSpec((1, 2, INNER, HEAD), lambda b: (b, 0, 0, 0)))
    res = pl.pallas_call(
        functools.partial(_ssd_body, t=t, has_init=has_init, emit_state=emit_state),
        out_shape=out_shape,
        grid=(bsz,),
        in_specs=in_specs,
        out_specs=out_specs,
        scratch_shapes=[pltpu.VMEM((t, INNER), F32), pltpu.VMEM((t, PAIR), F32), pltpu.VMEM((t, PAIR), F32),
                        pltpu.VMEM((nc, 2, N_PAIRS, PAIR, PAIR), F32), pltpu.VMEM((nc, HALO, PAIR), F32),
                        pltpu.VMEM((2, N_PAIRS, PAIR, PAIR), F32)],
        compiler_params=_params(),
        name="ssd",
    )(*args)
    return (res[0], res[1]) if emit_state else (res[0], None)


def _attn_body(*refs, t, s_ctx, rope):
    q_ref, kk_ref, vv_ref, qg_ref, kg_ref = refs[:5]
    pos = 5
    if rope:
        cos_ref, sin_ref, ck_ref, cv_ref = refs[pos:pos + 4]
        pos += 4
    o_ref = refs[pos]
    pos += 1
    if not rope:
        knew_ref, vnew_ref = refs[pos:pos + 2]
        pos += 2
    q_sc, k_sc, v_sc, s_sc, e_sc, l_sc = refs[pos:]

    rc = ROW_CHUNK
    lane = _iota((rc, PAIR), 1)
    first_half = (lane % (HEAD // 2)) < HEAD // 4

    def rot(x, cos, sin):
        partner = jnp.where(first_half, pltpu.roll(x, PAIR - HEAD // 4, 1), pltpu.roll(x, HEAD // 4, 1))
        return x * cos + partner * sin

    def normed(x, gain):
        return x * lax.rsqrt(_head_sumsq(x) * (1.0 / HEAD) + EPS) * gain

    if rope:
        k_sc[0:s_ctx, :] = ck_ref[0].astype(BF16)
        v_sc[0:s_ctx, :] = cv_ref[0].astype(BF16)

    def prep(c, carry):
        r0 = pl.multiple_of(c * rc, rc)
        if rope:
            cos = cos_ref[pl.ds(r0, rc), :]
            sin = sin_ref[pl.ds(r0, rc), :]
        for p in range(N_PAIRS):
            x = normed(q_ref[0, pl.ds(r0, rc), p * PAIR:(p + 1) * PAIR], qg_ref[...])
            if rope:
                x = rot(x, cos, sin)
            q_sc[pl.ds(r0, rc), p * PAIR:(p + 1) * PAIR] = (x * (HEAD ** -0.5 * LOG2E)).astype(BF16)
        kn = []
        for g in range(KV_HEADS):
            x = normed(kk_ref[0, pl.ds(r0, rc), g * PAIR:(g + 1) * PAIR], kg_ref[...])
            kn.append(x)
            if rope:
                x = rot(x, cos, sin)
            k_sc[pl.ds(s_ctx + r0, rc), g * PAIR:(g + 1) * PAIR] = x.astype(BF16)
        vv = vv_ref[0, pl.ds(r0, rc), :]
        v_sc[pl.ds(s_ctx + r0, rc), :] = vv.astype(BF16)
        if not rope:
            knew_ref[0, pl.ds(r0, rc), :] = jnp.where(lane < HEAD, kn[0], kn[1])
            vnew_ref[0, pl.ds(r0, rc), :] = jnp.where(lane < HEAD, vv[:, :PAIR], vv[:, PAIR:])
        return carry

    lax.fori_loop(0, t // rc, prep, 0)

    tq = ATT_TQ
    lane_q = _iota((tq, PAIR), 1)

    pairs_per_group = N_PAIRS // KV_HEADS

    n_tiles = t // tq

    def scores(i, g, dst):
        r0 = i * tq if isinstance(i, int) else pl.multiple_of(i * tq, tq)
        tiles = []
        for p in range(g * pairs_per_group, (g + 1) * pairs_per_group):
            qp = q_sc[pl.ds(r0, tq), p * PAIR:(p + 1) * PAIR]
            for a in range(2):
                tiles.append(jnp.where(lane_q // HEAD == a, qp, jnp.zeros_like(qp)))
        qs = jnp.concatenate(tiles, axis=0)
        dst[...] = lax.dot_general(qs, k_sc[:, g * PAIR:(g + 1) * PAIR], (((1,), (1,)), ((), ())),
                                   preferred_element_type=F32)

    def softmax(src, e_dst, l_dst):
        rows_u, s_all = src.shape
        rb = ATT_SM_ROWS
        for r in range(rows_u // rb):
            rows = slice(r * rb, (r + 1) * rb)
            m = src[rows, 0:PAIR]
            for c in range(1, s_all // PAIR):
                m = jnp.maximum(m, src[rows, c * PAIR:(c + 1) * PAIR])
            mb = jnp.broadcast_to(jnp.max(m, axis=-1, keepdims=True), (rb, PAIR))
            acc = None
            for c in range(s_all // PAIR):
                e = jnp.exp2(src[rows, c * PAIR:(c + 1) * PAIR] - mb)
                acc = e if acc is None else acc + e
                e_dst[rows, c * PAIR:(c + 1) * PAIR] = e.astype(BF16)
            l_dst[rows, :] = jnp.broadcast_to(jnp.sum(acc, axis=-1, keepdims=True), (rb, PAIR))

    def values(i, g, e_src, l_src):
        r0 = i * tq if isinstance(i, int) else pl.multiple_of(i * tq, tq)
        o = jnp.dot(e_src[...], v_sc[:, g * PAIR:(g + 1) * PAIR], preferred_element_type=F32) / l_src[...]
        for j in range(pairs_per_group):
            p = g * pairs_per_group + j
            pair = jnp.where(lane_q < HEAD, o[(2 * j) * tq:(2 * j + 1) * tq], o[(2 * j + 1) * tq:(2 * j + 2) * tq])
            o_ref[0, pl.ds(r0, tq), p * PAIR:(p + 1) * PAIR] = pair.astype(BF16)

    def step(i, slot):
        for g in range(KV_HEADS):
            scores(jnp.minimum(i + 1, n_tiles - 1), g, s_sc.at[1 - slot, g])
            softmax(s_sc.at[slot, g], e_sc.at[slot, g], l_sc.at[slot, g])
            values(jnp.maximum(i - 1, 0), g, e_sc.at[1 - slot, g], l_sc.at[1 - slot, g])

    for g in range(KV_HEADS):
        scores(0, g, s_sc.at[0, g])
        e_sc[1, g] = jnp.zeros(e_sc.shape[2:], BF16)
        l_sc[1, g] = jnp.ones(l_sc.shape[2:], F32)

    def attend(j, carry):
        step(2 * j, 0)
        step(2 * j + 1, 1)
        return carry

    assert n_tiles % 2 == 0
    lax.fori_loop(0, n_tiles // 2, attend, 0)
    for g in range(KV_HEADS):
        values(n_tiles - 1, g, e_sc.at[1, g], l_sc.at[1, g])


def _attn(proj, prm, rope_tabs, ctx_kv):
    bsz, t, _ = proj.shape
    rope = rope_tabs is not None
    s_ctx = ctx_kv[0].shape[1] if rope else 0
    s_all = s_ctx + t
    rows_u = (N_HEADS // KV_HEADS) * ATT_TQ
    seq = lambda w, j: pl.BlockSpec((1, t, w), lambda b: (b, 0, j), pipeline_mode=pl.Buffered(1))
    in_specs = [seq(INNER, COL_QB), seq(2 * PAIR, COL_KK), seq(2 * PAIR, COL_VV), _full((1, PAIR)), _full((1, PAIR))]
    args = [proj, proj, proj, prm["qgain"], prm["kgain"]]
    if rope:
        in_specs += [_full((t, PAIR)), _full((t, PAIR)),
                     pl.BlockSpec((1, s_ctx, 2 * PAIR), lambda b: (b, 0, 0)),
                     pl.BlockSpec((1, s_ctx, 2 * PAIR), lambda b: (b, 0, 0))]
        args += [rope_tabs[0], rope_tabs[1], ctx_kv[0], ctx_kv[1]]
    out_shape = [jax.ShapeDtypeStruct((bsz, t, INNER), BF16)]
    out_specs = [pl.BlockSpec((1, t, INNER), lambda b: (b, 0, 0))]
    if not rope:
        out_shape += [jax.ShapeDtypeStruct((bsz, t, PAIR), F32)] * 2
        out_specs += [pl.BlockSpec((1, t, PAIR), lambda b: (b, 0, 0))] * 2
    res = pl.pallas_call(
        functools.partial(_attn_body, t=t, s_ctx=s_ctx, rope=rope),
        out_shape=out_shape,
        grid=(bsz,),
        in_specs=in_specs,
        out_specs=out_specs,
        scratch_shapes=[pltpu.VMEM((t, INNER), BF16), pltpu.VMEM((s_all, 2 * PAIR), BF16),
                        pltpu.VMEM((s_all, 2 * PAIR), BF16),
                        pltpu.VMEM((2, KV_HEADS, rows_u, s_all), F32), pltpu.VMEM((2, KV_HEADS, rows_u, s_all), BF16),
                        pltpu.VMEM((2, KV_HEADS, rows_u, PAIR), F32)],
        compiler_params=_params(),
        name="attn",
    )(*args)
    return res


def _dn_body(*refs, t, has_init, emit_state):
    (q_ref, k_ref, v_ref, gate_ref, ba_ref, baT_ref, cwq_ref, cwk_ref, cwv_ref, cbq_ref, cbk_ref, cbv_ref,
     alog_row_ref, bias_row_ref, alog_col_ref, bias_col_ref, gnorm_ref) = refs[:17]
    pos = 17
    s0_ref = None
    if has_init:
        s0_ref = refs[pos]
        pos += 1
    o_ref = refs[pos]
    pos += 1
    st_ref = None
    if emit_state:
        st_ref = refs[pos]
        pos += 1
    qn_sc, kn_sc, vc_sc, bg_sc, bgT_sc, o_sc, s_sc, hu_sc, hb_sc, hg_sc = refs[pos:]

    rc = ROW_CHUNK
    cl = DN_CHUNK
    nc = t // cl
    nrc = t // rc
    hp = lax.Precision.HIGHEST
    nb = 2 * N_HEADS

    lane_rc = _iota((rc, PAIR), 1)
    row_t = _iota((2 * nb, rc), 0)

    def prep(c, carry):
        r0 = pl.multiple_of(c * rc, rc)
        for src, cw, cb, dst, norm in ((q_ref, cwq_ref, cbq_ref, qn_sc, True), (k_ref, cwk_ref, cbk_ref, kn_sc, True),
                                       (v_ref, cwv_ref, cbv_ref, vc_sc, False)):
            x = _silu(_dwconv(_conv_window(src, r0, c, nrc, rc, t), cw[...], cb[...], rc))
            if norm:
                tiles = []
                for p in range(N_PAIRS):
                    xp = x[:, p * PAIR:(p + 1) * PAIR]
                    tiles.append(xp * lax.rsqrt(_head_sumsq(xp) + EPS))
                x = jnp.concatenate(tiles, axis=1)
                if dst is qn_sc:
                    x = x * (HEAD ** -0.5)
            dst[pl.ds(r0, rc), :] = x
        o_sc[pl.ds(r0, rc), :] = jnp.zeros((rc, INNER), F32)
        ba = ba_ref[0, pl.ds(r0, rc), :]
        beta = jax.nn.sigmoid(ba)
        gdec = -jnp.exp(alog_row_ref[...]) * _softplus(ba + bias_row_ref[...])
        bg_sc[pl.ds(r0, rc), :] = jnp.where(lane_rc < nb, beta, gdec)
        baT = baT_ref[0, :, pl.ds(r0, rc)]
        betaT = jax.nn.sigmoid(baT)
        gdecT = -jnp.exp(alog_col_ref[...]) * _softplus(baT + bias_col_ref[...])
        bgT = jnp.where(row_t < nb, betaT, gdecT)
        for k in range(rc // cl):
            bgT_sc[c * (rc // cl) + k] = bgT[:, k * cl:(k + 1) * cl]
        return carry

    lax.fori_loop(0, nrc, prep, 0)

    rr = _iota((PAIR, PAIR), 0)
    cc = _iota((PAIR, PAIR), 1)
    same = (rr // HEAD) == (cc // HEAD)
    lane_c = _iota((cl, PAIR), 1)
    ii = _iota((cl, PAIR), 0)
    jj = lane_c % HEAD
    first = lane_c < HEAD
    lo = _lo_tri(cl)
    up = _up_tri(cl)
    eye = (ii == jj).astype(F32)
    offdiag = (ii != jj).astype(F32)
    blk_base = ii // DN_BASE == jj // DN_BASE
    off_masks = []
    b = DN_BASE
    while b < cl:
        off_masks.append((ii // (2 * b) == jj // (2 * b)) & (ii // b != jj // b))
        b *= 2

    for d in range(2):
        for p in range(N_PAIRS):
            if has_init:
                b0 = s0_ref[0, d, (2 * p) * HEAD:(2 * p + 1) * HEAD, :]
                b1 = s0_ref[0, d, (2 * p + 1) * HEAD:(2 * p + 2) * HEAD, :]
                z64 = jnp.zeros_like(b0)
                s_sc[d, p] = jnp.concatenate([jnp.concatenate([b0, z64], axis=1),
                                              jnp.concatenate([z64, b1], axis=1)], axis=0)
            else:
                s_sc[d, p] = jnp.zeros((PAIR, PAIR), F32)

    def bdiag(x):
        xb = x.astype(BF16)
        zero = jnp.zeros_like(xb)
        return jnp.concatenate([jnp.where(first, xb, zero), jnp.where(first, zero, xb)], axis=0)

    def pdot(x, y_bd):
        return jnp.dot(x.astype(BF16), y_bd, preferred_element_type=F32)

    def cols2(m, c0):
        return jnp.where(first, m[:, c0:c0 + 1], m[:, c0 + 1:c0 + 2])

    group = min(DN_GROUP, nc)
    systems =[(d, j, p) for j in range(group) for d in range(2) for p in range(N_PAIRS)]

    def chunk_row0(kg, d, j):
        c = kg * group + j if d == 0 else nc - 1 - (kg * group + j)
        return c, (c * cl if isinstance(c, int) else pl.multiple_of(c * cl, cl))

    def solve(kg):
        dirs = {}
        for d in range(2):
            for j in range(group):
                c, r0 = chunk_row0(kg, d, j)
                bg = bg_sc[pl.ds(r0, cl), :]
                bgT = bgT_sc[c]
                tri = lo if d == 0 else up
                triT = up if d == 0 else lo
                gc = _sel_dot(tri, bg)
                gcT = _dot_sel(bgT, triT)
                dirs[d, j] = (r0, bg, gc, gcT)
        st = []
        for d, j, p in systems:
            r0, bg, gc, gcT = dirs[d, j]
            last = cl - 1 if d == 0 else 0
            incl = (jj <= ii) if d == 0 else (jj >= ii)
            h0i = d * N_HEADS + 2 * p
            beta = cols2(bg, h0i)
            gcm = cols2(gc, nb + h0i)
            rgc = jnp.concatenate([gcT[nb + h0i:nb + h0i + 1, :], gcT[nb + h0i + 1:nb + h0i + 2, :]], axis=1)
            glast = jnp.where(first[0:1], gc[last:last + 1, nb + h0i:nb + h0i + 1],
                              gc[last:last + 1, nb + h0i + 1:nb + h0i + 2])
            decay = jnp.exp(jnp.where(incl, gcm - rgc, -jnp.inf))
            cols = pl.ds(p * PAIR, PAIR)
            kp = kn_sc[pl.ds(r0, cl), cols]
            qp = qn_sc[pl.ds(r0, cl), cols]
            vp = vc_sc[pl.ds(r0, cl), cols]
            egc = jnp.exp(gcm)
            kb = kp * beta
            st.append(dict(glast=glast, decay=decay, kp=kp, kb=kb, qp=qp, qg=qp * egc,
                           kd=kp * jnp.exp(glast - gcm), rhs_u=vp * beta, rhs_w=kb * egc))
        yield
        for e in st:
            k_bd = bdiag(e["kp"])
            kq = jnp.concatenate([e["kb"], e["qp"]], axis=0).astype(BF16)
            ga = lax.dot_general(kq, k_bd, (((1,), (1,)), ((), ())), preferred_element_type=F32)
            e["gm"] = ga[:cl]
            e["am"] = ga[cl:]
        yield
        for e in st:
            e["m"] = e["gm"] * e["decay"] * offdiag
            e["aq"] = e["am"] * e["decay"]
            e["n1"] = jnp.where(blk_base, e["m"], 0.0)
        for e in st:
            e["n2"] = pdot(e["n1"], bdiag(e["n1"]))
        yield
        for e in st:
            e["n2_bd"] = bdiag(e["n2"])
            e["n4"] = pdot(e["n2"], e["n2_bd"])
        yield
        for e in st:
            e["pm"] = eye + e["n2"] + e["n4"] + pdot(e["n4"], e["n2_bd"])
        yield
        for e in st:
            e["x"] = e["pm"] - pdot(e["n1"], bdiag(e["pm"]))
        yield
        for off_mask in off_masks:
            for e in st:
                e["cx"] = pdot(jnp.where(off_mask, e["m"], 0.0), bdiag(e["x"]))
            yield
            for e in st:
                e["x"] = e["x"] - pdot(e["x"], bdiag(e["cx"]))
            yield
        for e in st:
            rhs_bd = jnp.concatenate([bdiag(e["rhs_u"]), bdiag(e["rhs_w"])], axis=1)
            e["sol"] = pdot(e["x"], rhs_bd)
        yield
        for idx, e in enumerate(st):
            hu_sc[idx] = e["sol"][:, :PAIR]
            hb_sc[idx, 0] = e["sol"][:, PAIR:].astype(BF16)
            hb_sc[idx, 1] = e["qg"].astype(BF16)
            hb_sc[idx, 2] = e["aq"].astype(BF16)
            hb_sc[idx, 3] = e["kd"].astype(BF16)
            hg_sc[idx] = jnp.broadcast_to(e["glast"], (HALO, PAIR))

    def recur(kg):
        state = {(d, p): s_sc[d, p] for d in range(2) for p in range(N_PAIRS)}
        for j in range(group):
            cur = [(idx, d, p) for idx, (d, jj_, p) in enumerate(systems) if jj_ == j]
            tmp = {}
            for idx, d, p in cur:
                wq = jnp.dot(jnp.concatenate([hb_sc[idx, 0], hb_sc[idx, 1]], axis=0), state[d, p].astype(BF16),
                             preferred_element_type=F32)
                tmp[idx] = (hu_sc[idx] - wq[:cl], wq[cl:])
            yield
            for idx, d, p in cur:
                vnew, qs = tmp[idx]
                _, r0 = chunk_row0(kg, d, j)
                o_sc[pl.ds(r0, cl), pl.ds(p * PAIR, PAIR)] += qs + jnp.dot(hb_sc[idx, 2], bdiag(vnew),
                                                                           preferred_element_type=F32)
                upd = jnp.where(same, _dot_tn(hb_sc[idx, 3], vnew), 0.0)
                glast = hg_sc[idx][0:1, :]
                gl_rows = jnp.where(rr[:, 0:1] < HEAD, glast[:, 0:1], glast[:, HEAD:HEAD + 1])
                state[d, p] = state[d, p] * jnp.exp(gl_rows) + upd
            yield
        for (d, p), s in state.items():
            s_sc[d, p] = s

    def run(*gens):
        gens = list(gens)
        while gens:
            for g in list(gens):
                try:
                    next(g)
                except StopIteration:
                    gens.remove(g)

    n_groups = nc // group
    run(solve(0))

    def chunk(k, carry):
        run(recur(k), solve(k + 1))
        return carry

    lax.fori_loop(0, n_groups - 1, chunk, 0)
    run(recur(n_groups - 1))

    if emit_state:
        for d in range(2):
            for p in range(N_PAIRS):
                s = s_sc[d, p]
                st_ref[0, d, (2 * p) * HEAD:(2 * p + 1) * HEAD, :] = s[:HEAD, :HEAD]
                st_ref[0, d, (2 * p + 1) * HEAD:(2 * p + 2) * HEAD, :] = s[HEAD:, HEAD:]

    def finish(c, carry):
        r0 = pl.multiple_of(c * rc, rc)
        for p in range(N_PAIRS):
            cols = pl.ds(p * PAIR, PAIR)
            o = o_sc[pl.ds(r0, rc), cols]
            on = o * lax.rsqrt(_head_sumsq(o) * (1.0 / HEAD) + EPS) * gnorm_ref[...]
            o_ref[0, pl.ds(r0, rc), cols] = (on * _silu(gate_ref[0, pl.ds(r0, rc), cols])).astype(BF16)
        return carry

    lax.fori_loop(0, nrc, finish, 0)


def _dn(proj, projT, prm, s0, *, emit_state):
    bsz, t, _ = proj.shape
    nc = t // DN_CHUNK
    n_sys = min(DN_GROUP, nc) * 2 * N_PAIRS
    has_init = s0 is not None
    seq = lambda w, j: pl.BlockSpec((1, t, w), lambda b: (b, 0, j), pipeline_mode=pl.Buffered(1))
    in_specs = [seq(INNER, COL_QC), seq(INNER, COL_KC), seq(INNER, COL_VC), seq(INNER, COL_GC), seq(PAIR, COL_BA),
                pl.BlockSpec((1, 4 * N_HEADS, t), lambda b: (b, ROWT_BA, 0), pipeline_mode=pl.Buffered(1))]
    args = [proj] * 5 + [projT]
    for name in ("cwq", "cwk", "cwv", "cbq", "cbk", "cbv", "alog_row", "bias_row", "alog_col", "bias_col", "gnorm"):
        a = prm[name]
        in_specs.append(_full(a.shape))
        args.append(a)
    if has_init:
        in_specs.append(pl.BlockSpec((1, 2, INNER, HEAD), lambda b: (b, 0, 0, 0)))
        args.append(s0)
    out_shape = [jax.ShapeDtypeStruct((bsz, t, INNER), BF16)]
    out_specs = [pl.BlockSpec((1, t, INNER), lambda b: (b, 0, 0))]
    if emit_state:
        out_shape.append(jax.ShapeDtypeStruct((bsz, 2, INNER, HEAD), F32))
        out_specs.append(pl.BlockSpec((1, 2, INNER, HEAD), lambda b: (b, 0, 0, 0)))
    res = pl.pallas_call(
        functools.partial(_dn_body, t=t, has_init=has_init, emit_state=emit_state),
        out_shape=out_shape,
        grid=(bsz,),
        in_specs=in_specs,
        out_specs=out_specs,
        scratch_shapes=[pltpu.VMEM((t, INNER), F32), pltpu.VMEM((t, INNER), F32), pltpu.VMEM((t, INNER), F32),
                        pltpu.VMEM((t, PAIR), F32), pltpu.VMEM((nc, 4 * N_HEADS, DN_CHUNK), F32),
                        pltpu.VMEM((t, INNER), F32),
                        pltpu.VMEM((2, N_PAIRS, PAIR, PAIR), F32),
                        pltpu.VMEM((n_sys, DN_CHUNK, PAIR), F32), pltpu.VMEM((n_sys, 4, DN_CHUNK, PAIR), BF16),
                        pltpu.VMEM((n_sys, HALO, PAIR), F32)],
        compiler_params=_params(),
        name="deltanet",
    )(*args)
    return (res[0], res[1]) if emit_state else (res[0], None)


def _lru_body(*refs, t, has_init, emit_state):
    x_ref, y_ref, cw_ref, cb_ref, wbd_ref, bias_ref, lam_ref = refs[:7]
    pos = 7
    h0_ref = None
    if has_init:
        h0_ref = refs[pos]
        pos += 1
    o_ref = refs[pos]
    pos += 1
    st_ref = None
    if emit_state:
        st_ref = refs[pos]
        pos += 1
    af_sc, uf_sc, ab_sc, ub_sc = refs[pos:]

    rc = ROW_CHUNK
    nrc = t // rc
    row8 = _iota((HALO, INNER), 0)
    sp_lam = _softplus(-lam_ref[...])

    def gates(c, carry):
        r0 = pl.multiple_of(c * rc, rc)
        xl_all = _dwconv(_conv_window(x_ref, r0, c, nrc, rc, t), cw_ref[...], cb_ref[...], rc)
        for p in range(N_PAIRS):
            cols = pl.ds(p * PAIR, PAIR)
            xl = xl_all[:, p * PAIR:(p + 1) * PAIR]
            pre = _dot(xl, wbd_ref[p]) + bias_ref[p]
            for d, (a_sc, u_sc) in enumerate(((af_sc, uf_sc), (ab_sc, ub_sc))):
                r = jax.nn.sigmoid(pre[:, d * PAIR:(d + 1) * PAIR])
                ig = jax.nn.sigmoid(pre[:, (2 + d) * PAIR:(3 + d) * PAIR])
                log_a = -LRU_C * r * sp_lam[d:d + 1, p * PAIR:(p + 1) * PAIR]
                a = jnp.exp(log_a)
                a_sc[pl.ds(r0, rc), cols] = a
                u_sc[pl.ds(r0, rc), cols] = jnp.sqrt((1.0 - a) * (1.0 + a)) * ig * xl
        return carry

    lax.fori_loop(0, nrc, gates, 0)

    if has_init:
        cf0 = h0_ref[0, 0:1, :]
        cb0 = h0_ref[0, 1:2, :]
    else:
        cf0 = jnp.zeros((1, INNER), F32)
        cb0 = cf0

    def scan(k, carry):
        cf, cb = carry
        rf = pl.multiple_of(k * HALO, HALO)
        a8 = af_sc[pl.ds(rf, HALO), :]
        b8 = uf_sc[pl.ds(rf, HALO), :]
        for s in (1, 2, 4):
            ok = row8 >= s
            b8 = jnp.where(ok, a8 * pltpu.roll(b8, s, 0) + b8, b8)
            a8 = jnp.where(ok, a8 * pltpu.roll(a8, s, 0), a8)
        hf = a8 * cf + b8
        uf_sc[pl.ds(rf, HALO), :] = hf
        rb = pl.multiple_of(t - HALO - k * HALO, HALO)
        a8 = ab_sc[pl.ds(rb, HALO), :]
        b8 = ub_sc[pl.ds(rb, HALO), :]
        for s in (1, 2, 4):
            ok = row8 < HALO - s
            b8 = jnp.where(ok, a8 * pltpu.roll(b8, HALO - s, 0) + b8, b8)
            a8 = jnp.where(ok, a8 * pltpu.roll(a8, HALO - s, 0), a8)
        hb = a8 * cb + b8
        ub_sc[pl.ds(rb, HALO), :] = hb
        return hf[HALO - 1:HALO, :], hb[0:1, :]

    cf, cb = lax.fori_loop(0, t // HALO, scan, (cf0, cb0))
    if emit_state:
        st_ref[0, 0:1, :] = cf
        st_ref[0, 1:2, :] = cb

    def finish(c, carry):
        r0 = pl.multiple_of(c * rc, rc)
        y = y_ref[0, pl.ds(r0, rc), :]
        gelu = 0.5 * y * (1.0 + jnp.tanh(math.sqrt(2.0 / math.pi) * (y + 0.044715 * (y * y * y))))
        o_ref[0, pl.ds(r0, rc), :] = ((uf_sc[pl.ds(r0, rc), :] + ub_sc[pl.ds(r0, rc), :]) * gelu).astype(BF16)
        return carry

    lax.fori_loop(0, nrc, finish, 0)


def _lru(proj, prm, h0, *, emit_state):
    bsz, t, _ = proj.shape
    has_init = h0 is not None
    seq = lambda w, j: pl.BlockSpec((1, t, w), lambda b: (b, 0, j))
    in_specs = [seq(INNER, COL_XD), seq(INNER, COL_YD)]
    args = [proj, proj]
    for name in ("cw", "cb", "wbd", "bias", "lam"):
        a = prm[name]
        in_specs.append(_full(a.shape))
        args.append(a)
    if has_init:
        in_specs.append(pl.BlockSpec((1, 2, INNER), lambda b: (b, 0, 0)))
        args.append(h0)
    out_shape = [jax.ShapeDtypeStruct((bsz, t, INNER), BF16)]
    out_specs = [pl.BlockSpec((1, t, INNER), lambda b: (b, 0, 0))]
    if emit_state:
        out_shape.append(jax.ShapeDtypeStruct((bsz, 2, INNER), F32))
        out_specs.append(pl.BlockSpec((1, 2, INNER), lambda b: (b, 0, 0)))
    res = pl.pallas_call(
        functools.partial(_lru_body, t=t, has_init=has_init, emit_state=emit_state),
        out_shape=out_shape,
        grid=(bsz,),
        in_specs=in_specs,
        out_specs=out_specs,
        scratch_shapes=[pltpu.VMEM((t, INNER), F32)] * 4,
        compiler_params=_params(),
        name="rglru",
    )(*args)
    return (res[0], res[1]) if emit_state else (res[0], None)


def _pad_cols(x, n):
    return jnp.pad(x, ((0, 0), (0, n - x.shape[1])))


def _layer_params(l, w_in, ssm_conv_w, ssm_conv_b, ssm_a_log, ssm_dt_bias, ssm_d, ssm_norm, attn_q_norm, attn_k_norm,
                  dn_conv_w, dn_conv_b, dn_a_log, dn_dt_bias, dn_norm, lru_conv_w, lru_conv_b, lru_w_a, lru_b_a,
                  lru_w_i, lru_b_i, lru_lambda):
    w = w_in[l]
    o = 0

    def take(n):
        nonlocal o
        part = w[:, o:o + n]
        o += n
        return part

    z_a, xs_a, bc_a, dt_a = take(INNER), take(INNER), take(2 * PAIR), take(2 * N_HEADS)
    q_b, k_b, v_b = take(INNER), take(PAIR), take(PAIR)
    q_c, k_c, v_c = take(INNER), take(INNER), take(INNER)
    beta_c, a_c, gate_c = take(2 * N_HEADS), take(2 * N_HEADS), take(INNER)
    x_d, y_d = take(INNER), take(INNER)
    gate_raw = take(N_BRANCH * D_MODEL)

    def dup(x):
        return jnp.concatenate([x[:, :HEAD], x[:, :HEAD], x[:, HEAD:], x[:, HEAD:]], axis=1)

    nh2 = 2 * N_HEADS
    ba_c = jnp.concatenate([beta_c, a_c], axis=1)
    w_proj = jnp.concatenate([z_a, xs_a, q_b, q_c, k_c, v_c, gate_c, x_d, y_d, bc_a, dup(k_b), dup(v_b),
                              _pad_cols(dt_a, PAIR), _pad_cols(ba_c, PAIR)], axis=1)
    assert w_proj.shape[1] == PROJ_N
    prm = {
        "w_proj": w_proj.astype(BF16),
        "w_projT": jnp.concatenate([ba_c, dt_a], axis=1).T.astype(BF16),
        "w_gate": gate_raw.astype(BF16),
    }
    cw, cb = ssm_conv_w[l], ssm_conv_b[l][None, :]
    dtb = ssm_dt_bias[l].reshape(1, nh2)
    alog = ssm_a_log[l].reshape(1, nh2)
    prm["ssd"] = {
        "cwx": cw[:, :INNER], "cbx": cb[:, :INNER], "cwbc": cw[:, INNER:], "cbbc": cb[:, INNER:],
        "dtb_row": _pad_cols(dtb, PAIR), "dtb_col": jnp.broadcast_to(dtb.T, (nh2, SSD_CHUNK)),
        "alog_row": _pad_cols(alog, PAIR), "alog_col": jnp.broadcast_to(alog.T, (nh2, SSD_CHUNK)),
        "dskip": jnp.repeat(ssm_d[l], HEAD)[None, :], "gnorm": ssm_norm[l][None, :],
    }
    prm["att"] = {"qgain": jnp.tile(attn_q_norm[l], 2)[None, :], "kgain": jnp.tile(attn_k_norm[l], 2)[None, :]}
    dcw, dcb = dn_conv_w[l], dn_conv_b[l][None, :]
    zeros16 = jnp.zeros((1, nh2), F32)
    d_alog = jnp.concatenate([zeros16, dn_a_log[l].reshape(1, nh2)], axis=1)
    d_bias = jnp.concatenate([zeros16, dn_dt_bias[l].reshape(1, nh2)], axis=1)
    prm["dn"] = {
        "cwq": dcw[:, :INNER], "cwk": dcw[:, INNER:2 * INNER], "cwv": dcw[:, 2 * INNER:],
        "cbq": dcb[:, :INNER], "cbk": dcb[:, INNER:2 * INNER], "cbv": dcb[:, 2 * INNER:],
        "alog_row": _pad_cols(d_alog, PAIR), "bias_row": _pad_cols(d_bias, PAIR),
        "alog_col": jnp.broadcast_to(d_alog.T, (2 * nh2, ROW_CHUNK)),
        "bias_col": jnp.broadcast_to(d_bias.T, (2 * nh2, ROW_CHUNK)),
        "gnorm": jnp.tile(dn_norm[l], 2)[None, :],
    }
    wa, wi = lru_w_a[l], lru_w_i[l]
    z64 = jnp.zeros((HEAD, HEAD), F32)
    wbd, bias = [], []
    for p in range(N_PAIRS):
        blocks = []
        for wsrc in (wa, wi):
            for d in range(2):
                top = jnp.concatenate([wsrc[d, 2 * p], z64], axis=1)
                bot = jnp.concatenate([z64, wsrc[d, 2 * p + 1]], axis=1)
                blocks.append(jnp.concatenate([top, bot], axis=0))
        wbd.append(jnp.concatenate(blocks, axis=1))
        bias.append(jnp.concatenate([lru_b_a[l][0, p * PAIR:(p + 1) * PAIR], lru_b_a[l][1, p * PAIR:(p + 1) * PAIR],
                                     lru_b_i[l][0, p * PAIR:(p + 1) * PAIR], lru_b_i[l][1, p * PAIR:(p + 1) * PAIR]])[None, :])
    prm["lru"] = {"cw": lru_conv_w[l], "cb": lru_conv_b[l][None, :], "wbd": jnp.stack(wbd).astype(BF16),
                  "bias": jnp.stack(bias), "lam": lru_lambda[l]}
    return prm


def _rope_tables(t):
    n_freq = HEAD // 4
    inv = ROPE_THETA ** (-jnp.arange(n_freq, dtype=F32) / n_freq)
    rows = t // GRID_W
    row = jnp.repeat(jnp.arange(rows, dtype=F32), GRID_W)
    col = jnp.tile(jnp.arange(GRID_W, dtype=F32), rows)
    ang_r = row[:, None] * inv
    ang_c = col[:, None] * inv
    ang = jnp.concatenate([ang_r, ang_r, ang_c, ang_c], axis=1)
    sign = jnp.tile(jnp.concatenate([-jnp.ones((n_freq,), F32), jnp.ones((n_freq,), F32)]), 2)
    cos = jnp.cos(ang)
    sin = jnp.sin(ang) * sign
    return jnp.tile(cos, (1, 2)), jnp.tile(sin, (1, 2))


def _dup_kv(x):
    return jnp.concatenate([x[:, :, 0], x[:, :, 0], x[:, :, 1], x[:, :, 1]], axis=-1)


def _trunk_layer(h, mod, lw, prm, rope_tabs, ctx, fin_gain, *, final):
    h = _ffn(h, mod, lw["norm_ffn1"], lw["ffn1_w13"], lw["ffn1_w2"], fin_gain, rows=(0, 1, 2), final=False)
    gain = lw["norm_mix"]
    emit = ctx is None
    proj, projT = _proj(h, mod, gain, prm["w_proj"], prm["w_projT"])
    if emit:
        o_a, st_ssm = _ssd(proj, projT, prm["ssd"], None, emit_state=True)
        o_b, k_new, v_new = _attn(proj, prm["att"], None, None)
        o_c, st_dn = _dn(proj, projT, prm["dn"], None, emit_state=True)
        o_d, st_lru = _lru(proj, prm["lru"], None, emit_state=True)
        new_ctx = (k_new, v_new, st_ssm, st_dn, st_lru)
    else:
        ck, cv, ssm0, dn0, lru0 = ctx
        o_a, _ = _ssd(proj, projT, prm["ssd"], ssm0, emit_state=False)
        (o_b,) = _attn(proj, prm["att"], rope_tabs, (ck, cv))
        o_c, _ = _dn(proj, projT, prm["dn"], dn0, emit_state=False)
        o_d, _ = _lru(proj, prm["lru"], lru0, emit_state=False)
        new_ctx = None
    h = _merge(h, mod, gain, (o_a, o_b, o_c, o_d), prm["w_gate"], lw["w_branch"], lw["w_out"])
    h = _ffn(h, mod, lw["norm_ffn2"], lw["ffn2_w13"], lw["ffn2_w2"], fin_gain, rows=(6, 7, 8), final=final)
    return h, new_ctx


def kernel(x_prompt, x_sample, cache_k, cache_v, state_ssm, state_delta, state_lru, c, c_ctx,
           w_ada, b_ada, norm_ffn1, ffn1_w13, ffn1_w2, norm_mix, w_in,
           ssm_conv_w, ssm_conv_b, ssm_a_log, ssm_dt_bias, ssm_d, ssm_norm,
           attn_q_norm, attn_k_norm,
           dn_conv_w, dn_conv_b, dn_a_log, dn_dt_bias, dn_norm,
           lru_conv_w, lru_conv_b, lru_w_a, lru_b_a, lru_w_i, lru_b_i, lru_lambda,
           w_branch, w_out, norm_ffn2, ffn2_w13, ffn2_w2, final_norm):
    depth = w_in.shape[0]
    bsz_p, t_p, _ = x_prompt.shape
    bsz_s, t_s, _ = x_sample.shape
    assert bsz_s + 1 <= MOD_ROWS
    cvec = jnp.concatenate([c, c_ctx[None], jnp.zeros((MOD_ROWS - bsz_s - 1, D_MODEL), F32)], axis=0)
    mod_all = _adaln(cvec, w_ada, b_ada)
    mod_all = mod_all.reshape(depth, MOD_ROWS, N_MOD, D_MODEL)
    mod_all = jnp.pad(mod_all, ((0, 0), (0, 0), (0, MOD_ROWS - N_MOD), (0, 0)))
    rope_tabs = _rope_tables(t_s)
    fin_gain = final_norm[None, :]

    hp, hs = x_prompt, x_sample
    ks, vs, ssm_s, dn_s, lru_s = [], [], [], [], []
    for l in range(depth):
        prm = _layer_params(l, w_in, ssm_conv_w, ssm_conv_b, ssm_a_log, ssm_dt_bias, ssm_d, ssm_norm,
                            attn_q_norm, attn_k_norm, dn_conv_w, dn_conv_b, dn_a_log, dn_dt_bias, dn_norm,
                            lru_conv_w, lru_conv_b, lru_w_a, lru_b_a, lru_w_i, lru_b_i, lru_lambda)
        lw = {
            "norm_ffn1": norm_ffn1[l][None, :], "ffn1_w13": ffn1_w13[l].astype(BF16), "ffn1_w2": ffn1_w2[l].astype(BF16),
            "norm_mix": norm_mix[l][None, :], "w_branch": w_branch[l].astype(BF16), "w_out": w_out[l].astype(BF16),
            "norm_ffn2": norm_ffn2[l][None, :], "ffn2_w13": ffn2_w13[l].astype(BF16), "ffn2_w2": ffn2_w2[l].astype(BF16),
        }
        final = l == depth - 1
        mod_lat = mod_all[l, :bsz_s]
        mod_ctx = mod_all[l, bsz_s:bsz_s + 1]
        hp, st = _trunk_layer(hp, mod_ctx, lw, prm, None, None, fin_gain, final=final)
        k_c, v_c, st_ssm, st_dn, st_lru = st
        ks.append(k_c.reshape(bsz_p, t_p, KV_HEADS, HEAD))
        vs.append(v_c.reshape(bsz_p, t_p, KV_HEADS, HEAD))
        ssm_s.append(st_ssm.reshape(bsz_p, 2, N_HEADS, HEAD, HEAD))
        dn_s.append(st_dn.reshape(bsz_p, 2, N_HEADS, HEAD, HEAD))
        lru_s.append(st_lru)
        ctx_l = (_dup_kv(cache_k[:, l]), _dup_kv(cache_v[:, l]),
                 state_ssm[:, l].reshape(bsz_s, 2, INNER, HEAD), state_delta[:, l].reshape(bsz_s, 2, INNER, HEAD),
                 state_lru[:, l])
        hs, _ = _trunk_layer(hs, mod_lat, lw, prm, rope_tabs, ctx_l, fin_gain, final=final)
    return (hp, hs, jnp.stack(ks, axis=1), jnp.stack(vs, axis=1), jnp.stack(ssm_s, axis=1),
            jnp.stack(dn_s, axis=1), jnp.stack(lru_s, axis=1))
```

```python
import functools
import math

import jax
import jax.numpy as jnp
from jax import lax
from jax.experimental import pallas as pl
from jax.experimental.pallas import tpu as pltpu

F32 = jnp.float32
BF16 = jnp.bfloat16

D_MODEL = 1024
D_FF = 2816
N_MOD = 9
MOD_ROWS = 16
EPS = 1e-6
GRID_W = 64
CONV_W = 4
CONV_LP = CONV_W // 2
HALO = 8
HEAD = 64
PAIR = 2 * HEAD
N_HEADS = 8
N_PAIRS = N_HEADS // 2
INNER = N_HEADS * HEAD
KV_HEADS = 2
SSM_GROUPS = 2
ROPE_THETA = 10000.0
LRU_C = 8.0
LOG2E = 1.4426950408889634
N_BRANCH = 4
SSD_CHUNK = 256
DN_CHUNK = 64
DN_BASE = 8
DN_GROUP = 4
ROW_CHUNK = 256
ATT_TQ = 64
ATT_SM_ROWS = 64
FF_CHUNK = 256
VMEM_LIMIT = 56 * 1024 * 1024

COL_Z, COL_XS, COL_QB, COL_QC, COL_KC, COL_VC, COL_GC, COL_XD, COL_YD = range(9)
COL_BC, COL_KK, COL_VV = 18, 19, 20
COL_DT, COL_BA = 42, 43
PROJ_N = 44 * PAIR
ROWT_BA, ROWT_DT = 0, 2
PROJ_T = 48


def _dot(a, b):
    return jnp.dot(a.astype(BF16), b.astype(BF16), preferred_element_type=F32)


def _dot_nt(a, b):
    return lax.dot_general(a.astype(BF16), b.astype(BF16), (((1,), (1,)), ((), ())),
                           preferred_element_type=F32)


def _dot_tn(a, b):
    return lax.dot_general(a.astype(BF16), b.astype(BF16), (((0,), (0,)), ((), ())),
                           preferred_element_type=F32)


def _split3(x):
    hi = x.astype(BF16)
    r1 = x - hi.astype(F32)
    mid = r1.astype(BF16)
    lo = (r1 - mid.astype(F32)).astype(BF16)
    return hi, mid, lo


def _dot_sel(x, sel):
    hi, mid, lo = _split3(x)
    s = sel.astype(BF16)
    return (jnp.dot(hi, s, preferred_element_type=F32) + jnp.dot(mid, s, preferred_element_type=F32)
            + jnp.dot(lo, s, preferred_element_type=F32))


def _sel_dot(sel, x):
    hi, mid, lo = _split3(x)
    s = sel.astype(BF16)
    return (jnp.dot(s, hi, preferred_element_type=F32) + jnp.dot(s, mid, preferred_element_type=F32)
            + jnp.dot(s, lo, preferred_element_type=F32))


def _iota(shape, axis):
    return lax.broadcasted_iota(jnp.int32, shape, axis)


def _silu(x):
    return x * jax.nn.sigmoid(x)


def _softplus(x):
    return jnp.maximum(x, 0.0) + jnp.log1p(jnp.exp(-jnp.abs(x)))


def _rms(x, g):
    return x * lax.rsqrt(jnp.mean(x * x, axis=-1, keepdims=True) + EPS) * g


def _head_ones():
    return (_iota((PAIR, PAIR), 0) // HEAD == _iota((PAIR, PAIR), 1) // HEAD).astype(F32)


def _head_sumsq(x):
    x2 = x * x
    hi = x2.astype(BF16)
    lo = (x2 - hi.astype(F32)).astype(BF16)
    ones = _head_ones().astype(BF16)
    return jnp.dot(hi, ones, preferred_element_type=F32) + jnp.dot(lo, ones, preferred_element_type=F32)


def _lo_tri(n):
    return (_iota((n, n), 1) <= _iota((n, n), 0)).astype(F32)


def _up_tri(n):
    return (_iota((n, n), 1) >= _iota((n, n), 0)).astype(F32)


def _conv_window(ref, r0, c, nchunks, rows, t_total):
    cur = ref[0, pl.ds(r0, rows), :]
    prev_start = pl.multiple_of(jnp.maximum(r0 - HALO, 0), HALO)
    next_start = pl.multiple_of(jnp.minimum(r0 + rows, t_total - HALO), HALO)
    prev = jnp.where(c > 0, ref[0, pl.ds(prev_start, HALO), :], 0.0)
    nxt = jnp.where(c < nchunks - 1, ref[0, pl.ds(next_start, HALO), :], 0.0)
    return jnp.concatenate([prev, cur, nxt], axis=0)


def _dwconv(ext, w, b, rows):
    out = b
    n = ext.shape[0]
    for j in range(CONV_W):
        sh = (CONV_LP - j) % n
        win = ext if sh == 0 else pltpu.roll(ext, sh, 0)
        out = out + w[j:j + 1, :] * win[HALO:HALO + rows, :]
    return out


def _expand_heads(n_src_rows, first, width):
    r = _iota((n_src_rows, width), 0)
    l = _iota((n_src_rows, width), 1)
    return (r - first == l // HEAD).astype(F32)


def _full(shape):
    zeros = (0,) * len(shape)
    return pl.BlockSpec(shape, lambda *_: zeros, pipeline_mode=pl.Buffered(1))


def _params():
    return pltpu.CompilerParams(vmem_limit_bytes=VMEM_LIMIT)


def _adaln_body(c_ref, w_ref, b_ref, o_ref):
    o_ref[0] = _dot(_silu(c_ref[...]), w_ref[0]) + b_ref[0]


def _adaln(cvec, w_ada, b_ada):
    depth = w_ada.shape[0]
    n = w_ada.shape[2]
    tn = n // N_MOD
    return pl.pallas_call(
        _adaln_body,
        out_shape=jax.ShapeDtypeStruct((depth, MOD_ROWS, n), F32),
        grid=(depth, N_MOD),
        in_specs=[pl.BlockSpec((MOD_ROWS, D_MODEL), lambda l, j: (0, 0)),
                  pl.BlockSpec((1, D_MODEL, tn), lambda l, j: (l, 0, j)),
                  pl.BlockSpec((1, 1, tn), lambda l, j: (l, 0, j))],
        out_specs=pl.BlockSpec((1, MOD_ROWS, tn), lambda l, j: (l, 0, j)),
        compiler_params=_params(),
        name="adaln",
    )(cvec, w_ada, b_ada.reshape(depth, 1, n))


def _ffn_body(h_ref, mod_ref, gain_ref, w13_ref, w2_ref, fin_ref, o_ref, acc_sc, *, rows, final):
    shift, scale, gate = rows
    ms = mod_ref[0]
    h = h_ref[0]
    xn = (_rms(h, gain_ref[...]) * (1.0 + ms[scale:scale + 1]) + ms[shift:shift + 1]).astype(BF16)
    for j in range(D_FF // FF_CHUNK):
        lo, hi = j * FF_CHUNK, (j + 1) * FF_CHUNK
        g = jnp.dot(xn, w13_ref[:, lo:hi], preferred_element_type=F32)
        u = jnp.dot(xn, w13_ref[:, D_FF + lo:D_FF + hi], preferred_element_type=F32)
        part = _dot(_silu(g) * u, w2_ref[lo:hi, :])
        if j == 0:
            acc_sc[...] = part
        else:
            acc_sc[...] += part
    hn = h + 0.5 * ms[gate:gate + 1] * acc_sc[...]
    o_ref[0] = _rms(hn, fin_ref[...]) if final else hn


def _ffn(h, mod, gain, w13, w2, fin_gain, *, rows, final):
    bsz, t, d = h.shape
    tm = min(512, t)
    mb = mod.shape[0]
    mod_ix = (lambda b, i: (b, 0, 0)) if mb > 1 else (lambda b, i: (0, 0, 0))
    return pl.pallas_call(
        functools.partial(_ffn_body, rows=rows, final=final),
        out_shape=jax.ShapeDtypeStruct((bsz, t, d), F32),
        grid=(bsz, t // tm),
        in_specs=[pl.BlockSpec((1, tm, d), lambda b, i: (b, i, 0)),
                  pl.BlockSpec((1, MOD_ROWS, d), mod_ix),
                  pl.BlockSpec((1, d), lambda b, i: (0, 0)),
                  _full((d, 2 * D_FF)), _full((D_FF, d)),
                  pl.BlockSpec((1, d), lambda b, i: (0, 0))],
        out_specs=pl.BlockSpec((1, tm, d), lambda b, i: (b, i, 0)),
        scratch_shapes=[pltpu.VMEM((tm, d), F32)],
        compiler_params=_params(),
        name="ffn",
    )(h, mod, gain, w13, w2, fin_gain)


def _proj_body(h_ref, mod_ref, gain_ref, w_ref, wt_ref, o_ref, ot_ref):
    ms = mod_ref[0]
    xn = (_rms(h_ref[0], gain_ref[...]) * (1.0 + ms[4:5]) + ms[3:4]).astype(BF16)
    ot_ref[0] = lax.dot_general(wt_ref[...], xn, (((1,), (1,)), ((), ())), preferred_element_type=F32)
    o_ref[0] = jnp.dot(xn, w_ref[...], preferred_element_type=F32)


def _proj(h, mod, gain, w, wt):
    bsz, t, d = h.shape
    n = w.shape[1]
    r = wt.shape[0]
    tm = min(512, t)
    mb = mod.shape[0]
    mod_ix = (lambda b, i: (b, 0, 0)) if mb > 1 else (lambda b, i: (0, 0, 0))
    return pl.pallas_call(
        _proj_body,
        out_shape=[jax.ShapeDtypeStruct((bsz, t, n), F32), jax.ShapeDtypeStruct((bsz, r, t), F32)],
        grid=(bsz, t // tm),
        in_specs=[pl.BlockSpec((1, tm, d), lambda b, i: (b, i, 0)),
                  pl.BlockSpec((1, MOD_ROWS, d), mod_ix),
                  pl.BlockSpec((1, d), lambda b, i: (0, 0)),
                  _full((d, n)), _full((r, d))],
        out_specs=[pl.BlockSpec((1, tm, n), lambda b, i: (b, i, 0)),
                   pl.BlockSpec((1, r, tm), lambda b, i: (b, 0, i))],
        compiler_params=_params(),
        name="proj",
    )(h, mod, gain, w, wt)


def _merge_body(h_ref, mod_ref, gain_ref, oa_ref, ob_ref, oc_ref, od_ref, wg_ref, wb_ref, wo_ref, o_ref):
    ms = mod_ref[0]
    h = h_ref[0]
    xn = (_rms(h, gain_ref[...]) * (1.0 + ms[4:5]) + ms[3:4]).astype(BF16)
    merged = None
    for n, br_ref in enumerate((oa_ref, ob_ref, oc_ref, od_ref)):
        gate = jax.nn.sigmoid(jnp.dot(xn, wg_ref[:, n * D_MODEL:(n + 1) * D_MODEL], preferred_element_type=F32))
        term = gate * jnp.dot(br_ref[0], wb_ref[n], preferred_element_type=F32)
        merged = term if merged is None else merged + term
    o_ref[0] = h + ms[5:6] * _dot(merged, wo_ref[...])


def _merge(h, mod, gain, branches, w_gate, w_branch, w_out):
    bsz, t, d = h.shape
    tm = min(512, t)
    mb = mod.shape[0]
    mod_ix = (lambda b, i: (b, 0, 0)) if mb > 1 else (lambda b, i: (0, 0, 0))
    br_spec = pl.BlockSpec((1, tm, INNER), lambda b, i: (b, i, 0))
    return pl.pallas_call(
        _merge_body,
        out_shape=jax.ShapeDtypeStruct((bsz, t, d), F32),
        grid=(bsz, t // tm),
        in_specs=[pl.BlockSpec((1, tm, d), lambda b, i: (b, i, 0)),
                  pl.BlockSpec((1, MOD_ROWS, d), mod_ix),
                  pl.BlockSpec((1, d), lambda b, i: (0, 0)),
                  br_spec, br_spec, br_spec, br_spec,
                  _full((d, N_BRANCH * d)), _full((N_BRANCH, INNER, d)), _full((d, d))],
        out_specs=pl.BlockSpec((1, tm, d), lambda b, i: (b, i, 0)),
        compiler_params=_params(),
        name="merge",
    )(h, mod, gain, *branches, w_gate, w_branch, w_out)


def _ssd_body(*refs, t, has_init, emit_state):
    (z_ref, xs_ref, bc_ref, dt_ref, dtT_ref, cwx_ref, cbx_ref, cwbc_ref, cbbc_ref,
     dtb_row_ref, dtb_col_ref, alog_row_ref, alog_col_ref, dskip_ref, gnorm_ref) = refs[:15]
    pos = 15
    h0_ref = None
    if has_init:
        h0_ref = refs[pos]
        pos += 1
    o_ref = refs[pos]
    pos += 1
    st_ref = None
    if emit_state:
        st_ref = refs[pos]
        pos += 1
    y_sc, cum_sc, cm_sc, dh_sc, tot_sc, hst_sc = refs[pos:]

    cl = SSD_CHUNK
    nc = t // cl
    lane = _iota((cl, PAIR), 1)
    ii = _iota((cl, cl), 0)
    jj = _iota((cl, cl), 1)
    lo = _lo_tri(cl)
    up = _up_tri(cl)
    hp = lax.Precision.HIGHEST
    a_row = -jnp.exp(alog_row_ref[...])
    a_col = -jnp.exp(alog_col_ref[...])
    sel_f = _expand_heads(PAIR, 0, INNER)
    sel_b = _expand_heads(PAIR, N_HEADS, INNER)

    def intra(c, carry):
        r0 = pl.multiple_of(c * cl, cl)
        xs = _silu(_dwconv(_conv_window(xs_ref, r0, c, nc, cl, t), cwx_ref[...], cbx_ref[...], cl))
        bc = _silu(_dwconv(_conv_window(bc_ref, r0, c, nc, cl, t), cwbc_ref[...], cbbc_ref[...], cl))
        bm = bc[:, :PAIR]
        cm = bc[:, PAIR:]
        dt = _softplus(dt_ref[0, pl.ds(r0, cl), :] + dtb_row_ref[...])
        dtT = _softplus(dtT_ref[0, :, pl.ds(r0, cl)] + dtb_col_ref[...])
        da = dt * a_row
        daT = dtT * a_col
        cum = jnp.where(lane < N_HEADS, _sel_dot(lo, da), _sel_dot(up, da))
        rowsel = _iota((2 * N_HEADS, cl), 0) < N_HEADS
        cumT = jnp.where(rowsel, _dot_sel(daT, up), _dot_sel(daT, lo))
        cum2 = cum * LOG2E
        cumT2 = cumT * LOG2E
        cum_sc[pl.ds(r0, cl), :] = cum
        cm_sc[pl.ds(r0, cl), :] = cm
        cb = []
        for g in range(SSM_GROUPS):
            cg = jnp.where(lane // HEAD == g, cm, 0.0)
            cb.append(_dot_nt(cg, bm))
        ypairs = []
        for p in range(N_PAIRS):
            xp = xs[:, p * PAIR:(p + 1) * PAIR].astype(BF16)
            halves = []
            for a in range(2):
                h = 2 * p + a
                g = h // (N_HEADS // SSM_GROUPS)
                hb = N_HEADS + h
                expo = jnp.where(jj <= ii, cum2[:, h:h + 1] - cumT2[h:h + 1, :], cum2[:, hb:hb + 1] - cumT2[hb:hb + 1, :])
                dtf = dtT[h:h + 1, :]
                dtb = dtT[hb:hb + 1, :]
                wdt = jnp.where(jj < ii, dtf, jnp.where(jj > ii, dtb, dtf + dtb))
                s = (cb[g] * jnp.exp2(expo) * wdt).astype(BF16)
                halves.append(jnp.dot(s, xp, preferred_element_type=F32))
            ypairs.append(jnp.where(lane < HEAD, halves[0], halves[1]))
        y = jnp.concatenate(ypairs, axis=1) + dskip_ref[...] * xs
        y_sc[pl.ds(r0, cl), :] = y
        tot = jnp.where(lane[0:1] < N_HEADS, cum[cl - 1:cl, :], cum[0:1, :])
        tot_sc[c] = jnp.broadcast_to(tot, (HALO, PAIR))
        wexp = jnp.exp(tot - cum) * dt
        for d, sel in enumerate((sel_f, sel_b)):
            xw = xs * _dot_sel(wexp, sel)
            for p in range(N_PAIRS):
                g = p // (N_PAIRS // SSM_GROUPS)
                bg = jnp.where(lane // HEAD == g, bm, 0.0)
                dh_sc[c, d, p] = _dot_tn(xw[:, p * PAIR:(p + 1) * PAIR], bg)
        return carry

    lax.fori_loop(0, nc, intra, 0)

    rr = _iota((PAIR, PAIR), 0)
    for d in range(2):
        for p in range(N_PAIRS):
            g = p // (N_PAIRS // SSM_GROUPS)
            if has_init:
                blk = h0_ref[0, d, p * PAIR:(p + 1) * PAIR, :]
                z64 = jnp.zeros_like(blk)
                hst_sc[d, p] = jnp.concatenate([blk, z64] if g == 0 else [z64, blk], axis=1)
            else:
                hst_sc[d, p] = jnp.zeros((PAIR, PAIR), F32)

    def inter(k, carry):
        for d in range(2):
            sel = sel_f if d == 0 else sel_b
            c = k if d == 0 else nc - 1 - k
            r0 = pl.multiple_of(c * cl, cl)
            ecum = _dot_sel(jnp.exp(cum_sc[pl.ds(r0, cl), :]), sel)
            cm = cm_sc[pl.ds(r0, cl), :]
            tot = tot_sc[c]
            dec = jnp.exp(tot[0:1, :])
            for p in range(N_PAIRS):
                g = p // (N_PAIRS // SSM_GROUPS)
                cg = jnp.where(lane // HEAD == g, cm, 0.0)
                hs = hst_sc[d, p]
                yi = _dot_nt(cg, hs)
                cols = pl.ds(p * PAIR, PAIR)
                y_sc[pl.ds(r0, cl), cols] = y_sc[pl.ds(r0, cl), cols] + yi * ecum[:, p * PAIR:(p + 1) * PAIR]
                h0i = d * N_HEADS + 2 * p
                dcol = jnp.where(rr < HEAD, dec[:, h0i:h0i + 1], dec[:, h0i + 1:h0i + 2])
                hst_sc[d, p] = hs * dcol + dh_sc[c, d, p]
        return carry

    lax.fori_loop(0, nc, inter, 0)
    if emit_state:
        for d in range(2):
            for p in range(N_PAIRS):
                g = p // (N_PAIRS // SSM_GROUPS)
                st_ref[0, d, p * PAIR:(p + 1) * PAIR, :] = hst_sc[d, p][:, g * HEAD:(g + 1) * HEAD]

    def finish(c, carry):
        r0 = pl.multiple_of(c * cl, cl)
        v = y_sc[pl.ds(r0, cl), :] * _silu(z_ref[0, pl.ds(r0, cl), :])
        o_ref[0, pl.ds(r0, cl), :] = _rms(v, gnorm_ref[...]).astype(BF16)
        return carry

    lax.fori_loop(0, nc, finish, 0)


def _ssd(proj, projT, prm, h0, *, emit_state):
    bsz, t, _ = proj.shape
    nc = t // SSD_CHUNK
    has_init = h0 is not None
    seq = lambda w, j: pl.BlockSpec((1, t, w), lambda b: (b, 0, j))
    in_specs = [seq(INNER, COL_Z), seq(INNER, COL_XS), seq(2 * PAIR, COL_BC), seq(PAIR, COL_DT),
                pl.BlockSpec((1, 2 * N_HEADS, t), lambda b: (b, ROWT_DT, 0))]
    args = [proj, proj, proj, proj, projT]
    for name in ("cwx", "cbx", "cwbc", "cbbc", "dtb_row", "dtb_col", "alog_row", "alog_col", "dskip", "gnorm"):
        a = prm[name]
        in_specs.append(_full(a.shape))
        args.append(a)
    if has_init:
        in_specs.append(pl.BlockSpec((1, 2, INNER, HEAD), lambda b: (b, 0, 0, 0)))
        args.append(h0)
    out_shape = [jax.ShapeDtypeStruct((bsz, t, INNER), BF16)]
    out_specs = [pl.BlockSpec((1, t, INNER), lambda b: (b, 0, 0))]
    if emit_state:
        out_shape.append(jax.ShapeDtypeStruct((bsz, 2, INNER, HEAD), F32))
        out_specs.append(pl.BlockSpec((1, 2, INNER, HEAD), lambda b: (b, 0, 0, 0)))
    res = pl.pallas_call(
        functools.partial(_ssd_body, t=t, has_init=has_init, emit_state=emit_state),
        out_shape=out_shape,
        grid=(bsz,),
        in_specs=in_specs,
        out_specs=out_specs,
        scratch_shapes=[pltpu.VMEM((t, INNER), F32), pltpu.VMEM((t, PAIR), F32), pltpu.VMEM((t, PAIR), F32),
                        pltpu.VMEM((nc, 2, N_PAIRS, PAIR, PAIR), F32), pltpu.VMEM((nc, HALO, PAIR), F32),
                        pltpu.VMEM((2, N_PAIRS, PAIR, PAIR), F32)],
        compiler_params=_params(),
        name="ssd",
    )(*args)
    return (res[0], res[1]) if emit_state else (res[0], None)


def _attn_body(*refs, t, s_ctx, rope):
    q_ref, kk_ref, vv_ref, qg_ref, kg_ref = refs[:5]
    pos = 5
    if rope:
        cos_ref, sin_ref, ck_ref, cv_ref = refs[pos:pos + 4]
        pos += 4
    o_ref = refs[pos]
    pos += 1
    if not rope:
        knew_ref, vnew_ref = refs[pos:pos + 2]
        pos += 2
    q_sc, k_sc, v_sc, s_sc, e_sc, l_sc, m_sc = refs[pos:]

    rc = ROW_CHUNK
    lane = _iota((rc, PAIR), 1)
    first_half = (lane % (HEAD // 2)) < HEAD // 4

    def rot(x, cos, sin):
        partner = jnp.where(first_half, pltpu.roll(x, PAIR - HEAD // 4, 1), pltpu.roll(x, HEAD // 4, 1))
        return x * cos + partner * sin

    def normed(x, gain):
        return x * lax.rsqrt(_head_sumsq(x) * (1.0 / HEAD) + EPS) * gain

    if rope:
        k_sc[0:s_ctx, :] = ck_ref[0].astype(BF16)
        v_sc[0:s_ctx, :] = cv_ref[0].astype(BF16)

    def prep(c, carry):
        r0 = pl.multiple_of(c * rc, rc)
        if rope:
            cos = cos_ref[pl.ds(r0, rc), :]
            sin = sin_ref[pl.ds(r0, rc), :]
        for p in range(N_PAIRS):
            x = normed(q_ref[0, pl.ds(r0, rc), p * PAIR:(p + 1) * PAIR], qg_ref[...])
            if rope:
                x = rot(x, cos, sin)
            q_sc[pl.ds(r0, rc), p * PAIR:(p + 1) * PAIR] = (x * (HEAD ** -0.5 * LOG2E)).astype(BF16)
        kn = []
        for g in range(KV_HEADS):
            x = normed(kk_ref[0, pl.ds(r0, rc), g * PAIR:(g + 1) * PAIR], kg_ref[...])
            kn.append(x)
            if rope:
                x = rot(x, cos, sin)
            k_sc[pl.ds(s_ctx + r0, rc), g * PAIR:(g + 1) * PAIR] = x.astype(BF16)
        vv = vv_ref[0, pl.ds(r0, rc), :]
        v_sc[pl.ds(s_ctx + r0, rc), :] = vv.astype(BF16)
        if not rope:
            knew_ref[0, pl.ds(r0, rc), :] = jnp.where(lane < HEAD, kn[0], kn[1])
            vnew_ref[0, pl.ds(r0, rc), :] = jnp.where(lane < HEAD, vv[:, :PAIR], vv[:, PAIR:])
        return carry

    lax.fori_loop(0, t // rc, prep, 0)

    tq = ATT_TQ
    lane_q = _iota((tq, PAIR), 1)

    pairs_per_group = N_PAIRS // KV_HEADS

    n_tiles = t // tq

    def scores(i, g, dst, m_dst):
        r0 = i * tq if isinstance(i, int) else pl.multiple_of(i * tq, tq)
        tiles = []
        for p in range(g * pairs_per_group, (g + 1) * pairs_per_group):
            qp = q_sc[pl.ds(r0, tq), p * PAIR:(p + 1) * PAIR]
            for a in range(2):
                tiles.append(jnp.where(lane_q // HEAD == a, qp, jnp.zeros_like(qp)))
        qs = jnp.concatenate(tiles, axis=0)
        s = lax.dot_general(qs, k_sc[:, g * PAIR:(g + 1) * PAIR], (((1,), (1,)), ((), ())),
                            preferred_element_type=F32)
        dst[...] = s
        m_dst[...] = jnp.broadcast_to(jnp.max(s, axis=-1, keepdims=True), m_dst.shape)

    def softmax(src, m_src, e_dst, l_dst):
        rows_u, s_all = src.shape
        rb = ATT_SM_ROWS
        for r in range(rows_u // rb):
            rows = slice(r * rb, (r + 1) * rb)
            mb = m_src[rows, :]
            acc = None
            for c in range(s_all // PAIR):
                e = jnp.exp2(src[rows, c * PAIR:(c + 1) * PAIR] - mb)
                acc = e if acc is None else acc + e
                e_dst[rows, c * PAIR:(c + 1) * PAIR] = e.astype(BF16)
            l_dst[rows, :] = jnp.broadcast_to(jnp.sum(acc, axis=-1, keepdims=True), (rb, PAIR))

    def values(i, g, e_src, l_src):
        r0 = i * tq if isinstance(i, int) else pl.multiple_of(i * tq, tq)
        o = jnp.dot(e_src[...], v_sc[:, g * PAIR:(g + 1) * PAIR], preferred_element_type=F32) / l_src[...]
        for j in range(pairs_per_group):
            p = g * pairs_per_group + j
            pair = jnp.where(lane_q < HEAD, o[(2 * j) * tq:(2 * j + 1) * tq], o[(2 * j + 1) * tq:(2 * j + 2) * tq])
            o_ref[0, pl.ds(r0, tq), p * PAIR:(p + 1) * PAIR] = pair.astype(BF16)

    def step(i, slot):
        for g in range(KV_HEADS):
            scores(jnp.minimum(i + 1, n_tiles - 1), g, s_sc.at[1 - slot, g], m_sc.at[1 - slot, g])
            softmax(s_sc.at[slot, g], m_sc.at[slot, g], e_sc.at[slot, g], l_sc.at[slot, g])
            values(jnp.maximum(i - 1, 0), g, e_sc.at[1 - slot, g], l_sc.at[1 - slot, g])

    for g in range(KV_HEADS):
        scores(0, g, s_sc.at[0, g], m_sc.at[0, g])
        e_sc[1, g] = jnp.zeros(e_sc.shape[2:], BF16)
        l_sc[1, g] = jnp.ones(l_sc.shape[2:], F32)

    def attend(j, carry):
        step(2 * j, 0)
        step(2 * j + 1, 1)
        return carry

    assert n_tiles % 2 == 0
    lax.fori_loop(0, n_tiles // 2, attend, 0)
    for g in range(KV_HEADS):
        values(n_tiles - 1, g, e_sc.at[1, g], l_sc.at[1, g])


def _attn(proj, prm, rope_tabs, ctx_kv):
    bsz, t, _ = proj.shape
    rope = rope_tabs is not None
    s_ctx = ctx_kv[0].shape[1] if rope else 0
    s_all = s_ctx + t
    rows_u = (N_HEADS // KV_HEADS) * ATT_TQ
    seq = lambda w, j: pl.BlockSpec((1, t, w), lambda b: (b, 0, j))
    in_specs = [seq(INNER, COL_QB), seq(2 * PAIR, COL_KK), seq(2 * PAIR, COL_VV), _full((1, PAIR)), _full((1, PAIR))]
    args = [proj, proj, proj, prm["qgain"], prm["kgain"]]
    if rope:
        in_specs += [_full((t, PAIR)), _full((t, PAIR)),
                     pl.BlockSpec((1, s_ctx, 2 * PAIR), lambda b: (b, 0, 0)),
                     pl.BlockSpec((1, s_ctx, 2 * PAIR), lambda b: (b, 0, 0))]
        args += [rope_tabs[0], rope_tabs[1], ctx_kv[0], ctx_kv[1]]
    out_shape = [jax.ShapeDtypeStruct((bsz, t, INNER), BF16)]
    out_specs = [pl.BlockSpec((1, t, INNER), lambda b: (b, 0, 0))]
    if not rope:
        out_shape += [jax.ShapeDtypeStruct((bsz, t, PAIR), F32)] * 2
        out_specs += [pl.BlockSpec((1, t, PAIR), lambda b: (b, 0, 0))] * 2
    res = pl.pallas_call(
        functools.partial(_attn_body, t=t, s_ctx=s_ctx, rope=rope),
        out_shape=out_shape,
        grid=(bsz,),
        in_specs=in_specs,
        out_specs=out_specs,
        scratch_shapes=[pltpu.VMEM((t, INNER), BF16), pltpu.VMEM((s_all, 2 * PAIR), BF16),
                        pltpu.VMEM((s_all, 2 * PAIR), BF16),
                        pltpu.VMEM((2, KV_HEADS, rows_u, s_all), F32), pltpu.VMEM((2, KV_HEADS, rows_u, s_all), BF16),
                        pltpu.VMEM((2, KV_HEADS, rows_u, PAIR), F32), pltpu.VMEM((2, KV_HEADS, rows_u, PAIR), F32)],
        compiler_params=_params(),
        name="attn",
    )(*args)
    return res


def _dn_body(*refs, t, has_init, emit_state):
    (q_ref, k_ref, v_ref, gate_ref, ba_ref, baT_ref, cwq_ref, cwk_ref, cwv_ref, cbq_ref, cbk_ref, cbv_ref,
     alog_row_ref, bias_row_ref, alog_col_ref, bias_col_ref, gnorm_ref) = refs[:17]
    pos = 17
    s0_ref = None
    if has_init:
        s0_ref = refs[pos]
        pos += 1
    o_ref = refs[pos]
    pos += 1
    st_ref = None
    if emit_state:
        st_ref = refs[pos]
        pos += 1
    qn_sc, kn_sc, vc_sc, bg_sc, bgT_sc, o_sc, s_sc, hu_sc, hb_sc, hg_sc = refs[pos:]

    rc = ROW_CHUNK
    cl = DN_CHUNK
    nc = t // cl
    nrc = t // rc
    hp = lax.Precision.HIGHEST
    nb = 2 * N_HEADS

    lane_rc = _iota((rc, PAIR), 1)
    row_t = _iota((2 * nb, rc), 0)

    def prep(c, carry):
        r0 = pl.multiple_of(c * rc, rc)
        for src, cw, cb, dst, norm in ((q_ref, cwq_ref, cbq_ref, qn_sc, True), (k_ref, cwk_ref, cbk_ref, kn_sc, True),
                                       (v_ref, cwv_ref, cbv_ref, vc_sc, False)):
            x = _silu(_dwconv(_conv_window(src, r0, c, nrc, rc, t), cw[...], cb[...], rc))
            if norm:
                tiles = []
                for p in range(N_PAIRS):
                    xp = x[:, p * PAIR:(p + 1) * PAIR]
                    tiles.append(xp * lax.rsqrt(_head_sumsq(xp) + EPS))
                x = jnp.concatenate(tiles, axis=1)
                if dst is qn_sc:
                    x = x * (HEAD ** -0.5)
            dst[pl.ds(r0, rc), :] = x
        o_sc[pl.ds(r0, rc), :] = jnp.zeros((rc, INNER), F32)
        ba = ba_ref[0, pl.ds(r0, rc), :]
        beta = jax.nn.sigmoid(ba)
        gdec = -jnp.exp(alog_row_ref[...]) * _softplus(ba + bias_row_ref[...])
        bg_sc[pl.ds(r0, rc), :] = jnp.where(lane_rc < nb, beta, gdec)
        baT = baT_ref[0, :, pl.ds(r0, rc)]
        betaT = jax.nn.sigmoid(baT)
        gdecT = -jnp.exp(alog_col_ref[...]) * _softplus(baT + bias_col_ref[...])
        bgT = jnp.where(row_t < nb, betaT, gdecT)
        for k in range(rc // cl):
            bgT_sc[c * (rc // cl) + k] = bgT[:, k * cl:(k + 1) * cl]
        return carry

    lax.fori_loop(0, nrc, prep, 0)

    rr = _iota((PAIR, PAIR), 0)
    cc = _iota((PAIR, PAIR), 1)
    same = (rr // HEAD) == (cc // HEAD)
    lane_c = _iota((cl, PAIR), 1)
    ii = _iota((cl, PAIR), 0)
    jj = lane_c % HEAD
    first = lane_c < HEAD
    lo = _lo_tri(cl)
    up = _up_tri(cl)
    eye = (ii == jj).astype(F32)
    offdiag = (ii != jj).astype(F32)
    blk_base = ii // DN_BASE == jj // DN_BASE
    off_masks = []
    b = DN_BASE
    while b < cl:
        off_masks.append((ii // (2 * b) == jj // (2 * b)) & (ii // b != jj // b))
        b *= 2

    for d in range(2):
        for p in range(N_PAIRS):
            if has_init:
                b0 = s0_ref[0, d, (2 * p) * HEAD:(2 * p + 1) * HEAD, :]
                b1 = s0_ref[0, d, (2 * p + 1) * HEAD:(2 * p + 2) * HEAD, :]
                z64 = jnp.zeros_like(b0)
                s_sc[d, p] = jnp.concatenate([jnp.concatenate([b0, z64], axis=1),
                                              jnp.concatenate([z64, b1], axis=1)], axis=0)
            else:
                s_sc[d, p] = jnp.zeros((PAIR, PAIR), F32)

    def bdiag(x):
        xb = x.astype(BF16)
        zero = jnp.zeros_like(xb)
        return jnp.concatenate([jnp.where(first, xb, zero), jnp.where(first, zero, xb)], axis=0)

    def pdot(x, y_bd):
        return jnp.dot(x.astype(BF16), y_bd, preferred_element_type=F32)

    def cols2(m, c0):
        return jnp.where(first, m[:, c0:c0 + 1], m[:, c0 + 1:c0 + 2])

    group = min(DN_GROUP, nc)
    systems =[(d, j, p) for j in range(group) for d in range(2) for p in range(N_PAIRS)]

    def chunk_row0(kg, d, j):
        c = kg * group + j if d == 0 else nc - 1 - (kg * group + j)
        return c, (c * cl if isinstance(c, int) else pl.multiple_of(c * cl, cl))

    def solve(kg):
        dirs = {}
        for d in range(2):
            for j in range(group):
                c, r0 = chunk_row0(kg, d, j)
                bg = bg_sc[pl.ds(r0, cl), :]
                bgT = bgT_sc[c]
                tri = lo if d == 0 else up
                triT = up if d == 0 else lo
                gc = _sel_dot(tri, bg)
                gcT = _dot_sel(bgT, triT)
                dirs[d, j] = (r0, bg, gc, gcT)
        st = []
        for d, j, p in systems:
            r0, bg, gc, gcT = dirs[d, j]
            last = cl - 1 if d == 0 else 0
            incl = (jj <= ii) if d == 0 else (jj >= ii)
            h0i = d * N_HEADS + 2 * p
            beta = cols2(bg, h0i)
            gcm = cols2(gc, nb + h0i)
            rgc = jnp.concatenate([gcT[nb + h0i:nb + h0i + 1, :], gcT[nb + h0i + 1:nb + h0i + 2, :]], axis=1)
            glast = jnp.where(first[0:1], gc[last:last + 1, nb + h0i:nb + h0i + 1],
                              gc[last:last + 1, nb + h0i + 1:nb + h0i + 2])
            decay = jnp.exp(jnp.where(incl, gcm - rgc, -jnp.inf))
            cols = pl.ds(p * PAIR, PAIR)
            kp = kn_sc[pl.ds(r0, cl), cols]
            qp = qn_sc[pl.ds(r0, cl), cols]
            vp = vc_sc[pl.ds(r0, cl), cols]
            egc = jnp.exp(gcm)
            kb = kp * beta
            st.append(dict(glast=glast, decay=decay, kp=kp, kb=kb, qp=qp, qg=qp * egc,
                           kd=kp * jnp.exp(glast - gcm), rhs_u=vp * beta, rhs_w=kb * egc))
        yield
        for e in st:
            k_bd = bdiag(e["kp"])
            kq = jnp.concatenate([e["kb"], e["qp"]], axis=0).astype(BF16)
            ga = lax.dot_general(kq, k_bd, (((1,), (1,)), ((), ())), preferred_element_type=F32)
            e["gm"] = ga[:cl]
            e["am"] = ga[cl:]
        yield
        for e in st:
            e["m"] = e["gm"] * e["decay"] * offdiag
            e["aq"] = e["am"] * e["decay"]
            e["n1"] = jnp.where(blk_base, e["m"], 0.0)
        for e in st:
            e["n2"] = pdot(e["n1"], bdiag(e["n1"]))
        yield
        for e in st:
            e["n2_bd"] = bdiag(e["n2"])
            e["n4"] = pdot(e["n2"], e["n2_bd"])
        yield
        for e in st:
            e["pm"] = eye + e["n2"] + e["n4"] + pdot(e["n4"], e["n2_bd"])
        yield
        for e in st:
            e["x"] = e["pm"] - pdot(e["n1"], bdiag(e["pm"]))
        yield
        for off_mask in off_masks:
            for e in st:
                e["cx"] = pdot(jnp.where(off_mask, e["m"], 0.0), bdiag(e["x"]))
            yield
            for e in st:
                e["x"] = e["x"] - pdot(e["x"], bdiag(e["cx"]))
            yield
        for e in st:
            rhs_bd = jnp.concatenate([bdiag(e["rhs_u"]), bdiag(e["rhs_w"])], axis=1)
            e["sol"] = pdot(e["x"], rhs_bd)
        yield
        for idx, e in enumerate(st):
            hu_sc[idx] = e["sol"][:, :PAIR]
            hb_sc[idx, 0] = e["sol"][:, PAIR:].astype(BF16)
            hb_sc[idx, 1] = e["qg"].astype(BF16)
            hb_sc[idx, 2] = e["aq"].astype(BF16)
            hb_sc[idx, 3] = e["kd"].astype(BF16)
            hg_sc[idx] = jnp.broadcast_to(e["glast"], (HALO, PAIR))

    def recur(kg):
        state = {(d, p): s_sc[d, p] for d in range(2) for p in range(N_PAIRS)}
        for j in range(group):
            cur = [(idx, d, p) for idx, (d, jj_, p) in enumerate(systems) if jj_ == j]
            tmp = {}
            for idx, d, p in cur:
                wq = jnp.dot(jnp.concatenate([hb_sc[idx, 0], hb_sc[idx, 1]], axis=0), state[d, p].astype(BF16),
                             preferred_element_type=F32)
                tmp[idx] = (hu_sc[idx] - wq[:cl], wq[cl:])
            yield
            for idx, d, p in cur:
                vnew, qs = tmp[idx]
                _, r0 = chunk_row0(kg, d, j)
                o_sc[pl.ds(r0, cl), pl.ds(p * PAIR, PAIR)] += qs + jnp.dot(hb_sc[idx, 2], bdiag(vnew),
                                                                           preferred_element_type=F32)
                upd = jnp.where(same, _dot_tn(hb_sc[idx, 3], vnew), 0.0)
                glast = hg_sc[idx][0:1, :]
                gl_rows = jnp.where(rr[:, 0:1] < HEAD, glast[:, 0:1], glast[:, HEAD:HEAD + 1])
                state[d, p] = state[d, p] * jnp.exp(gl_rows) + upd
            yield
        for (d, p), s in state.items():
            s_sc[d, p] = s

    def run(*gens):
        gens = list(gens)
        while gens:
            for g in list(gens):
                try:
                    next(g)
                except StopIteration:
                    gens.remove(g)

    n_groups = nc // group
    run(solve(0))

    def chunk(k, carry):
        run(recur(k), solve(k + 1))
        return carry

    lax.fori_loop(0, n_groups - 1, chunk, 0)
    run(recur(n_groups - 1))

    if emit_state:
        for d in range(2):
            for p in range(N_PAIRS):
                s = s_sc[d, p]
                st_ref[0, d, (2 * p) * HEAD:(2 * p + 1) * HEAD, :] = s[:HEAD, :HEAD]
                st_ref[0, d, (2 * p + 1) * HEAD:(2 * p + 2) * HEAD, :] = s[HEAD:, HEAD:]

    def finish(c, carry):
        r0 = pl.multiple_of(c * rc, rc)
        for p in range(N_PAIRS):
            cols = pl.ds(p * PAIR, PAIR)
            o = o_sc[pl.ds(r0, rc), cols]
            on = o * lax.rsqrt(_head_sumsq(o) * (1.0 / HEAD) + EPS) * gnorm_ref[...]
            o_ref[0, pl.ds(r0, rc), cols] = (on * _silu(gate_ref[0, pl.ds(r0, rc), cols])).astype(BF16)
        return carry

    lax.fori_loop(0, nrc, finish, 0)


def _dn(proj, projT, prm, s0, *, emit_state):
    bsz, t, _ = proj.shape
    nc = t // DN_CHUNK
    n_sys = min(DN_GROUP, nc) * 2 * N_PAIRS
    has_init = s0 is not None
    seq = lambda w, j: pl.BlockSpec((1, t, w), lambda b: (b, 0, j), pipeline_mode=pl.Buffered(1))
    in_specs = [seq(INNER, COL_QC), seq(INNER, COL_KC), seq(INNER, COL_VC), seq(INNER, COL_GC), seq(PAIR, COL_BA),
                pl.BlockSpec((1, 4 * N_HEADS, t), lambda b: (b, ROWT_BA, 0), pipeline_mode=pl.Buffered(1))]
    args = [proj] * 5 + [projT]
    for name in ("cwq", "cwk", "cwv", "cbq", "cbk", "cbv", "alog_row", "bias_row", "alog_col", "bias_col", "gnorm"):
        a = prm[name]
        in_specs.append(_full(a.shape))
        args.append(a)
    if has_init:
        in_specs.append(pl.BlockSpec((1, 2, INNER, HEAD), lambda b: (b, 0, 0, 0)))
        args.append(s0)
    out_shape = [jax.ShapeDtypeStruct((bsz, t, INNER), BF16)]
    out_specs = [pl.BlockSpec((1, t, INNER), lambda b: (b, 0, 0))]
    if emit_state:
        out_shape.append(jax.ShapeDtypeStruct((bsz, 2, INNER, HEAD), F32))
        out_specs.append(pl.BlockSpec((1, 2, INNER, HEAD), lambda b: (b, 0, 0, 0)))
    res = pl.pallas_call(
        functools.partial(_dn_body, t=t, has_init=has_init, emit_state=emit_state),
        out_shape=out_shape,
        grid=(bsz,),
        in_specs=in_specs,
        out_specs=out_specs,
        scratch_shapes=[pltpu.VMEM((t, INNER), F32), pltpu.VMEM((t, INNER), F32), pltpu.VMEM((t, INNER), F32),
                        pltpu.VMEM((t, PAIR), F32), pltpu.VMEM((nc, 4 * N_HEADS, DN_CHUNK), F32),
                        pltpu.VMEM((t, INNER), F32),
                        pltpu.VMEM((2, N_PAIRS, PAIR, PAIR), F32),
                        pltpu.VMEM((n_sys, DN_CHUNK, PAIR), F32), pltpu.VMEM((n_sys, 4, DN_CHUNK, PAIR), BF16),
                        pltpu.VMEM((n_sys, HALO, PAIR), F32)],
        compiler_params=_params(),
        name="deltanet",
    )(*args)
    return (res[0], res[1]) if emit_state else (res[0], None)


def _lru_body(*refs, t, has_init, emit_state):
    x_ref, y_ref, cw_ref, cb_ref, wbd_ref, bias_ref, lam_ref = refs[:7]
    pos = 7
    h0_ref = None
    if has_init:
        h0_ref = refs[pos]
        pos += 1
    o_ref = refs[pos]
    pos += 1
    st_ref = None
    if emit_state:
        st_ref = refs[pos]
        pos += 1
    af_sc, uf_sc, ab_sc, ub_sc = refs[pos:]

    rc = ROW_CHUNK
    nrc = t // rc
    row8 = _iota((HALO, INNER), 0)
    sp_lam = _softplus(-lam_ref[...])

    def gates(c, carry):
        r0 = pl.multiple_of(c * rc, rc)
        xl_all = _dwconv(_conv_window(x_ref, r0, c, nrc, rc, t), cw_ref[...], cb_ref[...], rc)
        for p in range(N_PAIRS):
            cols = pl.ds(p * PAIR, PAIR)
            xl = xl_all[:, p * PAIR:(p + 1) * PAIR]
            pre = _dot(xl, wbd_ref[p]) + bias_ref[p]
            for d, (a_sc, u_sc) in enumerate(((af_sc, uf_sc), (ab_sc, ub_sc))):
                r = jax.nn.sigmoid(pre[:, d * PAIR:(d + 1) * PAIR])
                ig = jax.nn.sigmoid(pre[:, (2 + d) * PAIR:(3 + d) * PAIR])
                log_a = -LRU_C * r * sp_lam[d:d + 1, p * PAIR:(p + 1) * PAIR]
                a = jnp.exp(log_a)
                a_sc[pl.ds(r0, rc), cols] = a
                u_sc[pl.ds(r0, rc), cols] = jnp.sqrt((1.0 - a) * (1.0 + a)) * ig * xl
        return carry

    lax.fori_loop(0, nrc, gates, 0)

    if has_init:
        cf0 = h0_ref[0, 0:1, :]
        cb0 = h0_ref[0, 1:2, :]
    else:
        cf0 = jnp.zeros((1, INNER), F32)
        cb0 = cf0

    def scan(k, carry):
        cf, cb = carry
        rf = pl.multiple_of(k * HALO, HALO)
        a8 = af_sc[pl.ds(rf, HALO), :]
        b8 = uf_sc[pl.ds(rf, HALO), :]
        for s in (1, 2, 4):
            ok = row8 >= s
            b8 = jnp.where(ok, a8 * pltpu.roll(b8, s, 0) + b8, b8)
            a8 = jnp.where(ok, a8 * pltpu.roll(a8, s, 0), a8)
        hf = a8 * cf + b8
        uf_sc[pl.ds(rf, HALO), :] = hf
        rb = pl.multiple_of(t - HALO - k * HALO, HALO)
        a8 = ab_sc[pl.ds(rb, HALO), :]
        b8 = ub_sc[pl.ds(rb, HALO), :]
        for s in (1, 2, 4):
            ok = row8 < HALO - s
            b8 = jnp.where(ok, a8 * pltpu.roll(b8, HALO - s, 0) + b8, b8)
            a8 = jnp.where(ok, a8 * pltpu.roll(a8, HALO - s, 0), a8)
        hb = a8 * cb + b8
        ub_sc[pl.ds(rb, HALO), :] = hb
        return hf[HALO - 1:HALO, :], hb[0:1, :]

    cf, cb = lax.fori_loop(0, t // HALO, scan, (cf0, cb0))
    if emit_state:
        st_ref[0, 0:1, :] = cf
        st_ref[0, 1:2, :] = cb

    def finish(c, carry):
        r0 = pl.multiple_of(c * rc, rc)
        y = y_ref[0, pl.ds(r0, rc), :]
        gelu = 0.5 * y * (1.0 + jnp.tanh(math.sqrt(2.0 / math.pi) * (y + 0.044715 * (y * y * y))))
        o_ref[0, pl.ds(r0, rc), :] = ((uf_sc[pl.ds(r0, rc), :] + ub_sc[pl.ds(r0, rc), :]) * gelu).astype(BF16)
        return carry

    lax.fori_loop(0, nrc, finish, 0)


def _lru(proj, prm, h0, *, emit_state):
    bsz, t, _ = proj.shape
    has_init = h0 is not None
    seq = lambda w, j: pl.BlockSpec((1, t, w), lambda b: (b, 0, j))
    in_specs = [seq(INNER, COL_XD), seq(INNER, COL_YD)]
    args = [proj, proj]
    for name in ("cw", "cb", "wbd", "bias", "lam"):
        a = prm[name]
        in_specs.append(_full(a.shape))
        args.append(a)
    if has_init:
        in_specs.append(pl.BlockSpec((1, 2, INNER), lambda b: (b, 0, 0)))
        args.append(h0)
    out_shape = [jax.ShapeDtypeStruct((bsz, t, INNER), BF16)]
    out_specs = [pl.BlockSpec((1, t, INNER), lambda b: (b, 0, 0))]
    if emit_state:
        out_shape.append(jax.ShapeDtypeStruct((bsz, 2, INNER), F32))
        out_specs.append(pl.BlockSpec((1, 2, INNER), lambda b: (b, 0, 0)))
    res = pl.pallas_call(
        functools.partial(_lru_body, t=t, has_init=has_init, emit_state=emit_state),
        out_shape=out_shape,
        grid=(bsz,),
        in_specs=in_specs,
        out_specs=out_specs,
        scratch_shapes=[pltpu.VMEM((t, INNER), F32)] * 4,
        compiler_params=_params(),
        name="rglru",
    )(*args)
    return (res[0], res[1]) if emit_state else (res[0], None)


def _pad_cols(x, n):
    return jnp.pad(x, ((0, 0), (0, n - x.shape[1])))


def _layer_params(l, w_in, ssm_conv_w, ssm_conv_b, ssm_a_log, ssm_dt_bias, ssm_d, ssm_norm, attn_q_norm, attn_k_norm,
                  dn_conv_w, dn_conv_b, dn_a_log, dn_dt_bias, dn_norm, lru_conv_w, lru_conv_b, lru_w_a, lru_b_a,
                  lru_w_i, lru_b_i, lru_lambda):
    w = w_in[l].astype(BF16)
    o = 0

    def take(n):
        nonlocal o
        part = w[:, o:o + n]
        o += n
        return part

    z_a, xs_a, bc_a, dt_a = take(INNER), take(INNER), take(2 * PAIR), take(2 * N_HEADS)
    q_b, k_b, v_b = take(INNER), take(PAIR), take(PAIR)
    q_c, k_c, v_c = take(INNER), take(INNER), take(INNER)
    beta_c, a_c, gate_c = take(2 * N_HEADS), take(2 * N_HEADS), take(INNER)
    x_d, y_d = take(INNER), take(INNER)
    gate_raw = take(N_BRANCH * D_MODEL)

    def dup(x):
        return jnp.concatenate([x[:, :HEAD], x[:, :HEAD], x[:, HEAD:], x[:, HEAD:]], axis=1)

    nh2 = 2 * N_HEADS
    ba_c = jnp.concatenate([beta_c, a_c], axis=1)
    w_proj = jnp.concatenate([z_a, xs_a, q_b, q_c, k_c, v_c, gate_c, x_d, y_d, bc_a, dup(k_b), dup(v_b),
                              _pad_cols(dt_a, PAIR), _pad_cols(ba_c, PAIR)], axis=1)
    assert w_proj.shape[1] == PROJ_N
    prm = {
        "w_proj": w_proj.astype(BF16),
        "w_projT": jnp.concatenate([ba_c, dt_a], axis=1).T.astype(BF16),
        "w_gate": gate_raw.astype(BF16),
    }
    cw, cb = ssm_conv_w[l], ssm_conv_b[l][None, :]
    dtb = ssm_dt_bias[l].reshape(1, nh2)
    alog = ssm_a_log[l].reshape(1, nh2)
    prm["ssd"] = {
        "cwx": cw[:, :INNER], "cbx": cb[:, :INNER], "cwbc": cw[:, INNER:], "cbbc": cb[:, INNER:],
        "dtb_row": _pad_cols(dtb, PAIR), "dtb_col": jnp.broadcast_to(dtb.T, (nh2, SSD_CHUNK)),
        "alog_row": _pad_cols(alog, PAIR), "alog_col": jnp.broadcast_to(alog.T, (nh2, SSD_CHUNK)),
        "dskip": jnp.repeat(ssm_d[l], HEAD)[None, :], "gnorm": ssm_norm[l][None, :],
    }
    prm["att"] = {"qgain": jnp.tile(attn_q_norm[l], 2)[None, :], "kgain": jnp.tile(attn_k_norm[l], 2)[None, :]}
    dcw, dcb = dn_conv_w[l], dn_conv_b[l][None, :]
    zeros16 = jnp.zeros((1, nh2), F32)
    d_alog = jnp.concatenate([zeros16, dn_a_log[l].reshape(1, nh2)], axis=1)
    d_bias = jnp.concatenate([zeros16, dn_dt_bias[l].reshape(1, nh2)], axis=1)
    prm["dn"] = {
        "cwq": dcw[:, :INNER], "cwk": dcw[:, INNER:2 * INNER], "cwv": dcw[:, 2 * INNER:],
        "cbq": dcb[:, :INNER], "cbk": dcb[:, INNER:2 * INNER], "cbv": dcb[:, 2 * INNER:],
        "alog_row": _pad_cols(d_alog, PAIR), "bias_row": _pad_cols(d_bias, PAIR),
        "alog_col": jnp.broadcast_to(d_alog.T, (2 * nh2, ROW_CHUNK)),
        "bias_col": jnp.broadcast_to(d_bias.T, (2 * nh2, ROW_CHUNK)),
        "gnorm": jnp.tile(dn_norm[l], 2)[None, :],
    }
    wa, wi = lru_w_a[l], lru_w_i[l]
    z64 = jnp.zeros((HEAD, HEAD), F32)
    wbd, bias = [], []
    for p in range(N_PAIRS):
        blocks = []
        for wsrc in (wa, wi):
            for d in range(2):
                top = jnp.concatenate([wsrc[d, 2 * p], z64], axis=1)
                bot = jnp.concatenate([z64, wsrc[d, 2 * p + 1]], axis=1)
                blocks.append(jnp.concatenate([top, bot], axis=0))
        wbd.append(jnp.concatenate(blocks, axis=1))
        bias.append(jnp.concatenate([lru_b_a[l][0, p * PAIR:(p + 1) * PAIR], lru_b_a[l][1, p * PAIR:(p + 1) * PAIR],
                                     lru_b_i[l][0, p * PAIR:(p + 1) * PAIR], lru_b_i[l][1, p * PAIR:(p + 1) * PAIR]])[None, :])
    prm["lru"] = {"cw": lru_conv_w[l], "cb": lru_conv_b[l][None, :], "wbd": jnp.stack(wbd).astype(BF16),
                  "bias": jnp.stack(bias), "lam": lru_lambda[l]}
    return prm


def _rope_tables(t):
    n_freq = HEAD // 4
    inv = ROPE_THETA ** (-jnp.arange(n_freq, dtype=F32) / n_freq)
    rows = t // GRID_W
    row = jnp.repeat(jnp.arange(rows, dtype=F32), GRID_W)
    col = jnp.tile(jnp.arange(GRID_W, dtype=F32), rows)
    ang_r = row[:, None] * inv
    ang_c = col[:, None] * inv
    ang = jnp.concatenate([ang_r, ang_r, ang_c, ang_c], axis=1)
    sign = jnp.tile(jnp.concatenate([-jnp.ones((n_freq,), F32), jnp.ones((n_freq,), F32)]), 2)
    cos = jnp.cos(ang)
    sin = jnp.sin(ang) * sign
    return jnp.tile(cos, (1, 2)), jnp.tile(sin, (1, 2))


def _dup_kv(x):
    return jnp.concatenate([x[:, :, 0], x[:, :, 0], x[:, :, 1], x[:, :, 1]], axis=-1)


def _trunk_layer(h, mod, lw, prm, rope_tabs, ctx, fin_gain, *, final):
    h = _ffn(h, mod, lw["norm_ffn1"], lw["ffn1_w13"], lw["ffn1_w2"], fin_gain, rows=(0, 1, 2), final=False)
    gain = lw["norm_mix"]
    emit = ctx is None
    proj, projT = _proj(h, mod, gain, prm["w_proj"], prm["w_projT"])
    if emit:
        o_a, st_ssm = _ssd(proj, projT, prm["ssd"], None, emit_state=True)
        o_b, k_new, v_new = _attn(proj, prm["att"], None, None)
        o_c, st_dn = _dn(proj, projT, prm["dn"], None, emit_state=True)
        o_d, st_lru = _lru(proj, prm["lru"], None, emit_state=True)
        new_ctx = (k_new, v_new, st_ssm, st_dn, st_lru)
    else:
        ck, cv, ssm0, dn0, lru0 = ctx
        o_a, _ = _ssd(proj, projT, prm["ssd"], ssm0, emit_state=False)
        (o_b,) = _attn(proj, prm["att"], rope_tabs, (ck, cv))
        o_c, _ = _dn(proj, projT, prm["dn"], dn0, emit_state=False)
        o_d, _ = _lru(proj, prm["lru"], lru0, emit_state=False)
        new_ctx = None
    h = _merge(h, mod, gain, (o_a, o_b, o_c, o_d), prm["w_gate"], lw["w_branch"], lw["w_out"])
    h = _ffn(h, mod, lw["norm_ffn2"], lw["ffn2_w13"], lw["ffn2_w2"], fin_gain, rows=(6, 7, 8), final=final)
    return h, new_ctx


def kernel(x_prompt, x_sample, cache_k, cache_v, state_ssm, state_delta, state_lru, c, c_ctx,
           w_ada, b_ada, norm_ffn1, ffn1_w13, ffn1_w2, norm_mix, w_in,
           ssm_conv_w, ssm_conv_b, ssm_a_log, ssm_dt_bias, ssm_d, ssm_norm,
           attn_q_norm, attn_k_norm,
           dn_conv_w, dn_conv_b, dn_a_log, dn_dt_bias, dn_norm,
           lru_conv_w, lru_conv_b, lru_w_a, lru_b_a, lru_w_i, lru_b_i, lru_lambda,
           w_branch, w_out, norm_ffn2, ffn2_w13, ffn2_w2, final_norm):
    depth = w_in.shape[0]
    bsz_p, t_p, _ = x_prompt.shape
    bsz_s, t_s, _ = x_sample.shape
    assert bsz_s + 1 <= MOD_ROWS
    cvec = jnp.concatenate([c, c_ctx[None], jnp.zeros((MOD_ROWS - bsz_s - 1, D_MODEL), F32)], axis=0)
    mod_all = _adaln(cvec, w_ada, b_ada)
    mod_all = mod_all.reshape(depth, MOD_ROWS, N_MOD, D_MODEL)
    mod_all = jnp.pad(mod_all, ((0, 0), (0, 0), (0, MOD_ROWS - N_MOD), (0, 0)))
    rope_tabs = _rope_tables(t_s)
    fin_gain = final_norm[None, :]

    hp, hs = x_prompt, x_sample
    ks, vs, ssm_s, dn_s, lru_s = [], [], [], [], []
    for l in range(depth):
        prm = _layer_params(l, w_in, ssm_conv_w, ssm_conv_b, ssm_a_log, ssm_dt_bias, ssm_d, ssm_norm,
                            attn_q_norm, attn_k_norm, dn_conv_w, dn_conv_b, dn_a_log, dn_dt_bias, dn_norm,
                            lru_conv_w, lru_conv_b, lru_w_a, lru_b_a, lru_w_i, lru_b_i, lru_lambda)
        lw = {
            "norm_ffn1": norm_ffn1[l][None, :], "ffn1_w13": ffn1_w13[l].astype(BF16), "ffn1_w2": ffn1_w2[l].astype(BF16),
            "norm_mix": norm_mix[l][None, :], "w_branch": w_branch[l].astype(BF16), "w_out": w_out[l].astype(BF16),
            "norm_ffn2": norm_ffn2[l][None, :], "ffn2_w13": ffn2_w13[l].astype(BF16), "ffn2_w2": ffn2_w2[l].astype(BF16),
        }
        final = l == depth - 1
        mod_lat = mod_all[l, :bsz_s]
        mod_ctx = mod_all[l, bsz_s:bsz_s + 1]
        hp, st = _trunk_layer(hp, mod_ctx, lw, prm, None, None, fin_gain, final=final)
        k_c, v_c, st_ssm, st_dn, st_lru = st
        ks.append(k_c.reshape(bsz_p, t_p, KV_HEADS, HEAD))
        vs.append(v_c.reshape(bsz_p, t_p, KV_HEADS, HEAD))
        ssm_s.append(st_ssm.reshape(bsz_p, 2, N_HEADS, HEAD, HEAD))
        dn_s.append(st_dn.reshape(bsz_p, 2, N_HEADS, HEAD, HEAD))
        lru_s.append(st_lru)
        ctx_l = (_dup_kv(cache_k[:, l]), _dup_kv(cache_v[:, l]),
                 state_ssm[:, l].reshape(bsz_s, 2, INNER, HEAD), state_delta[:, l].reshape(bsz_s, 2, INNER, HEAD),
                 state_lru[:, l])
        hs, _ = _trunk_layer(hs, mod_lat, lw, prm, rope_tabs, ctx_l, fin_gain, final=final)
    return (hp, hs, jnp.stack(ks, axis=1), jnp.stack(vs, axis=1), jnp.stack(ssm_s, axis=1),
            jnp.stack(dn_s, axis=1), jnp.stack(lru_s, axis=1))
```

```python
import functools
import math

import jax
import jax.numpy as jnp
from jax import lax
from jax.experimental import pallas as pl
from jax.experimental.pallas import tpu as pltpu

F32 = jnp.float32
BF16 = jnp.bfloat16

D_MODEL = 1024
D_FF = 2816
N_MOD = 9
MOD_ROWS = 16
EPS = 1e-6
GRID_W = 64
CONV_W = 4
CONV_LP = CONV_W // 2
HALO = 8
HEAD = 64
PAIR = 2 * HEAD
N_HEADS = 8
N_PAIRS = N_HEADS // 2
INNER = N_HEADS * HEAD
KV_HEADS = 2
SSM_GROUPS = 2
ROPE_THETA = 10000.0
LRU_C = 8.0
LOG2E = 1.4426950408889634
N_BRANCH = 4
SSD_CHUNK = 256
DN_CHUNK = 64
DN_BASE = 8
DN_GROUP = 4
ROW_CHUNK = 256
ATT_TQ = 64
ATT_SM_ROWS = 64
FF_CHUNK = 256
VMEM_LIMIT = 56 * 1024 * 1024

COL_Z, COL_XS, COL_QB, COL_QC, COL_KC, COL_VC, COL_GC, COL_XD, COL_YD = range(9)
COL_BC, COL_KK, COL_VV = 18, 19, 20
COL_DT, COL_BA = 42, 43
PROJ_N = 44 * PAIR
ROWT_BA, ROWT_DT = 0, 2
PROJ_T = 48


def _dot(a, b):
    return jnp.dot(a.astype(BF16), b.astype(BF16), preferred_element_type=F32)


def _dot_nt(a, b):
    return lax.dot_general(a.astype(BF16), b.astype(BF16), (((1,), (1,)), ((), ())),
                           preferred_element_type=F32)


def _dot_tn(a, b):
    return lax.dot_general(a.astype(BF16), b.astype(BF16), (((0,), (0,)), ((), ())),
                           preferred_element_type=F32)


def _split3(x):
    hi = x.astype(BF16)
    r1 = x - hi.astype(F32)
    mid = r1.astype(BF16)
    lo = (r1 - mid.astype(F32)).astype(BF16)
    return hi, mid, lo


def _dot_sel(x, sel):
    hi, mid, lo = _split3(x)
    s = sel.astype(BF16)
    return (jnp.dot(hi, s, preferred_element_type=F32) + jnp.dot(mid, s, preferred_element_type=F32)
            + jnp.dot(lo, s, preferred_element_type=F32))


def _sel_dot(sel, x):
    hi, mid, lo = _split3(x)
    s = sel.astype(BF16)
    return (jnp.dot(s, hi, preferred_element_type=F32) + jnp.dot(s, mid, preferred_element_type=F32)
            + jnp.dot(s, lo, preferred_element_type=F32))


def _iota(shape, axis):
    return lax.broadcasted_iota(jnp.int32, shape, axis)


def _silu(x):
    return x * jax.nn.sigmoid(x)


def _softplus(x):
    return jnp.maximum(x, 0.0) + jnp.log1p(jnp.exp(-jnp.abs(x)))


def _rms(x, g):
    return x * lax.rsqrt(jnp.mean(x * x, axis=-1, keepdims=True) + EPS) * g


def _head_ones():
    return (_iota((PAIR, PAIR), 0) // HEAD == _iota((PAIR, PAIR), 1) // HEAD).astype(F32)


def _head_sumsq(x):
    x2 = x * x
    hi = x2.astype(BF16)
    lo = (x2 - hi.astype(F32)).astype(BF16)
    ones = _head_ones().astype(BF16)
    return jnp.dot(hi, ones, preferred_element_type=F32) + jnp.dot(lo, ones, preferred_element_type=F32)


def _lo_tri(n):
    return (_iota((n, n), 1) <= _iota((n, n), 0)).astype(F32)


def _up_tri(n):
    return (_iota((n, n), 1) >= _iota((n, n), 0)).astype(F32)


def _conv_window(ref, r0, c, nchunks, rows, t_total):
    cur = ref[0, pl.ds(r0, rows), :]
    prev_start = pl.multiple_of(jnp.maximum(r0 - HALO, 0), HALO)
    next_start = pl.multiple_of(jnp.minimum(r0 + rows, t_total - HALO), HALO)
    prev = jnp.where(c > 0, ref[0, pl.ds(prev_start, HALO), :], 0.0)
    nxt = jnp.where(c < nchunks - 1, ref[0, pl.ds(next_start, HALO), :], 0.0)
    return jnp.concatenate([prev, cur, nxt], axis=0)


def _dwconv(ext, w, b, rows):
    out = b
    n = ext.shape[0]
    for j in range(CONV_W):
        sh = (CONV_LP - j) % n
        win = ext if sh == 0 else pltpu.roll(ext, sh, 0)
        out = out + w[j:j + 1, :] * win[HALO:HALO + rows, :]
    return out


def _expand_heads(n_src_rows, first, width):
    r = _iota((n_src_rows, width), 0)
    l = _iota((n_src_rows, width), 1)
    return (r - first == l // HEAD).astype(F32)


def _full(shape):
    zeros = (0,) * len(shape)
    return pl.BlockSpec(shape, lambda *_: zeros, pipeline_mode=pl.Buffered(1))


def _layer(shape, l):
    zeros = (0,) * len(shape)
    return pl.BlockSpec((None,) + tuple(shape), lambda *_: (l,) + zeros, pipeline_mode=pl.Buffered(1))


def _params():
    return pltpu.CompilerParams(vmem_limit_bytes=VMEM_LIMIT)


def _adaln_body(c_ref, w_ref, b_ref, o_ref):
    o_ref[0] = _dot(_silu(c_ref[...]), w_ref[0]) + b_ref[0]


def _adaln(cvec, w_ada, b_ada):
    depth = w_ada.shape[0]
    n = w_ada.shape[2]
    tn = n // N_MOD
    return pl.pallas_call(
        _adaln_body,
        out_shape=jax.ShapeDtypeStruct((depth, MOD_ROWS, n), F32),
        grid=(depth, N_MOD),
        in_specs=[pl.BlockSpec((MOD_ROWS, D_MODEL), lambda l, j: (0, 0)),
                  pl.BlockSpec((1, D_MODEL, tn), lambda l, j: (l, 0, j)),
                  pl.BlockSpec((1, 1, tn), lambda l, j: (l, 0, j))],
        out_specs=pl.BlockSpec((1, MOD_ROWS, tn), lambda l, j: (l, 0, j)),
        compiler_params=_params(),
        name="adaln",
    )(cvec, w_ada, b_ada.reshape(depth, 1, n))


def _ffn_body(h_ref, mod_ref, gain_ref, w13_ref, w2_ref, fin_ref, o_ref, acc_sc, *, rows, final):
    shift, scale, gate = rows
    ms = mod_ref[0]
    h = h_ref[0]
    xn = (_rms(h, gain_ref[...]) * (1.0 + ms[scale:scale + 1]) + ms[shift:shift + 1]).astype(BF16)
    for j in range(D_FF // FF_CHUNK):
        lo, hi = j * FF_CHUNK, (j + 1) * FF_CHUNK
        g = jnp.dot(xn, w13_ref[:, lo:hi], preferred_element_type=F32)
        u = jnp.dot(xn, w13_ref[:, D_FF + lo:D_FF + hi], preferred_element_type=F32)
        part = _dot(_silu(g) * u, w2_ref[lo:hi, :])
        if j == 0:
            acc_sc[...] = part
        else:
            acc_sc[...] += part
    hn = h + 0.5 * ms[gate:gate + 1] * acc_sc[...]
    o_ref[0] = _rms(hn, fin_ref[...]) if final else hn


def _ffn(h, mod, gain, w13, w2, fin_gain, *, layer, rows, final):
    bsz, t, d = h.shape
    tm = min(512, t)
    mb = mod.shape[0]
    mod_ix = (lambda b, i: (b, 0, 0)) if mb > 1 else (lambda b, i: (0, 0, 0))
    return pl.pallas_call(
        functools.partial(_ffn_body, rows=rows, final=final),
        out_shape=jax.ShapeDtypeStruct((bsz, t, d), F32),
        grid=(bsz, t // tm),
        in_specs=[pl.BlockSpec((1, tm, d), lambda b, i: (b, i, 0)),
                  pl.BlockSpec((1, MOD_ROWS, d), mod_ix),
                  pl.BlockSpec((1, d), lambda b, i: (0, 0)),
                  _layer((d, 2 * D_FF), layer), _layer((D_FF, d), layer),
                  pl.BlockSpec((1, d), lambda b, i: (0, 0))],
        out_specs=pl.BlockSpec((1, tm, d), lambda b, i: (b, i, 0)),
        scratch_shapes=[pltpu.VMEM((tm, d), F32)],
        compiler_params=_params(),
        name="ffn",
    )(h, mod, gain, w13, w2, fin_gain)


def _proj_body(h_ref, mod_ref, gain_ref, w_ref, wt_ref, o_ref, ot_ref):
    ms = mod_ref[0]
    xn = (_rms(h_ref[0], gain_ref[...]) * (1.0 + ms[4:5]) + ms[3:4]).astype(BF16)
    ot_ref[0] = lax.dot_general(wt_ref[...], xn, (((1,), (1,)), ((), ())), preferred_element_type=F32)
    o_ref[0] = jnp.dot(xn, w_ref[...], preferred_element_type=F32)


def _proj(h, mod, gain, w, wt, *, layer):
    bsz, t, d = h.shape
    n = w.shape[2]
    r = wt.shape[1]
    tm = min(512, t)
    mb = mod.shape[0]
    mod_ix = (lambda b, i: (b, 0, 0)) if mb > 1 else (lambda b, i: (0, 0, 0))
    return pl.pallas_call(
        _proj_body,
        out_shape=[jax.ShapeDtypeStruct((bsz, t, n), F32), jax.ShapeDtypeStruct((bsz, r, t), F32)],
        grid=(bsz, t // tm),
        in_specs=[pl.BlockSpec((1, tm, d), lambda b, i: (b, i, 0)),
                  pl.BlockSpec((1, MOD_ROWS, d), mod_ix),
                  pl.BlockSpec((1, d), lambda b, i: (0, 0)),
                  _layer((d, n), layer), _layer((r, d), layer)],
        out_specs=[pl.BlockSpec((1, tm, n), lambda b, i: (b, i, 0)),
                   pl.BlockSpec((1, r, tm), lambda b, i: (b, 0, i))],
        compiler_params=_params(),
        name="proj",
    )(h, mod, gain, w, wt)


def _merge_body(h_ref, mod_ref, gain_ref, oa_ref, ob_ref, oc_ref, od_ref, wg_ref, wb_ref, wo_ref, o_ref):
    ms = mod_ref[0]
    h = h_ref[0]
    xn = (_rms(h, gain_ref[...]) * (1.0 + ms[4:5]) + ms[3:4]).astype(BF16)
    merged = None
    for n, br_ref in enumerate((oa_ref, ob_ref, oc_ref, od_ref)):
        gate = jax.nn.sigmoid(jnp.dot(xn, wg_ref[:, n * D_MODEL:(n + 1) * D_MODEL], preferred_element_type=F32))
        term = gate * jnp.dot(br_ref[0], wb_ref[n], preferred_element_type=F32)
        merged = term if merged is None else merged + term
    o_ref[0] = h + ms[5:6] * _dot(merged, wo_ref[...])


def _merge(h, mod, gain, branches, w_gate, w_branch, w_out, *, layer):
    bsz, t, d = h.shape
    tm = min(512, t)
    mb = mod.shape[0]
    mod_ix = (lambda b, i: (b, 0, 0)) if mb > 1 else (lambda b, i: (0, 0, 0))
    br_spec = pl.BlockSpec((1, tm, INNER), lambda b, i: (b, i, 0))
    return pl.pallas_call(
        _merge_body,
        out_shape=jax.ShapeDtypeStruct((bsz, t, d), F32),
        grid=(bsz, t // tm),
        in_specs=[pl.BlockSpec((1, tm, d), lambda b, i: (b, i, 0)),
                  pl.BlockSpec((1, MOD_ROWS, d), mod_ix),
                  pl.BlockSpec((1, d), lambda b, i: (0, 0)),
                  br_spec, br_spec, br_spec, br_spec,
                  _layer((d, N_BRANCH * d), layer), _layer((N_BRANCH, INNER, d), layer), _layer((d, d), layer)],
        out_specs=pl.BlockSpec((1, tm, d), lambda b, i: (b, i, 0)),
        compiler_params=_params(),
        name="merge",
    )(h, mod, gain, *branches, w_gate, w_branch, w_out)


def _ssd_body(*refs, t, has_init, emit_state):
    (z_ref, xs_ref, bc_ref, dt_ref, dtT_ref, cwx_ref, cbx_ref, cwbc_ref, cbbc_ref,
     dtb_row_ref, dtb_col_ref, alog_row_ref, alog_col_ref, dskip_ref, gnorm_ref) = refs[:15]
    pos = 15
    h0_ref = None
    if has_init:
        h0_ref = refs[pos]
        pos += 1
    o_ref = refs[pos]
    pos += 1
    st_ref = None
    if emit_state:
        st_ref = refs[pos]
        pos += 1
    y_sc, cum_sc, cm_sc, dh_sc, tot_sc, hst_sc = refs[pos:]

    cl = SSD_CHUNK
    nc = t // cl
    lane = _iota((cl, PAIR), 1)
    ii = _iota((cl, cl), 0)
    jj = _iota((cl, cl), 1)
    lo = _lo_tri(cl)
    up = _up_tri(cl)
    hp = lax.Precision.HIGHEST
    a_row = -jnp.exp(alog_row_ref[...])
    a_col = -jnp.exp(alog_col_ref[...])
    sel_f = _expand_heads(PAIR, 0, INNER)
    sel_b = _expand_heads(PAIR, N_HEADS, INNER)

    def intra(c, carry):
        r0 = pl.multiple_of(c * cl, cl)
        xs = _silu(_dwconv(_conv_window(xs_ref, r0, c, nc, cl, t), cwx_ref[...], cbx_ref[...], cl))
        bc = _silu(_dwconv(_conv_window(bc_ref, r0, c, nc, cl, t), cwbc_ref[...], cbbc_ref[...], cl))
        bm = bc[:, :PAIR]
        cm = bc[:, PAIR:]
        dt = _softplus(dt_ref[0, pl.ds(r0, cl), :] + dtb_row_ref[...])
        dtT = _softplus(dtT_ref[0, :, pl.ds(r0, cl)] + dtb_col_ref[...])
        da = dt * a_row
        daT = dtT * a_col
        cum = jnp.where(lane < N_HEADS, _sel_dot(lo, da), _sel_dot(up, da))
        rowsel = _iota((2 * N_HEADS, cl), 0) < N_HEADS
        cumT = jnp.where(rowsel, _dot_sel(daT, up), _dot_sel(daT, lo))
        cum2 = cum * LOG2E
        cumT2 = cumT * LOG2E
        cum_sc[pl.ds(r0, cl), :] = cum
        cm_sc[pl.ds(r0, cl), :] = cm
        cb = []
        for g in range(SSM_GROUPS):
            cg = jnp.where(lane // HEAD == g, cm, 0.0)
            cb.append(_dot_nt(cg, bm))
        ypairs = []
        for p in range(N_PAIRS):
            xp = xs[:, p * PAIR:(p + 1) * PAIR].astype(BF16)
            halves = []
            for a in range(2):
                h = 2 * p + a
                g = h // (N_HEADS // SSM_GROUPS)
                hb = N_HEADS + h
                expo = jnp.where(jj <= ii, cum2[:, h:h + 1] - cumT2[h:h + 1, :], cum2[:, hb:hb + 1] - cumT2[hb:hb + 1, :])
                dtf = dtT[h:h + 1, :]
                dtb = dtT[hb:hb + 1, :]
                wdt = jnp.where(jj < ii, dtf, jnp.where(jj > ii, dtb, dtf + dtb))
                s = (cb[g] * jnp.exp2(expo) * wdt).astype(BF16)
                halves.append(jnp.dot(s, xp, preferred_element_type=F32))
            ypairs.append(jnp.where(lane < HEAD, halves[0], halves[1]))
        y = jnp.concatenate(ypairs, axis=1) + dskip_ref[...] * xs
        y_sc[pl.ds(r0, cl), :] = y
        tot = jnp.where(lane[0:1] < N_HEADS, cum[cl - 1:cl, :], cum[0:1, :])
        tot_sc[c] = jnp.broadcast_to(tot, (HALO, PAIR))
        wexp = jnp.exp(tot - cum) * dt
        for d, sel in enumerate((sel_f, sel_b)):
            xw = xs * _dot_sel(wexp, sel)
            for p in range(N_PAIRS):
                g = p // (N_PAIRS // SSM_GROUPS)
                bg = jnp.where(lane // HEAD == g, bm, 0.0)
                dh_sc[c, d, p] = _dot_tn(xw[:, p * PAIR:(p + 1) * PAIR], bg)
        return carry

    lax.fori_loop(0, nc, intra, 0)

    rr = _iota((PAIR, PAIR), 0)
    for d in range(2):
        for p in range(N_PAIRS):
            g = p // (N_PAIRS // SSM_GROUPS)
            if has_init:
                blk = h0_ref[0, d, p * PAIR:(p + 1) * PAIR, :]
                z64 = jnp.zeros_like(blk)
                hst_sc[d, p] = jnp.concatenate([blk, z64] if g == 0 else [z64, blk], axis=1)
            else:
                hst_sc[d, p] = jnp.zeros((PAIR, PAIR), F32)

    def inter(k, carry):
        for d in range(2):
            sel = sel_f if d == 0 else sel_b
            c = k if d == 0 else nc - 1 - k
            r0 = pl.multiple_of(c * cl, cl)
            ecum = _dot_sel(jnp.exp(cum_sc[pl.ds(r0, cl), :]), sel)
            cm = cm_sc[pl.ds(r0, cl), :]
            tot = tot_sc[c]
            dec = jnp.exp(tot[0:1, :])
            for p in range(N_PAIRS):
                g = p // (N_PAIRS // SSM_GROUPS)
                cg = jnp.where(lane // HEAD == g, cm, 0.0)
                hs = hst_sc[d, p]
                yi = _dot_nt(cg, hs)
                cols = pl.ds(p * PAIR, PAIR)
                y_sc[pl.ds(r0, cl), cols] = y_sc[pl.ds(r0, cl), cols] + yi * ecum[:, p * PAIR:(p + 1) * PAIR]
                h0i = d * N_HEADS + 2 * p
                dcol = jnp.where(rr < HEAD, dec[:, h0i:h0i + 1], dec[:, h0i + 1:h0i + 2])
                hst_sc[d, p] = hs * dcol + dh_sc[c, d, p]
        return carry

    lax.fori_loop(0, nc, inter, 0)
    if emit_state:
        for d in range(2):
            for p in range(N_PAIRS):
                g = p // (N_PAIRS // SSM_GROUPS)
                st_ref[0, d, p * PAIR:(p + 1) * PAIR, :] = hst_sc[d, p][:, g * HEAD:(g + 1) * HEAD]

    def finish(c, carry):
        r0 = pl.multiple_of(c * cl, cl)
        v = y_sc[pl.ds(r0, cl), :] * _silu(z_ref[0, pl.ds(r0, cl), :])
        o_ref[0, pl.ds(r0, cl), :] = _rms(v, gnorm_ref[...]).astype(BF16)
        return carry

    lax.fori_loop(0, nc, finish, 0)


def _ssd(proj, projT, prm, h0, *, emit_state):
    bsz, t, _ = proj.shape
    nc = t // SSD_CHUNK
    has_init = h0 is not None
    seq = lambda w, j: pl.BlockSpec((1, t, w), lambda b: (b, 0, j))
    in_specs = [seq(INNER, COL_Z), seq(INNER, COL_XS), seq(2 * PAIR, COL_BC), seq(PAIR, COL_DT),
                pl.BlockSpec((1, 2 * N_HEADS, t), lambda b: (b, ROWT_DT, 0))]
    args = [proj, proj, proj, proj, projT]
    for name in ("cwx", "cbx", "cwbc", "cbbc", "dtb_row", "dtb_col", "alog_row", "alog_col", "dskip", "gnorm"):
        a = prm[name]
        in_specs.append(_full(a.shape))
        args.append(a)
    if has_init:
        in_specs.append(pl.BlockSpec((1, 2, INNER, HEAD), lambda b: (b, 0, 0, 0)))
        args.append(h0)
    out_shape = [jax.ShapeDtypeStruct((bsz, t, INNER), BF16)]
    out_specs = [pl.BlockSpec((1, t, INNER), lambda b: (b, 0, 0))]
    if emit_state:
        out_shape.append(jax.ShapeDtypeStruct((bsz, 2, INNER, HEAD), F32))
        out_specs.append(pl.BlockSpec((1, 2, INNER, HEAD), lambda b: (b, 0, 0, 0)))
    res = pl.pallas_call(
        functools.partial(_ssd_body, t=t, has_init=has_init, emit_state=emit_state),
        out_shape=out_shape,
        grid=(bsz,),
        in_specs=in_specs,
        out_specs=out_specs,
        scratch_shapes=[pltpu.VMEM((t, INNER), F32), pltpu.VMEM((t, PAIR), F32), pltpu.VMEM((t, PAIR), F32),
                        pltpu.VMEM((nc, 2, N_PAIRS, PAIR, PAIR), F32), pltpu.VMEM((nc, HALO, PAIR), F32),
                        pltpu.VMEM((2, N_PAIRS, PAIR, PAIR), F32)],
        compiler_params=_params(),
        name="ssd",
    )(*args)
    return (res[0], res[1]) if emit_state else (res[0], None)


def _attn_body(*refs, t, s_ctx, rope):
    q_ref, kk_ref, vv_ref, qg_ref, kg_ref = refs[:5]
    pos = 5
    if rope:
        cos_ref, sin_ref, ck_ref, cv_ref = refs[pos:pos + 4]
        pos += 4
    o_ref = refs[pos]
    pos += 1
    if not rope:
        knew_ref, vnew_ref = refs[pos:pos + 2]
        pos += 2
    q_sc, k_sc, v_sc, s_sc, e_sc, l_sc, m_sc = refs[pos:]

    rc = ROW_CHUNK
    lane = _iota((rc, PAIR), 1)
    first_half = (lane % (HEAD // 2)) < HEAD // 4

    def rot(x, cos, sin):
        partner = jnp.where(first_half, pltpu.roll(x, PAIR - HEAD // 4, 1), pltpu.roll(x, HEAD // 4, 1))
        return x * cos + partner * sin

    def normed(x, gain):
        return x * lax.rsqrt(_head_sumsq(x) * (1.0 / HEAD) + EPS) * gain

    if rope:
        k_sc[0:s_ctx, :] = ck_ref[0].astype(BF16)
        v_sc[0:s_ctx, :] = cv_ref[0].astype(BF16)

    def prep(c, carry):
        r0 = pl.multiple_of(c * rc, rc)
        if rope:
            cos = cos_ref[pl.ds(r0, rc), :]
            sin = sin_ref[pl.ds(r0, rc), :]
        for p in range(N_PAIRS):
            x = normed(q_ref[0, pl.ds(r0, rc), p * PAIR:(p + 1) * PAIR], qg_ref[...])
            if rope:
                x = rot(x, cos, sin)
            q_sc[pl.ds(r0, rc), p * PAIR:(p + 1) * PAIR] = (x * (HEAD ** -0.5 * LOG2E)).astype(BF16)
        kn = []
        for g in range(KV_HEADS):
            x = normed(kk_ref[0, pl.ds(r0, rc), g * PAIR:(g + 1) * PAIR], kg_ref[...])
            kn.append(x)
            if rope:
                x = rot(x, cos, sin)
            k_sc[pl.ds(s_ctx + r0, rc), g * PAIR:(g + 1) * PAIR] = x.astype(BF16)
        vv = vv_ref[0, pl.ds(r0, rc), :]
        v_sc[pl.ds(s_ctx + r0, rc), :] = vv.astype(BF16)
        if not rope:
            knew_ref[0, pl.ds(r0, rc), :] = jnp.where(lane < HEAD, kn[0], kn[1])
            vnew_ref[0, pl.ds(r0, rc), :] = jnp.where(lane < HEAD, vv[:, :PAIR], vv[:, PAIR:])
        return carry

    lax.fori_loop(0, t // rc, prep, 0)

    tq = ATT_TQ
    lane_q = _iota((tq, PAIR), 1)

    pairs_per_group = N_PAIRS // KV_HEADS

    n_tiles = t // tq

    def scores(i, g, dst, m_dst):
        r0 = i * tq if isinstance(i, int) else pl.multiple_of(i * tq, tq)
        tiles = []
        for p in range(g * pairs_per_group, (g + 1) * pairs_per_group):
            qp = q_sc[pl.ds(r0, tq), p * PAIR:(p + 1) * PAIR]
            for a in range(2):
                tiles.append(jnp.where(lane_q // HEAD == a, qp, jnp.zeros_like(qp)))
        qs = jnp.concatenate(tiles, axis=0)
        s = lax.dot_general(qs, k_sc[:, g * PAIR:(g + 1) * PAIR], (((1,), (1,)), ((), ())),
                            preferred_element_type=F32)
        dst[...] = s
        m_dst[...] = jnp.broadcast_to(jnp.max(s, axis=-1, keepdims=True), m_dst.shape)

    def softmax(src, m_src, e_dst, l_dst):
        rows_u, s_all = src.shape
        rb = ATT_SM_ROWS
        for r in range(rows_u // rb):
            rows = slice(r * rb, (r + 1) * rb)
            mb = m_src[rows, :]
            acc = None
            for c in range(s_all // PAIR):
                e = jnp.exp2(src[rows, c * PAIR:(c + 1) * PAIR] - mb)
                acc = e if acc is None else acc + e
                e_dst[rows, c * PAIR:(c + 1) * PAIR] = e.astype(BF16)
            l_dst[rows, :] = jnp.broadcast_to(jnp.sum(acc, axis=-1, keepdims=True), (rb, PAIR))

    def values(i, g, e_src, l_src):
        r0 = i * tq if isinstance(i, int) else pl.multiple_of(i * tq, tq)
        o = jnp.dot(e_src[...], v_sc[:, g * PAIR:(g + 1) * PAIR], preferred_element_type=F32) / l_src[...]
        for j in range(pairs_per_group):
            p = g * pairs_per_group + j
            pair = jnp.where(lane_q < HEAD, o[(2 * j) * tq:(2 * j + 1) * tq], o[(2 * j + 1) * tq:(2 * j + 2) * tq])
            o_ref[0, pl.ds(r0, tq), p * PAIR:(p + 1) * PAIR] = pair.astype(BF16)

    def step(i, slot):
        for g in range(KV_HEADS):
            scores(jnp.minimum(i + 1, n_tiles - 1), g, s_sc.at[1 - slot, g], m_sc.at[1 - slot, g])
            softmax(s_sc.at[slot, g], m_sc.at[slot, g], e_sc.at[slot, g], l_sc.at[slot, g])
            values(jnp.maximum(i - 1, 0), g, e_sc.at[1 - slot, g], l_sc.at[1 - slot, g])

    for g in range(KV_HEADS):
        scores(0, g, s_sc.at[0, g], m_sc.at[0, g])
        e_sc[1, g] = jnp.zeros(e_sc.shape[2:], BF16)
        l_sc[1, g] = jnp.ones(l_sc.shape[2:], F32)

    def attend(j, carry):
        step(2 * j, 0)
        step(2 * j + 1, 1)
        return carry

    assert n_tiles % 2 == 0
    lax.fori_loop(0, n_tiles // 2, attend, 0)
    for g in range(KV_HEADS):
        values(n_tiles - 1, g, e_sc.at[1, g], l_sc.at[1, g])


def _attn(proj, prm, rope_tabs, ctx_kv):
    bsz, t, _ = proj.shape
    rope = rope_tabs is not None
    s_ctx = ctx_kv[0].shape[1] if rope else 0
    s_all = s_ctx + t
    rows_u = (N_HEADS // KV_HEADS) * ATT_TQ
    seq = lambda w, j: pl.BlockSpec((1, t, w), lambda b: (b, 0, j))
    in_specs = [seq(INNER, COL_QB), seq(2 * PAIR, COL_KK), seq(2 * PAIR, COL_VV), _full((1, PAIR)), _full((1, PAIR))]
    args = [proj, proj, proj, prm["qgain"], prm["kgain"]]
    if rope:
        in_specs += [_full((t, PAIR)), _full((t, PAIR)),
                     pl.BlockSpec((1, s_ctx, 2 * PAIR), lambda b: (b, 0, 0)),
                     pl.BlockSpec((1, s_ctx, 2 * PAIR), lambda b: (b, 0, 0))]
        args += [rope_tabs[0], rope_tabs[1], ctx_kv[0], ctx_kv[1]]
    out_shape = [jax.ShapeDtypeStruct((bsz, t, INNER), BF16)]
    out_specs = [pl.BlockSpec((1, t, INNER), lambda b: (b, 0, 0))]
    if not rope:
        out_shape += [jax.ShapeDtypeStruct((bsz, t, PAIR), F32)] * 2
        out_specs += [pl.BlockSpec((1, t, PAIR), lambda b: (b, 0, 0))] * 2
    res = pl.pallas_call(
        functools.partial(_attn_body, t=t, s_ctx=s_ctx, rope=rope),
        out_shape=out_shape,
        grid=(bsz,),
        in_specs=in_specs,
        out_specs=out_specs,
        scratch_shapes=[pltpu.VMEM((t, INNER), BF16), pltpu.VMEM((s_all, 2 * PAIR), BF16),
                        pltpu.VMEM((s_all, 2 * PAIR), BF16),
                        pltpu.VMEM((2, KV_HEADS, rows_u, s_all), F32), pltpu.VMEM((2, KV_HEADS, rows_u, s_all), BF16),
                        pltpu.VMEM((2, KV_HEADS, rows_u, PAIR), F32), pltpu.VMEM((2, KV_HEADS, rows_u, PAIR), F32)],
        compiler_params=_params(),
        name="attn",
    )(*args)
    return res


def _dn_body(*refs, t, has_init, emit_state):
    (q_ref, k_ref, v_ref, gate_ref, ba_ref, baT_ref, cwq_ref, cwk_ref, cwv_ref, cbq_ref, cbk_ref, cbv_ref,
     alog_row_ref, bias_row_ref, alog_col_ref, bias_col_ref, gnorm_ref) = refs[:17]
    pos = 17
    s0_ref = None
    if has_init:
        s0_ref = refs[pos]
        pos += 1
    o_ref = refs[pos]
    pos += 1
    st_ref = None
    if emit_state:
        st_ref = refs[pos]
        pos += 1
    qn_sc, kn_sc, vc_sc, bg_sc, bgT_sc, o_sc, s_sc, hu_sc, hb_sc, hg_sc = refs[pos:]

    rc = ROW_CHUNK
    cl = DN_CHUNK
    nc = t // cl
    nrc = t // rc
    hp = lax.Precision.HIGHEST
    nb = 2 * N_HEADS

    lane_rc = _iota((rc, PAIR), 1)
    row_t = _iota((2 * nb, rc), 0)

    def prep(c, carry):
        r0 = pl.multiple_of(c * rc, rc)
        for src, cw, cb, dst, norm in ((q_ref, cwq_ref, cbq_ref, qn_sc, True), (k_ref, cwk_ref, cbk_ref, kn_sc, True),
                                       (v_ref, cwv_ref, cbv_ref, vc_sc, False)):
            x = _silu(_dwconv(_conv_window(src, r0, c, nrc, rc, t), cw[...], cb[...], rc))
            if norm:
                tiles = []
                for p in range(N_PAIRS):
                    xp = x[:, p * PAIR:(p + 1) * PAIR]
                    tiles.append(xp * lax.rsqrt(_head_sumsq(xp) + EPS))
                x = jnp.concatenate(tiles, axis=1)
                if dst is qn_sc:
                    x = x * (HEAD ** -0.5)
            dst[pl.ds(r0, rc), :] = x
        o_sc[pl.ds(r0, rc), :] = jnp.zeros((rc, INNER), F32)
        ba = ba_ref[0, pl.ds(r0, rc), :]
        beta = jax.nn.sigmoid(ba)
        gdec = -jnp.exp(alog_row_ref[...]) * _softplus(ba + bias_row_ref[...])
        bg_sc[pl.ds(r0, rc), :] = jnp.where(lane_rc < nb, beta, gdec)
        baT = baT_ref[0, :, pl.ds(r0, rc)]
        betaT = jax.nn.sigmoid(baT)
        gdecT = -jnp.exp(alog_col_ref[...]) * _softplus(baT + bias_col_ref[...])
        bgT = jnp.where(row_t < nb, betaT, gdecT)
        for k in range(rc // cl):
            bgT_sc[c * (rc // cl) + k] = bgT[:, k * cl:(k + 1) * cl]
        return carry

    lax.fori_loop(0, nrc, prep, 0)

    rr = _iota((PAIR, PAIR), 0)
    cc = _iota((PAIR, PAIR), 1)
    same = (rr // HEAD) == (cc // HEAD)
    lane_c = _iota((cl, PAIR), 1)
    ii = _iota((cl, PAIR), 0)
    jj = lane_c % HEAD
    first = lane_c < HEAD
    lo = _lo_tri(cl)
    up = _up_tri(cl)
    eye = (ii == jj).astype(F32)
    offdiag = (ii != jj).astype(F32)
    blk_base = ii // DN_BASE == jj // DN_BASE
    off_masks = []
    b = DN_BASE
    while b < cl:
        off_masks.append((ii // (2 * b) == jj // (2 * b)) & (ii // b != jj // b))
        b *= 2

    for d in range(2):
        for p in range(N_PAIRS):
            if has_init:
                b0 = s0_ref[0, d, (2 * p) * HEAD:(2 * p + 1) * HEAD, :]
                b1 = s0_ref[0, d, (2 * p + 1) * HEAD:(2 * p + 2) * HEAD, :]
                z64 = jnp.zeros_like(b0)
                s_sc[d, p] = jnp.concatenate([jnp.concatenate([b0, z64], axis=1),
                                              jnp.concatenate([z64, b1], axis=1)], axis=0)
            else:
                s_sc[d, p] = jnp.zeros((PAIR, PAIR), F32)

    def bdiag(x):
        xb = x.astype(BF16)
        zero = jnp.zeros_like(xb)
        return jnp.concatenate([jnp.where(first, xb, zero), jnp.where(first, zero, xb)], axis=0)

    def pdot(x, y_bd):
        return jnp.dot(x.astype(BF16), y_bd, preferred_element_type=F32)

    def cols2(m, c0):
        return jnp.where(first, m[:, c0:c0 + 1], m[:, c0 + 1:c0 + 2])

    group = min(DN_GROUP, nc)
    systems =[(d, j, p) for j in range(group) for d in range(2) for p in range(N_PAIRS)]

    def chunk_row0(kg, d, j):
        c = kg * group + j if d == 0 else nc - 1 - (kg * group + j)
        return c, (c * cl if isinstance(c, int) else pl.multiple_of(c * cl, cl))

    def solve(kg):
        dirs = {}
        for d in range(2):
            for j in range(group):
                c, r0 = chunk_row0(kg, d, j)
                bg = bg_sc[pl.ds(r0, cl), :]
                bgT = bgT_sc[c]
                tri = lo if d == 0 else up
                triT = up if d == 0 else lo
                gc = _sel_dot(tri, bg)
                gcT = _dot_sel(bgT, triT)
                dirs[d, j] = (r0, bg, gc, gcT)
        st = []
        for d, j, p in systems:
            r0, bg, gc, gcT = dirs[d, j]
            last = cl - 1 if d == 0 else 0
            incl = (jj <= ii) if d == 0 else (jj >= ii)
            h0i = d * N_HEADS + 2 * p
            beta = cols2(bg, h0i)
            gcm = cols2(gc, nb + h0i)
            rgc = jnp.concatenate([gcT[nb + h0i:nb + h0i + 1, :], gcT[nb + h0i + 1:nb + h0i + 2, :]], axis=1)
            glast = jnp.where(first[0:1], gc[last:last + 1, nb + h0i:nb + h0i + 1],
                              gc[last:last + 1, nb + h0i + 1:nb + h0i + 2])
            decay = jnp.exp(jnp.where(incl, gcm - rgc, -jnp.inf))
            cols = pl.ds(p * PAIR, PAIR)
            kp = kn_sc[pl.ds(r0, cl), cols]
            qp = qn_sc[pl.ds(r0, cl), cols]
            vp = vc_sc[pl.ds(r0, cl), cols]
            egc = jnp.exp(gcm)
            kb = kp * beta
            st.append(dict(glast=glast, decay=decay, kp=kp, kb=kb, qp=qp, qg=qp * egc,
                           kd=kp * jnp.exp(glast - gcm), rhs_u=vp * beta, rhs_w=kb * egc))
        yield
        for e in st:
            k_bd = bdiag(e["kp"])
            kq = jnp.concatenate([e["kb"], e["qp"]], axis=0).astype(BF16)
            ga = lax.dot_general(kq, k_bd, (((1,), (1,)), ((), ())), preferred_element_type=F32)
            e["gm"] = ga[:cl]
            e["am"] = ga[cl:]
        yield
        for e in st:
            e["m"] = e["gm"] * e["decay"] * offdiag
            e["aq"] = e["am"] * e["decay"]
            e["n1"] = jnp.where(blk_base, e["m"], 0.0)
        for e in st:
            e["n2"] = pdot(e["n1"], bdiag(e["n1"]))
        yield
        for e in st:
            e["n2_bd"] = bdiag(e["n2"])
            e["n4"] = pdot(e["n2"], e["n2_bd"])
        yield
        for e in st:
            e["pm"] = eye + e["n2"] + e["n4"] + pdot(e["n4"], e["n2_bd"])
        yield
        for e in st:
            e["x"] = e["pm"] - pdot(e["n1"], bdiag(e["pm"]))
        yield
        for off_mask in off_masks:
            for e in st:
                e["cx"] = pdot(jnp.where(off_mask, e["m"], 0.0), bdiag(e["x"]))
            yield
            for e in st:
                e["x"] = e["x"] - pdot(e["x"], bdiag(e["cx"]))
            yield
        for e in st:
            rhs_bd = jnp.concatenate([bdiag(e["rhs_u"]), bdiag(e["rhs_w"])], axis=1)
            e["sol"] = pdot(e["x"], rhs_bd)
        yield
        for idx, e in enumerate(st):
            hu_sc[idx] = e["sol"][:, :PAIR]
            hb_sc[idx, 0] = e["sol"][:, PAIR:].astype(BF16)
            hb_sc[idx, 1] = e["qg"].astype(BF16)
            hb_sc[idx, 2] = e["aq"].astype(BF16)
            hb_sc[idx, 3] = e["kd"].astype(BF16)
            hg_sc[idx] = jnp.broadcast_to(e["glast"], (HALO, PAIR))

    def recur(kg):
        state = {(d, p): s_sc[d, p] for d in range(2) for p in range(N_PAIRS)}
        for j in range(group):
            cur = [(idx, d, p) for idx, (d, jj_, p) in enumerate(systems) if jj_ == j]
            tmp = {}
            for idx, d, p in cur:
                wq = jnp.dot(jnp.concatenate([hb_sc[idx, 0], hb_sc[idx, 1]], axis=0), state[d, p].astype(BF16),
                             preferred_element_type=F32)
                tmp[idx] = (hu_sc[idx] - wq[:cl], wq[cl:])
            yield
            for idx, d, p in cur:
                vnew, qs = tmp[idx]
                _, r0 = chunk_row0(kg, d, j)
                o_sc[pl.ds(r0, cl), pl.ds(p * PAIR, PAIR)] += qs + jnp.dot(hb_sc[idx, 2], bdiag(vnew),
                                                                           preferred_element_type=F32)
                upd = jnp.where(same, _dot_tn(hb_sc[idx, 3], vnew), 0.0)
                glast = hg_sc[idx][0:1, :]
                gl_rows = jnp.where(rr[:, 0:1] < HEAD, glast[:, 0:1], glast[:, HEAD:HEAD + 1])
                state[d, p] = state[d, p] * jnp.exp(gl_rows) + upd
            yield
        for (d, p), s in state.items():
            s_sc[d, p] = s

    def run(*gens):
        gens = list(gens)
        while gens:
            for g in list(gens):
                try:
                    next(g)
                except StopIteration:
                    gens.remove(g)

    n_groups = nc // group
    run(solve(0))

    def chunk(k, carry):
        run(recur(k), solve(k + 1))
        return carry

    lax.fori_loop(0, n_groups - 1, chunk, 0)
    run(recur(n_groups - 1))

    if emit_state:
        for d in range(2):
            for p in range(N_PAIRS):
                s = s_sc[d, p]
                st_ref[0, d, (2 * p) * HEAD:(2 * p + 1) * HEAD, :] = s[:HEAD, :HEAD]
                st_ref[0, d, (2 * p + 1) * HEAD:(2 * p + 2) * HEAD, :] = s[HEAD:, HEAD:]

    def finish(c, carry):
        r0 = pl.multiple_of(c * rc, rc)
        for p in range(N_PAIRS):
            cols = pl.ds(p * PAIR, PAIR)
            o = o_sc[pl.ds(r0, rc), cols]
            on = o * lax.rsqrt(_head_sumsq(o) * (1.0 / HEAD) + EPS) * gnorm_ref[...]
            o_ref[0, pl.ds(r0, rc), cols] = (on * _silu(gate_ref[0, pl.ds(r0, rc), cols])).astype(BF16)
        return carry

    lax.fori_loop(0, nrc, finish, 0)


def _dn(proj, projT, prm, s0, *, emit_state):
    bsz, t, _ = proj.shape
    nc = t // DN_CHUNK
    n_sys = min(DN_GROUP, nc) * 2 * N_PAIRS
    has_init = s0 is not None
    seq = lambda w, j: pl.BlockSpec((1, t, w), lambda b: (b, 0, j), pipeline_mode=pl.Buffered(1))
    in_specs = [seq(INNER, COL_QC), seq(INNER, COL_KC), seq(INNER, COL_VC), seq(INNER, COL_GC), seq(PAIR, COL_BA),
                pl.BlockSpec((1, 4 * N_HEADS, t), lambda b: (b, ROWT_BA, 0), pipeline_mode=pl.Buffered(1))]
    args = [proj] * 5 + [projT]
    for name in ("cwq", "cwk", "cwv", "cbq", "cbk", "cbv", "alog_row", "bias_row", "alog_col", "bias_col", "gnorm"):
        a = prm[name]
        in_specs.append(_full(a.shape))
        args.append(a)
    if has_init:
        in_specs.append(pl.BlockSpec((1, 2, INNER, HEAD), lambda b: (b, 0, 0, 0)))
        args.append(s0)
    out_shape = [jax.ShapeDtypeStruct((bsz, t, INNER), BF16)]
    out_specs = [pl.BlockSpec((1, t, INNER), lambda b: (b, 0, 0))]
    if emit_state:
        out_shape.append(jax.ShapeDtypeStruct((bsz, 2, INNER, HEAD), F32))
        out_specs.append(pl.BlockSpec((1, 2, INNER, HEAD), lambda b: (b, 0, 0, 0)))
    res = pl.pallas_call(
        functools.partial(_dn_body, t=t, has_init=has_init, emit_state=emit_state),
        out_shape=out_shape,
        grid=(bsz,),
        in_specs=in_specs,
        out_specs=out_specs,
        scratch_shapes=[pltpu.VMEM((t, INNER), F32), pltpu.VMEM((t, INNER), F32), pltpu.VMEM((t, INNER), F32),
                        pltpu.VMEM((t, PAIR), F32), pltpu.VMEM((nc, 4 * N_HEADS, DN_CHUNK), F32),
                        pltpu.VMEM((t, INNER), F32),
                        pltpu.VMEM((2, N_PAIRS, PAIR, PAIR), F32),
                        pltpu.VMEM((n_sys, DN_CHUNK, PAIR), F32), pltpu.VMEM((n_sys, 4, DN_CHUNK, PAIR), BF16),
                        pltpu.VMEM((n_sys, HALO, PAIR), F32)],
        compiler_params=_params(),
        name="deltanet",
    )(*args)
    return (res[0], res[1]) if emit_state else (res[0], None)


def _lru_body(*refs, t, has_init, emit_state):
    x_ref, y_ref, cw_ref, cb_ref, wbd_ref, bias_ref, lam_ref = refs[:7]
    pos = 7
    h0_ref = None
    if has_init:
        h0_ref = refs[pos]
        pos += 1
    o_ref = refs[pos]
    pos += 1
    st_ref = None
    if emit_state:
        st_ref = refs[pos]
        pos += 1
    af_sc, uf_sc, ab_sc, ub_sc = refs[pos:]

    rc = ROW_CHUNK
    nrc = t // rc
    row8 = _iota((HALO, INNER), 0)
    sp_lam = _softplus(-lam_ref[...])

    def gates(c, carry):
        r0 = pl.multiple_of(c * rc, rc)
        xl_all = _dwconv(_conv_window(x_ref, r0, c, nrc, rc, t), cw_ref[...], cb_ref[...], rc)
        for p in range(N_PAIRS):
            cols = pl.ds(p * PAIR, PAIR)
            xl = xl_all[:, p * PAIR:(p + 1) * PAIR]
            pre = _dot(xl, wbd_ref[p]) + bias_ref[p]
            for d, (a_sc, u_sc) in enumerate(((af_sc, uf_sc), (ab_sc, ub_sc))):
                r = jax.nn.sigmoid(pre[:, d * PAIR:(d + 1) * PAIR])
                ig = jax.nn.sigmoid(pre[:, (2 + d) * PAIR:(3 + d) * PAIR])
                log_a = -LRU_C * r * sp_lam[d:d + 1, p * PAIR:(p + 1) * PAIR]
                a = jnp.exp(log_a)
                a_sc[pl.ds(r0, rc), cols] = a
                u_sc[pl.ds(r0, rc), cols] = jnp.sqrt((1.0 - a) * (1.0 + a)) * ig * xl
        return carry

    lax.fori_loop(0, nrc, gates, 0)

    if has_init:
        cf0 = h0_ref[0, 0:1, :]
        cb0 = h0_ref[0, 1:2, :]
    else:
        cf0 = jnp.zeros((1, INNER), F32)
        cb0 = cf0

    def scan(k, carry):
        cf, cb = carry
        rf = pl.multiple_of(k * HALO, HALO)
        a8 = af_sc[pl.ds(rf, HALO), :]
        b8 = uf_sc[pl.ds(rf, HALO), :]
        for s in (1, 2, 4):
            ok = row8 >= s
            b8 = jnp.where(ok, a8 * pltpu.roll(b8, s, 0) + b8, b8)
            a8 = jnp.where(ok, a8 * pltpu.roll(a8, s, 0), a8)
        hf = a8 * cf + b8
        uf_sc[pl.ds(rf, HALO), :] = hf
        rb = pl.multiple_of(t - HALO - k * HALO, HALO)
        a8 = ab_sc[pl.ds(rb, HALO), :]
        b8 = ub_sc[pl.ds(rb, HALO), :]
        for s in (1, 2, 4):
            ok = row8 < HALO - s
            b8 = jnp.where(ok, a8 * pltpu.roll(b8, HALO - s, 0) + b8, b8)
            a8 = jnp.where(ok, a8 * pltpu.roll(a8, HALO - s, 0), a8)
        hb = a8 * cb + b8
        ub_sc[pl.ds(rb, HALO), :] = hb
        return hf[HALO - 1:HALO, :], hb[0:1, :]

    cf, cb = lax.fori_loop(0, t // HALO, scan, (cf0, cb0))
    if emit_state:
        st_ref[0, 0:1, :] = cf
        st_ref[0, 1:2, :] = cb

    def finish(c, carry):
        r0 = pl.multiple_of(c * rc, rc)
        y = y_ref[0, pl.ds(r0, rc), :]
        gelu = 0.5 * y * (1.0 + jnp.tanh(math.sqrt(2.0 / math.pi) * (y + 0.044715 * (y * y * y))))
        o_ref[0, pl.ds(r0, rc), :] = ((uf_sc[pl.ds(r0, rc), :] + ub_sc[pl.ds(r0, rc), :]) * gelu).astype(BF16)
        return carry

    lax.fori_loop(0, nrc, finish, 0)


def _lru(proj, prm, h0, *, emit_state):
    bsz, t, _ = proj.shape
    has_init = h0 is not None
    seq = lambda w, j: pl.BlockSpec((1, t, w), lambda b: (b, 0, j))
    in_specs = [seq(INNER, COL_XD), seq(INNER, COL_YD)]
    args = [proj, proj]
    for name in ("cw", "cb", "wbd", "bias", "lam"):
        a = prm[name]
        in_specs.append(_full(a.shape))
        args.append(a)
    if has_init:
        in_specs.append(pl.BlockSpec((1, 2, INNER), lambda b: (b, 0, 0)))
        args.append(h0)
    out_shape = [jax.ShapeDtypeStruct((bsz, t, INNER), BF16)]
    out_specs = [pl.BlockSpec((1, t, INNER), lambda b: (b, 0, 0))]
    if emit_state:
        out_shape.append(jax.ShapeDtypeStruct((bsz, 2, INNER), F32))
        out_specs.append(pl.BlockSpec((1, 2, INNER), lambda b: (b, 0, 0)))
    res = pl.pallas_call(
        functools.partial(_lru_body, t=t, has_init=has_init, emit_state=emit_state),
        out_shape=out_shape,
        grid=(bsz,),
        in_specs=in_specs,
        out_specs=out_specs,
        scratch_shapes=[pltpu.VMEM((t, INNER), F32)] * 4,
        compiler_params=_params(),
        name="rglru",
    )(*args)
    return (res[0], res[1]) if emit_state else (res[0], None)


def _pad_cols(x, n):
    return jnp.pad(x, ((0, 0), (0, n - x.shape[1])))


def _pack_w_in(w_in):
    w = w_in.astype(BF16)
    o = 0

    def take(n):
        nonlocal o
        part = w[:, :, o:o + n]
        o += n
        return part

    z_a, xs_a, bc_a, dt_a = take(INNER), take(INNER), take(2 * PAIR), take(2 * N_HEADS)
    q_b, k_b, v_b = take(INNER), take(PAIR), take(PAIR)
    q_c, k_c, v_c = take(INNER), take(INNER), take(INNER)
    beta_c, a_c, gate_c = take(2 * N_HEADS), take(2 * N_HEADS), take(INNER)
    x_d, y_d = take(INNER), take(INNER)
    gate_raw = take(N_BRANCH * D_MODEL)

    def dup(x):
        return jnp.concatenate([x[..., :HEAD], x[..., :HEAD], x[..., HEAD:], x[..., HEAD:]], axis=-1)

    def pad_last(x, n):
        return jnp.pad(x, ((0, 0), (0, 0), (0, n - x.shape[-1])))

    ba_c = jnp.concatenate([beta_c, a_c], axis=-1)
    w_proj = jnp.concatenate([z_a, xs_a, q_b, q_c, k_c, v_c, gate_c, x_d, y_d, bc_a, dup(k_b), dup(v_b),
                              pad_last(dt_a, PAIR), pad_last(ba_c, PAIR)], axis=-1)
    assert w_proj.shape[-1] == PROJ_N
    w_projT = jnp.swapaxes(jnp.concatenate([ba_c, dt_a], axis=-1), 1, 2)
    return w_proj, w_projT, gate_raw


def _layer_params(l, ssm_conv_w, ssm_conv_b, ssm_a_log, ssm_dt_bias, ssm_d, ssm_norm, attn_q_norm, attn_k_norm,
                  dn_conv_w, dn_conv_b, dn_a_log, dn_dt_bias, dn_norm, lru_conv_w, lru_conv_b, lru_w_a, lru_b_a,
                  lru_w_i, lru_b_i, lru_lambda):
    nh2 = 2 * N_HEADS
    prm = {}
    cw, cb = ssm_conv_w[l], ssm_conv_b[l][None, :]
    dtb = ssm_dt_bias[l].reshape(1, nh2)
    alog = ssm_a_log[l].reshape(1, nh2)
    prm["ssd"] = {
        "cwx": cw[:, :INNER], "cbx": cb[:, :INNER], "cwbc": cw[:, INNER:], "cbbc": cb[:, INNER:],
        "dtb_row": _pad_cols(dtb, PAIR), "dtb_col": jnp.broadcast_to(dtb.T, (nh2, SSD_CHUNK)),
        "alog_row": _pad_cols(alog, PAIR), "alog_col": jnp.broadcast_to(alog.T, (nh2, SSD_CHUNK)),
        "dskip": jnp.repeat(ssm_d[l], HEAD)[None, :], "gnorm": ssm_norm[l][None, :],
    }
    prm["att"] = {"qgain": jnp.tile(attn_q_norm[l], 2)[None, :], "kgain": jnp.tile(attn_k_norm[l], 2)[None, :]}
    dcw, dcb = dn_conv_w[l], dn_conv_b[l][None, :]
    zeros16 = jnp.zeros((1, nh2), F32)
    d_alog = jnp.concatenate([zeros16, dn_a_log[l].reshape(1, nh2)], axis=1)
    d_bias = jnp.concatenate([zeros16, dn_dt_bias[l].reshape(1, nh2)], axis=1)
    prm["dn"] = {
        "cwq": dcw[:, :INNER], "cwk": dcw[:, INNER:2 * INNER], "cwv": dcw[:, 2 * INNER:],
        "cbq": dcb[:, :INNER], "cbk": dcb[:, INNER:2 * INNER], "cbv": dcb[:, 2 * INNER:],
        "alog_row": _pad_cols(d_alog, PAIR), "bias_row": _pad_cols(d_bias, PAIR),
        "alog_col": jnp.broadcast_to(d_alog.T, (2 * nh2, ROW_CHUNK)),
        "bias_col": jnp.broadcast_to(d_bias.T, (2 * nh2, ROW_CHUNK)),
        "gnorm": jnp.tile(dn_norm[l], 2)[None, :],
    }
    wa, wi = lru_w_a[l], lru_w_i[l]
    z64 = jnp.zeros((HEAD, HEAD), F32)
    wbd, bias = [], []
    for p in range(N_PAIRS):
        blocks = []
        for wsrc in (wa, wi):
            for d in range(2):
                top = jnp.concatenate([wsrc[d, 2 * p], z64], axis=1)
                bot = jnp.concatenate([z64, wsrc[d, 2 * p + 1]], axis=1)
                blocks.append(jnp.concatenate([top, bot], axis=0))
        wbd.append(jnp.concatenate(blocks, axis=1))
        bias.append(jnp.concatenate([lru_b_a[l][0, p * PAIR:(p + 1) * PAIR], lru_b_a[l][1, p * PAIR:(p + 1) * PAIR],
                                     lru_b_i[l][0, p * PAIR:(p + 1) * PAIR], lru_b_i[l][1, p * PAIR:(p + 1) * PAIR]])[None, :])
    prm["lru"] = {"cw": lru_conv_w[l], "cb": lru_conv_b[l][None, :], "wbd": jnp.stack(wbd).astype(BF16),
                  "bias": jnp.stack(bias), "lam": lru_lambda[l]}
    return prm


def _rope_tables(t):
    n_freq = HEAD // 4
    inv = ROPE_THETA ** (-jnp.arange(n_freq, dtype=F32) / n_freq)
    rows = t // GRID_W
    row = jnp.repeat(jnp.arange(rows, dtype=F32), GRID_W)
    col = jnp.tile(jnp.arange(GRID_W, dtype=F32), rows)
    ang_r = row[:, None] * inv
    ang_c = col[:, None] * inv
    ang = jnp.concatenate([ang_r, ang_r, ang_c, ang_c], axis=1)
    sign = jnp.tile(jnp.concatenate([-jnp.ones((n_freq,), F32), jnp.ones((n_freq,), F32)]), 2)
    cos = jnp.cos(ang)
    sin = jnp.sin(ang) * sign
    return jnp.tile(cos, (1, 2)), jnp.tile(sin, (1, 2))


def _dup_kv(x):
    return jnp.concatenate([x[:, :, 0], x[:, :, 0], x[:, :, 1], x[:, :, 1]], axis=-1)


def _trunk_layer(h, mod, l, big, lw, prm, rope_tabs, ctx, fin_gain, *, final):
    h = _ffn(h, mod, lw["norm_ffn1"], big["ffn1_w13"], big["ffn1_w2"], fin_gain, layer=l, rows=(0, 1, 2), final=False)
    gain = lw["norm_mix"]
    emit = ctx is None
    proj, projT = _proj(h, mod, gain, big["w_proj"], big["w_projT"], layer=l)
    if emit:
        o_a, st_ssm = _ssd(proj, projT, prm["ssd"], None, emit_state=True)
        o_b, k_new, v_new = _attn(proj, prm["att"], None, None)
        o_c, st_dn = _dn(proj, projT, prm["dn"], None, emit_state=True)
        o_d, st_lru = _lru(proj, prm["lru"], None, emit_state=True)
        new_ctx = (k_new, v_new, st_ssm, st_dn, st_lru)
    else:
        ck, cv, ssm0, dn0, lru0 = ctx
        o_a, _ = _ssd(proj, projT, prm["ssd"], ssm0, emit_state=False)
        (o_b,) = _attn(proj, prm["att"], rope_tabs, (ck, cv))
        o_c, _ = _dn(proj, projT, prm["dn"], dn0, emit_state=False)
        o_d, _ = _lru(proj, prm["lru"], lru0, emit_state=False)
        new_ctx = None
    h = _merge(h, mod, gain, (o_a, o_b, o_c, o_d), big["w_gate"], big["w_branch"], big["w_out"], layer=l)
    h = _ffn(h, mod, lw["norm_ffn2"], big["ffn2_w13"], big["ffn2_w2"], fin_gain, layer=l, rows=(6, 7, 8), final=final)
    return h, new_ctx


def kernel(x_prompt, x_sample, cache_k, cache_v, state_ssm, state_delta, state_lru, c, c_ctx,
           w_ada, b_ada, norm_ffn1, ffn1_w13, ffn1_w2, norm_mix, w_in,
           ssm_conv_w, ssm_conv_b, ssm_a_log, ssm_dt_bias, ssm_d, ssm_norm,
           attn_q_norm, attn_k_norm,
           dn_conv_w, dn_conv_b, dn_a_log, dn_dt_bias, dn_norm,
           lru_conv_w, lru_conv_b, lru_w_a, lru_b_a, lru_w_i, lru_b_i, lru_lambda,
           w_branch, w_out, norm_ffn2, ffn2_w13, ffn2_w2, final_norm):
    depth = w_in.shape[0]
    bsz_p, t_p, _ = x_prompt.shape
    bsz_s, t_s, _ = x_sample.shape
    assert bsz_s + 1 <= MOD_ROWS
    cvec = jnp.concatenate([c, c_ctx[None], jnp.zeros((MOD_ROWS - bsz_s - 1, D_MODEL), F32)], axis=0)
    mod_all = _adaln(cvec, w_ada, b_ada)
    mod_all = mod_all.reshape(depth, MOD_ROWS, N_MOD, D_MODEL)
    mod_all = jnp.pad(mod_all, ((0, 0), (0, 0), (0, MOD_ROWS - N_MOD), (0, 0)))
    rope_tabs = _rope_tables(t_s)
    fin_gain = final_norm[None, :]

    w_proj, w_projT, w_gate = _pack_w_in(w_in)
    big = {
        "ffn1_w13": ffn1_w13.astype(BF16), "ffn1_w2": ffn1_w2.astype(BF16),
        "ffn2_w13": ffn2_w13.astype(BF16), "ffn2_w2": ffn2_w2.astype(BF16),
        "w_proj": w_proj, "w_projT": w_projT, "w_gate": w_gate,
        "w_branch": w_branch.astype(BF16), "w_out": w_out.astype(BF16),
    }

    hp, hs = x_prompt, x_sample
    ks, vs, ssm_s, dn_s, lru_s = [], [], [], [], []
    for l in range(depth):
        prm = _layer_params(l, ssm_conv_w, ssm_conv_b, ssm_a_log, ssm_dt_bias, ssm_d, ssm_norm,
                            attn_q_norm, attn_k_norm, dn_conv_w, dn_conv_b, dn_a_log, dn_dt_bias, dn_norm,
                            lru_conv_w, lru_conv_b, lru_w_a, lru_b_a, lru_w_i, lru_b_i, lru_lambda)
        lw = {"norm_ffn1": norm_ffn1[l][None, :], "norm_mix": norm_mix[l][None, :], "norm_ffn2": norm_ffn2[l][None, :]}
        final = l == depth - 1
        mod_lat = mod_all[l, :bsz_s]
        mod_ctx = mod_all[l, bsz_s:bsz_s + 1]
        hp, st = _trunk_layer(hp, mod_ctx, l, big, lw, prm, None, None, fin_gain, final=final)
        k_c, v_c, st_ssm, st_dn, st_lru = st
        ks.append(k_c.reshape(bsz_p, t_p, KV_HEADS, HEAD))
        vs.append(v_c.reshape(bsz_p, t_p, KV_HEADS, HEAD))
        ssm_s.append(st_ssm.reshape(bsz_p, 2, N_HEADS, HEAD, HEAD))
        dn_s.append(st_dn.reshape(bsz_p, 2, N_HEADS, HEAD, HEAD))
        lru_s.append(st_lru)
        ctx_l = (_dup_kv(cache_k[:, l]), _dup_kv(cache_v[:, l]),
                 state_ssm[:, l].reshape(bsz_s, 2, INNER, HEAD), state_delta[:, l].reshape(bsz_s, 2, INNER, HEAD),
                 state_lru[:, l])
        hs, _ = _trunk_layer(hs, mod_lat, l, big, lw, prm, rope_tabs, ctx_l, fin_gain, final=final)
    return (hp, hs, jnp.stack(ks, axis=1), jnp.stack(vs, axis=1), jnp.stack(ssm_s, axis=1),
            jnp.stack(dn_s, axis=1), jnp.stack(lru_s, axis=1))
```

```python
import functools
import math

import jax
import jax.numpy as jnp
from jax import lax
from jax.experimental import pallas as pl
from jax.experimental.pallas import tpu as pltpu

F32 = jnp.float32
BF16 = jnp.bfloat16

D_MODEL = 1024
D_FF = 2816
N_MOD = 9
MOD_ROWS = 16
EPS = 1e-6
GRID_W = 64
CONV_W = 4
CONV_LP = CONV_W // 2
HALO = 8
HEAD = 64
PAIR = 2 * HEAD
N_HEADS = 8
N_PAIRS = N_HEADS // 2
INNER = N_HEADS * HEAD
KV_HEADS = 2
SSM_GROUPS = 2
ROPE_THETA = 10000.0
LRU_C = 8.0
LOG2E = 1.4426950408889634
N_BRANCH = 4
SSD_CHUNK = 256
DN_CHUNK = 64
DN_BASE = 8
DN_GROUP = 4
ROW_CHUNK = 256
ATT_TQ = 64
ATT_SM_ROWS = 64
FF_CHUNK = 256
VMEM_LIMIT = 56 * 1024 * 1024

COL_Z, COL_XS, COL_QB, COL_QC, COL_KC, COL_VC, COL_GC, COL_XD, COL_YD = range(9)
COL_BC, COL_KK, COL_VV = 18, 19, 20
COL_DT, COL_BA = 42, 43
PROJ_N = 44 * PAIR
ROWT_BA, ROWT_DT = 0, 2
PROJ_T = 48


def _dot(a, b):
    return jnp.dot(a.astype(BF16), b.astype(BF16), preferred_element_type=F32)


def _dot_nt(a, b):
    return lax.dot_general(a.astype(BF16), b.astype(BF16), (((1,), (1,)), ((), ())),
                           preferred_element_type=F32)


def _dot_tn(a, b):
    return lax.dot_general(a.astype(BF16), b.astype(BF16), (((0,), (0,)), ((), ())),
                           preferred_element_type=F32)


def _split3(x):
    hi = x.astype(BF16)
    r1 = x - hi.astype(F32)
    mid = r1.astype(BF16)
    lo = (r1 - mid.astype(F32)).astype(BF16)
    return hi, mid, lo


def _dot_sel(x, sel):
    hi, mid, lo = _split3(x)
    s = sel.astype(BF16)
    return (jnp.dot(hi, s, preferred_element_type=F32) + jnp.dot(mid, s, preferred_element_type=F32)
            + jnp.dot(lo, s, preferred_element_type=F32))


def _sel_dot(sel, x):
    hi, mid, lo = _split3(x)
    s = sel.astype(BF16)
    return (jnp.dot(s, hi, preferred_element_type=F32) + jnp.dot(s, mid, preferred_element_type=F32)
            + jnp.dot(s, lo, preferred_element_type=F32))


def _iota(shape, axis):
    return lax.broadcasted_iota(jnp.int32, shape, axis)


def _silu(x):
    return x * jax.nn.sigmoid(x)


def _softplus(x):
    return jnp.maximum(x, 0.0) + jnp.log1p(jnp.exp(-jnp.abs(x)))


def _rms(x, g):
    return x * lax.rsqrt(jnp.mean(x * x, axis=-1, keepdims=True) + EPS) * g


def _head_ones():
    return (_iota((PAIR, PAIR), 0) // HEAD == _iota((PAIR, PAIR), 1) // HEAD).astype(F32)


def _head_sumsq(x):
    x2 = x * x
    hi = x2.astype(BF16)
    lo = (x2 - hi.astype(F32)).astype(BF16)
    ones = _head_ones().astype(BF16)
    return jnp.dot(hi, ones, preferred_element_type=F32) + jnp.dot(lo, ones, preferred_element_type=F32)


def _lo_tri(n):
    return (_iota((n, n), 1) <= _iota((n, n), 0)).astype(F32)


def _up_tri(n):
    return (_iota((n, n), 1) >= _iota((n, n), 0)).astype(F32)


def _conv_window(ref, r0, c, nchunks, rows, t_total):
    cur = ref[0, pl.ds(r0, rows), :]
    prev_start = pl.multiple_of(jnp.maximum(r0 - HALO, 0), HALO)
    next_start = pl.multiple_of(jnp.minimum(r0 + rows, t_total - HALO), HALO)
    prev = jnp.where(c > 0, ref[0, pl.ds(prev_start, HALO), :], 0.0)
    nxt = jnp.where(c < nchunks - 1, ref[0, pl.ds(next_start, HALO), :], 0.0)
    return jnp.concatenate([prev, cur, nxt], axis=0)


def _dwconv(ext, w, b, rows):
    out = b
    n = ext.shape[0]
    for j in range(CONV_W):
        sh = (CONV_LP - j) % n
        win = ext if sh == 0 else pltpu.roll(ext, sh, 0)
        out = out + w[j:j + 1, :] * win[HALO:HALO + rows, :]
    return out


def _expand_heads(n_src_rows, first, width):
    r = _iota((n_src_rows, width), 0)
    l = _iota((n_src_rows, width), 1)
    return (r - first == l // HEAD).astype(F32)


def _full(shape):
    zeros = (0,) * len(shape)
    return pl.BlockSpec(shape, lambda *_: zeros, pipeline_mode=pl.Buffered(1))


def _layer(shape, l):
    zeros = (0,) * len(shape)
    return pl.BlockSpec((None,) + tuple(shape), lambda *_: (l,) + zeros, pipeline_mode=pl.Buffered(1))


def _params():
    return pltpu.CompilerParams(vmem_limit_bytes=VMEM_LIMIT)


def _adaln_body(c_ref, w_ref, b_ref, o_ref):
    o_ref[0] = _dot(_silu(c_ref[...]), w_ref[0]) + b_ref[0]


def _adaln(cvec, w_ada, b_ada):
    depth = w_ada.shape[0]
    n = w_ada.shape[2]
    steps = 3
    tn = n // steps
    return pl.pallas_call(
        _adaln_body,
        out_shape=jax.ShapeDtypeStruct((depth, MOD_ROWS, n), F32),
        grid=(depth, steps),
        in_specs=[pl.BlockSpec((MOD_ROWS, D_MODEL), lambda l, j: (0, 0)),
                  pl.BlockSpec((1, D_MODEL, tn), lambda l, j: (l, 0, j)),
                  pl.BlockSpec((1, 1, tn), lambda l, j: (l, 0, j))],
        out_specs=pl.BlockSpec((1, MOD_ROWS, tn), lambda l, j: (l, 0, j)),
        compiler_params=_params(),
        name="adaln",
    )(cvec, w_ada, b_ada.reshape(depth, 1, n))


def _ffn_body(h_ref, mod_ref, gain_ref, w13_ref, w2_ref, fin_ref, o_ref, acc_sc, *, rows, final):
    shift, scale, gate = rows
    ms = mod_ref[0]
    h = h_ref[0]
    xn = (_rms(h, gain_ref[...]) * (1.0 + ms[scale:scale + 1]) + ms[shift:shift + 1]).astype(BF16)
    for j in range(D_FF // FF_CHUNK):
        lo, hi = j * FF_CHUNK, (j + 1) * FF_CHUNK
        g = jnp.dot(xn, w13_ref[:, lo:hi], preferred_element_type=F32)
        u = jnp.dot(xn, w13_ref[:, D_FF + lo:D_FF + hi], preferred_element_type=F32)
        part = _dot(_silu(g) * u, w2_ref[lo:hi, :])
        if j == 0:
            acc_sc[...] = part
        else:
            acc_sc[...] += part
    hn = h + 0.5 * ms[gate:gate + 1] * acc_sc[...]
    o_ref[0] = _rms(hn, fin_ref[...]) if final else hn


def _ffn(h, mod, gain, w13, w2, fin_gain, *, layer, rows, final):
    bsz, t, d = h.shape
    tm = min(512, t)
    mb = mod.shape[0]
    mod_ix = (lambda b, i: (b, 0, 0)) if mb > 1 else (lambda b, i: (0, 0, 0))
    return pl.pallas_call(
        functools.partial(_ffn_body, rows=rows, final=final),
        out_shape=jax.ShapeDtypeStruct((bsz, t, d), F32),
        grid=(bsz, t // tm),
        in_specs=[pl.BlockSpec((1, tm, d), lambda b, i: (b, i, 0)),
                  pl.BlockSpec((1, MOD_ROWS, d), mod_ix),
                  pl.BlockSpec((1, d), lambda b, i: (0, 0)),
                  _layer((d, 2 * D_FF), layer), _layer((D_FF, d), layer),
                  pl.BlockSpec((1, d), lambda b, i: (0, 0))],
        out_specs=pl.BlockSpec((1, tm, d), lambda b, i: (b, i, 0)),
        scratch_shapes=[pltpu.VMEM((tm, d), F32)],
        compiler_params=_params(),
        name="ffn",
    )(h, mod, gain, w13, w2, fin_gain)


def _proj_body(h_ref, mod_ref, gain_ref, w_ref, wt_ref, o_ref, ot_ref):
    ms = mod_ref[0]
    xn = (_rms(h_ref[0], gain_ref[...]) * (1.0 + ms[4:5]) + ms[3:4]).astype(BF16)
    ot_ref[0] = lax.dot_general(wt_ref[...], xn, (((1,), (1,)), ((), ())), preferred_element_type=F32)
    o_ref[0] = jnp.dot(xn, w_ref[...], preferred_element_type=F32)


def _proj(h, mod, gain, w, wt, *, layer):
    bsz, t, d = h.shape
    n = w.shape[2]
    r = wt.shape[1]
    tm = min(512, t)
    mb = mod.shape[0]
    mod_ix = (lambda b, i: (b, 0, 0)) if mb > 1 else (lambda b, i: (0, 0, 0))
    return pl.pallas_call(
        _proj_body,
        out_shape=[jax.ShapeDtypeStruct((bsz, t, n), F32), jax.ShapeDtypeStruct((bsz, r, t), F32)],
        grid=(bsz, t // tm),
        in_specs=[pl.BlockSpec((1, tm, d), lambda b, i: (b, i, 0)),
                  pl.BlockSpec((1, MOD_ROWS, d), mod_ix),
                  pl.BlockSpec((1, d), lambda b, i: (0, 0)),
                  _layer((d, n), layer), _layer((r, d), layer)],
        out_specs=[pl.BlockSpec((1, tm, n), lambda b, i: (b, i, 0)),
                   pl.BlockSpec((1, r, tm), lambda b, i: (b, 0, i))],
        compiler_params=_params(),
        name="proj",
    )(h, mod, gain, w, wt)


def _merge_body(h_ref, mod_ref, gain_ref, oa_ref, ob_ref, oc_ref, od_ref, wg_ref, wb_ref, wo_ref, o_ref):
    ms = mod_ref[0]
    h = h_ref[0]
    xn = (_rms(h, gain_ref[...]) * (1.0 + ms[4:5]) + ms[3:4]).astype(BF16)
    merged = None
    for n, br_ref in enumerate((oa_ref, ob_ref, oc_ref, od_ref)):
        gate = jax.nn.sigmoid(jnp.dot(xn, wg_ref[:, n * D_MODEL:(n + 1) * D_MODEL], preferred_element_type=F32))
        term = gate * jnp.dot(br_ref[0], wb_ref[n], preferred_element_type=F32)
        merged = term if merged is None else merged + term
    o_ref[0] = h + ms[5:6] * _dot(merged, wo_ref[...])


def _merge(h, mod, gain, branches, w_gate, w_branch, w_out, *, layer):
    bsz, t, d = h.shape
    tm = min(512, t)
    mb = mod.shape[0]
    mod_ix = (lambda b, i: (b, 0, 0)) if mb > 1 else (lambda b, i: (0, 0, 0))
    br_spec = pl.BlockSpec((1, tm, INNER), lambda b, i: (b, i, 0))
    return pl.pallas_call(
        _merge_body,
        out_shape=jax.ShapeDtypeStruct((bsz, t, d), F32),
        grid=(bsz, t // tm),
        in_specs=[pl.BlockSpec((1, tm, d), lambda b, i: (b, i, 0)),
                  pl.BlockSpec((1, MOD_ROWS, d), mod_ix),
                  pl.BlockSpec((1, d), lambda b, i: (0, 0)),
                  br_spec, br_spec, br_spec, br_spec,
                  _layer((d, N_BRANCH * d), layer), _layer((N_BRANCH, INNER, d), layer), _layer((d, d), layer)],
        out_specs=pl.BlockSpec((1, tm, d), lambda b, i: (b, i, 0)),
        compiler_params=_params(),
        name="merge",
    )(h, mod, gain, *branches, w_gate, w_branch, w_out)


def _ssd_body(*refs, t, has_init, emit_state):
    (z_ref, xs_ref, bc_ref, dt_ref, dtT_ref, cwx_ref, cbx_ref, cwbc_ref, cbbc_ref,
     dtb_row_ref, dtb_col_ref, alog_row_ref, alog_col_ref, dskip_ref, gnorm_ref) = refs[:15]
    pos = 15
    h0_ref = None
    if has_init:
        h0_ref = refs[pos]
        pos += 1
    o_ref = refs[pos]
    pos += 1
    st_ref = None
    if emit_state:
        st_ref = refs[pos]
        pos += 1
    y_sc, cum_sc, cm_sc, dh_sc, tot_sc, hst_sc = refs[pos:]

    cl = SSD_CHUNK
    nc = t // cl
    lane = _iota((cl, PAIR), 1)
    ii = _iota((cl, cl), 0)
    jj = _iota((cl, cl), 1)
    lo = _lo_tri(cl)
    up = _up_tri(cl)
    a_row = -jnp.exp(alog_row_ref[...])
    a_col = -jnp.exp(alog_col_ref[...])
    sel_f = _expand_heads(PAIR, 0, INNER)
    sel_b = _expand_heads(PAIR, N_HEADS, INNER)

    def intra(c, carry):
        r0 = pl.multiple_of(c * cl, cl)
        xs = _silu(_dwconv(_conv_window(xs_ref, r0, c, nc, cl, t), cwx_ref[...], cbx_ref[...], cl))
        bc = _silu(_dwconv(_conv_window(bc_ref, r0, c, nc, cl, t), cwbc_ref[...], cbbc_ref[...], cl))
        bm = bc[:, :PAIR]
        cm = bc[:, PAIR:]
        dt = _softplus(dt_ref[0, pl.ds(r0, cl), :] + dtb_row_ref[...])
        dtT = _softplus(dtT_ref[0, :, pl.ds(r0, cl)] + dtb_col_ref[...])
        da = dt * a_row
        daT = dtT * a_col
        cum = jnp.where(lane < N_HEADS, _sel_dot(lo, da), _sel_dot(up, da))
        rowsel = _iota((2 * N_HEADS, cl), 0) < N_HEADS
        cumT = jnp.where(rowsel, _dot_sel(daT, up), _dot_sel(daT, lo))
        cum2 = cum * LOG2E
        cumT2 = cumT * LOG2E
        cum_sc[pl.ds(r0, cl), :] = cum
        cm_sc[pl.ds(r0, cl), :] = cm
        cb = []
        for g in range(SSM_GROUPS):
            cg = jnp.where(lane // HEAD == g, cm, 0.0)
            cb.append(_dot_nt(cg, bm))
        ypairs = []
        for p in range(N_PAIRS):
            xp = xs[:, p * PAIR:(p + 1) * PAIR].astype(BF16)
            halves = []
            for a in range(2):
                h = 2 * p + a
                g = h // (N_HEADS // SSM_GROUPS)
                hb = N_HEADS + h
                expo = jnp.where(jj <= ii, cum2[:, h:h + 1] - cumT2[h:h + 1, :], cum2[:, hb:hb + 1] - cumT2[hb:hb + 1, :])
                dtf = dtT[h:h + 1, :]
                dtb = dtT[hb:hb + 1, :]
                wdt = jnp.where(jj < ii, dtf, jnp.where(jj > ii, dtb, dtf + dtb))
                s = (cb[g] * jnp.exp2(expo) * wdt).astype(BF16)
                halves.append(jnp.dot(s, xp, preferred_element_type=F32))
            ypairs.append(jnp.where(lane < HEAD, halves[0], halves[1]))
        y = jnp.concatenate(ypairs, axis=1) + dskip_ref[...] * xs
        y_sc[pl.ds(r0, cl), :] = y
        tot = jnp.where(lane[0:1] < N_HEADS, cum[cl - 1:cl, :], cum[0:1, :])
        tot_sc[c] = jnp.broadcast_to(tot, (HALO, PAIR))
        wexp = jnp.exp(tot - cum) * dt
        for d, sel in enumerate((sel_f, sel_b)):
            xw = xs * _dot_sel(wexp, sel)
            for p in range(N_PAIRS):
                g = p // (N_PAIRS // SSM_GROUPS)
                bg = jnp.where(lane // HEAD == g, bm, 0.0)
                dh_sc[c, d, p] = _dot_tn(xw[:, p * PAIR:(p + 1) * PAIR], bg)
        return carry

    lax.fori_loop(0, nc, intra, 0)

    rr = _iota((PAIR, PAIR), 0)
    for d in range(2):
        for p in range(N_PAIRS):
            g = p // (N_PAIRS // SSM_GROUPS)
            if has_init:
                blk = h0_ref[0, d, p * PAIR:(p + 1) * PAIR, :]
                z64 = jnp.zeros_like(blk)
                hst_sc[d, p] = jnp.concatenate([blk, z64] if g == 0 else [z64, blk], axis=1)
            else:
                hst_sc[d, p] = jnp.zeros((PAIR, PAIR), F32)

    def inter(k, carry):
        for d in range(2):
            sel = sel_f if d == 0 else sel_b
            c = k if d == 0 else nc - 1 - k
            r0 = pl.multiple_of(c * cl, cl)
            ecum = _dot_sel(jnp.exp(cum_sc[pl.ds(r0, cl), :]), sel)
            cm = cm_sc[pl.ds(r0, cl), :]
            tot = tot_sc[c]
            dec = jnp.exp(tot[0:1, :])
            for p in range(N_PAIRS):
                g = p // (N_PAIRS // SSM_GROUPS)
                cg = jnp.where(lane // HEAD == g, cm, 0.0)
                hs = hst_sc[d, p]
                yi = _dot_nt(cg, hs)
                cols = pl.ds(p * PAIR, PAIR)
                y_sc[pl.ds(r0, cl), cols] = y_sc[pl.ds(r0, cl), cols] + yi * ecum[:, p * PAIR:(p + 1) * PAIR]
                h0i = d * N_HEADS + 2 * p
                dcol = jnp.where(rr < HEAD, dec[:, h0i:h0i + 1], dec[:, h0i + 1:h0i + 2])
                hst_sc[d, p] = hs * dcol + dh_sc[c, d, p]
        return carry

    lax.fori_loop(0, nc, inter, 0)
    if emit_state:
        for d in range(2):
            for p in range(N_PAIRS):
                g = p // (N_PAIRS // SSM_GROUPS)
                st_ref[0, d, p * PAIR:(p + 1) * PAIR, :] = hst_sc[d, p][:, g * HEAD:(g + 1) * HEAD]

    def finish(c, carry):
        r0 = pl.multiple_of(c * cl, cl)
        v = y_sc[pl.ds(r0, cl), :] * _silu(z_ref[0, pl.ds(r0, cl), :])
        o_ref[0, pl.ds(r0, cl), :] = _rms(v, gnorm_ref[...]).astype(BF16)
        return carry

    lax.fori_loop(0, nc, finish, 0)


def _ssd(proj, projT, prm, h0, *, emit_state):
    bsz, t, _ = proj.shape
    nc = t // SSD_CHUNK
    has_init = h0 is not None
    seq = lambda w, j: pl.BlockSpec((1, t, w), lambda b: (b, 0, j))
    in_specs = [seq(INNER, COL_Z), seq(INNER, COL_XS), seq(2 * PAIR, COL_BC), seq(PAIR, COL_DT),
                pl.BlockSpec((1, 2 * N_HEADS, t), lambda b: (b, ROWT_DT, 0))]
    args = [proj, proj, proj, proj, projT]
    for name in ("cwx", "cbx", "cwbc", "cbbc", "dtb_row", "dtb_col", "alog_row", "alog_col", "dskip", "gnorm"):
        a = prm[name]
        in_specs.append(_full(a.shape))
        args.append(a)
    if has_init:
        in_specs.append(pl.BlockSpec((1, 2, INNER, HEAD), lambda b: (b, 0, 0, 0)))
        args.append(h0)
    out_shape = [jax.ShapeDtypeStruct((bsz, t, INNER), BF16)]
    out_specs = [pl.BlockSpec((1, t, INNER), lambda b: (b, 0, 0))]
    if emit_state:
        out_shape.append(jax.ShapeDtypeStruct((bsz, 2, INNER, HEAD), F32))
        out_specs.append(pl.BlockSpec((1, 2, INNER, HEAD), lambda b: (b, 0, 0, 0)))
    res = pl.pallas_call(
        functools.partial(_ssd_body, t=t, has_init=has_init, emit_state=emit_state),
        out_shape=out_shape,
        grid=(bsz,),
        in_specs=in_specs,
        out_specs=out_specs,
        scratch_shapes=[pltpu.VMEM((t, INNER), F32), pltpu.VMEM((t, PAIR), F32), pltpu.VMEM((t, PAIR), F32),
                        pltpu.VMEM((nc, 2, N_PAIRS, PAIR, PAIR), F32), pltpu.VMEM((nc, HALO, PAIR), F32),
                        pltpu.VMEM((2, N_PAIRS, PAIR, PAIR), F32)],
        compiler_params=_params(),
        name="ssd",
    )(*args)
    return (res[0], res[1]) if emit_state else (res[0], None)


def _attn_body(*refs, t, s_ctx, rope):
    q_ref, kk_ref, vv_ref, qg_ref, kg_ref = refs[:5]
    pos = 5
    if rope:
        cos_ref, sin_ref, ck_ref, cv_ref = refs[pos:pos + 4]
        pos += 4
    o_ref = refs[pos]
    pos += 1
    if not rope:
        knew_ref, vnew_ref = refs[pos:pos + 2]
        pos += 2
    q_sc, k_sc, v_sc, s_sc, e_sc, l_sc, m_sc = refs[pos:]

    rc = ROW_CHUNK
    lane = _iota((rc, PAIR), 1)
    first_half = (lane % (HEAD // 2)) < HEAD // 4

    def rot(x, cos, sin):
        partner = jnp.where(first_half, pltpu.roll(x, PAIR - HEAD // 4, 1), pltpu.roll(x, HEAD // 4, 1))
        return x * cos + partner * sin

    def normed(x, gain):
        return x * lax.rsqrt(_head_sumsq(x) * (1.0 / HEAD) + EPS) * gain

    if rope:
        k_sc[0:s_ctx, :] = ck_ref[0].astype(BF16)
        v_sc[0:s_ctx, :] = cv_ref[0].astype(BF16)

    def prep(c, carry):
        r0 = pl.multiple_of(c * rc, rc)
        if rope:
            cos = cos_ref[pl.ds(r0, rc), :]
            sin = sin_ref[pl.ds(r0, rc), :]
        for p in range(N_PAIRS):
            x = normed(q_ref[0, pl.ds(r0, rc), p * PAIR:(p + 1) * PAIR], qg_ref[...])
            if rope:
                x = rot(x, cos, sin)
            q_sc[pl.ds(r0, rc), p * PAIR:(p + 1) * PAIR] = (x * (HEAD ** -0.5 * LOG2E)).astype(BF16)
        kn = []
        for g in range(KV_HEADS):
            x = normed(kk_ref[0, pl.ds(r0, rc), g * PAIR:(g + 1) * PAIR], kg_ref[...])
            kn.append(x)
            if rope:
                x = rot(x, cos, sin)
            k_sc[pl.ds(s_ctx + r0, rc), g * PAIR:(g + 1) * PAIR] = x.astype(BF16)
        vv = vv_ref[0, pl.ds(r0, rc), :]
        v_sc[pl.ds(s_ctx + r0, rc), :] = vv.astype(BF16)
        if not rope:
            knew_ref[0, pl.ds(r0, rc), :] = jnp.where(lane < HEAD, kn[0], kn[1])
            vnew_ref[0, pl.ds(r0, rc), :] = jnp.where(lane < HEAD, vv[:, :PAIR], vv[:, PAIR:])
        return carry

    lax.fori_loop(0, t // rc, prep, 0)

    tq = ATT_TQ
    lane_q = _iota((tq, PAIR), 1)

    pairs_per_group = N_PAIRS // KV_HEADS

    n_tiles = t // tq

    def scores(i, g, dst, m_dst):
        r0 = i * tq if isinstance(i, int) else pl.multiple_of(i * tq, tq)
        tiles = []
        for p in range(g * pairs_per_group, (g + 1) * pairs_per_group):
            qp = q_sc[pl.ds(r0, tq), p * PAIR:(p + 1) * PAIR]
            for a in range(2):
                tiles.append(jnp.where(lane_q // HEAD == a, qp, jnp.zeros_like(qp)))
        qs = jnp.concatenate(tiles, axis=0)
        s = lax.dot_general(qs, k_sc[:, g * PAIR:(g + 1) * PAIR], (((1,), (1,)), ((), ())),
                            preferred_element_type=F32)
        dst[...] = s
        m_dst[...] = jnp.broadcast_to(jnp.max(s, axis=-1, keepdims=True), m_dst.shape)

    def softmax(src, m_src, e_dst, l_dst):
        rows_u, s_all = src.shape
        rb = ATT_SM_ROWS
        for r in range(rows_u // rb):
            rows = slice(r * rb, (r + 1) * rb)
            mb = m_src[rows, :]
            acc = None
            for c in range(s_all // PAIR):
                e = jnp.exp2(src[rows, c * PAIR:(c + 1) * PAIR] - mb)
                acc = e if acc is None else acc + e
                e_dst[rows, c * PAIR:(c + 1) * PAIR] = e.astype(BF16)
            l_dst[rows, :] = jnp.broadcast_to(jnp.sum(acc, axis=-1, keepdims=True), (rb, PAIR))

    def values(i, g, e_src, l_src):
        r0 = i * tq if isinstance(i, int) else pl.multiple_of(i * tq, tq)
        o = jnp.dot(e_src[...], v_sc[:, g * PAIR:(g + 1) * PAIR], preferred_element_type=F32) / l_src[...]
        for j in range(pairs_per_group):
            p = g * pairs_per_group + j
            pair = jnp.where(lane_q < HEAD, o[(2 * j) * tq:(2 * j + 1) * tq], o[(2 * j + 1) * tq:(2 * j + 2) * tq])
            o_ref[0, pl.ds(r0, tq), p * PAIR:(p + 1) * PAIR] = pair.astype(BF16)

    def step(i, slot):
        for g in range(KV_HEADS):
            scores(jnp.minimum(i + 1, n_tiles - 1), g, s_sc.at[1 - slot, g], m_sc.at[1 - slot, g])
            softmax(s_sc.at[slot, g], m_sc.at[slot, g], e_sc.at[slot, g], l_sc.at[slot, g])
            values(jnp.maximum(i - 1, 0), g, e_sc.at[1 - slot, g], l_sc.at[1 - slot, g])

    for g in range(KV_HEADS):
        scores(0, g, s_sc.at[0, g], m_sc.at[0, g])
        e_sc[1, g] = jnp.zeros(e_sc.shape[2:], BF16)
        l_sc[1, g] = jnp.ones(l_sc.shape[2:], F32)

    def attend(j, carry):
        step(2 * j, 0)
        step(2 * j + 1, 1)
        return carry

    assert n_tiles % 2 == 0
    lax.fori_loop(0, n_tiles // 2, attend, 0)
    for g in range(KV_HEADS):
        values(n_tiles - 1, g, e_sc.at[1, g], l_sc.at[1, g])


def _attn(proj, prm, rope_tabs, ctx_kv):
    bsz, t, _ = proj.shape
    rope = rope_tabs is not None
    s_ctx = ctx_kv[0].shape[1] if rope else 0
    s_all = s_ctx + t
    rows_u = (N_HEADS // KV_HEADS) * ATT_TQ
    seq = lambda w, j: pl.BlockSpec((1, t, w), lambda b: (b, 0, j))
    in_specs = [seq(INNER, COL_QB), seq(2 * PAIR, COL_KK), seq(2 * PAIR, COL_VV), _full((1, PAIR)), _full((1, PAIR))]
    args = [proj, proj, proj, prm["qgain"], prm["kgain"]]
    if rope:
        in_specs += [_full((t, PAIR)), _full((t, PAIR)),
                     pl.BlockSpec((1, s_ctx, 2 * PAIR), lambda b: (b, 0, 0)),
                     pl.BlockSpec((1, s_ctx, 2 * PAIR), lambda b: (b, 0, 0))]
        args += [rope_tabs[0], rope_tabs[1], ctx_kv[0], ctx_kv[1]]
    out_shape = [jax.ShapeDtypeStruct((bsz, t, INNER), BF16)]
    out_specs = [pl.BlockSpec((1, t, INNER), lambda b: (b, 0, 0))]
    if not rope:
        out_shape += [jax.ShapeDtypeStruct((bsz, t, PAIR), F32)] * 2
        out_specs += [pl.BlockSpec((1, t, PAIR), lambda b: (b, 0, 0))] * 2
    res = pl.pallas_call(
        functools.partial(_attn_body, t=t, s_ctx=s_ctx, rope=rope),
        out_shape=out_shape,
        grid=(bsz,),
        in_specs=in_specs,
        out_specs=out_specs,
        scratch_shapes=[pltpu.VMEM((t, INNER), BF16), pltpu.VMEM((s_all, 2 * PAIR), BF16),
                        pltpu.VMEM((s_all, 2 * PAIR), BF16),
                        pltpu.VMEM((2, KV_HEADS, rows_u, s_all), F32), pltpu.VMEM((2, KV_HEADS, rows_u, s_all), BF16),
                        pltpu.VMEM((2, KV_HEADS, rows_u, PAIR), F32), pltpu.VMEM((2, KV_HEADS, rows_u, PAIR), F32)],
        compiler_params=_params(),
        name="attn",
    )(*args)
    return res


def _dn_body(*refs, t, has_init, emit_state):
    (q_ref, k_ref, v_ref, gate_ref, ba_ref, baT_ref, cwq_ref, cwk_ref, cwv_ref, cbq_ref, cbk_ref, cbv_ref,
     alog_row_ref, bias_row_ref, alog_col_ref, bias_col_ref, gnorm_ref) = refs[:17]
    pos = 17
    s0_ref = None
    if has_init:
        s0_ref = refs[pos]
        pos += 1
    o_ref = refs[pos]
    pos += 1
    st_ref = None
    if emit_state:
        st_ref = refs[pos]
        pos += 1
    qn_sc, kn_sc, vc_sc, bg_sc, bgT_sc, o_sc, s_sc, hu_sc, hb_sc, hg_sc = refs[pos:]

    rc = ROW_CHUNK
    cl = DN_CHUNK
    nc = t // cl
    nrc = t // rc
    nb = 2 * N_HEADS

    lane_rc = _iota((rc, PAIR), 1)
    row_t = _iota((2 * nb, rc), 0)

    def prep(c, carry):
        r0 = pl.multiple_of(c * rc, rc)
        for src, cw, cb, dst, norm in ((q_ref, cwq_ref, cbq_ref, qn_sc, True), (k_ref, cwk_ref, cbk_ref, kn_sc, True),
                                       (v_ref, cwv_ref, cbv_ref, vc_sc, False)):
            x = _silu(_dwconv(_conv_window(src, r0, c, nrc, rc, t), cw[...], cb[...], rc))
            if norm:
                tiles = []
                for p in range(N_PAIRS):
                    xp = x[:, p * PAIR:(p + 1) * PAIR]
                    tiles.append(xp * lax.rsqrt(_head_sumsq(xp) + EPS))
                x = jnp.concatenate(tiles, axis=1)
                if dst is qn_sc:
                    x = x * (HEAD ** -0.5)
            dst[pl.ds(r0, rc), :] = x
        o_sc[pl.ds(r0, rc), :] = jnp.zeros((rc, INNER), F32)
        ba = ba_ref[0, pl.ds(r0, rc), :]
        beta = jax.nn.sigmoid(ba)
        gdec = -jnp.exp(alog_row_ref[...]) * _softplus(ba + bias_row_ref[...])
        bg_sc[pl.ds(r0, rc), :] = jnp.where(lane_rc < nb, beta, gdec)
        baT = baT_ref[0, :, pl.ds(r0, rc)]
        betaT = jax.nn.sigmoid(baT)
        gdecT = -jnp.exp(alog_col_ref[...]) * _softplus(baT + bias_col_ref[...])
        bgT = jnp.where(row_t < nb, betaT, gdecT)
        for k in range(rc // cl):
            bgT_sc[c * (rc // cl) + k] = bgT[:, k * cl:(k + 1) * cl]
        return carry

    lax.fori_loop(0, nrc, prep, 0)

    rr = _iota((PAIR, PAIR), 0)
    cc = _iota((PAIR, PAIR), 1)
    same = (rr // HEAD) == (cc // HEAD)
    lane_c = _iota((cl, PAIR), 1)
    ii = _iota((cl, PAIR), 0)
    jj = lane_c % HEAD
    first = lane_c < HEAD
    lo = _lo_tri(cl)
    up = _up_tri(cl)
    eye = (ii == jj).astype(F32)
    offdiag = (ii != jj).astype(F32)
    blk_base = ii // DN_BASE == jj // DN_BASE
    off_masks = []
    b = DN_BASE
    while b < cl:
        off_masks.append((ii // (2 * b) == jj // (2 * b)) & (ii // b != jj // b))
        b *= 2

    for d in range(2):
        for p in range(N_PAIRS):
            if has_init:
                b0 = s0_ref[0, d, (2 * p) * HEAD:(2 * p + 1) * HEAD, :]
                b1 = s0_ref[0, d, (2 * p + 1) * HEAD:(2 * p + 2) * HEAD, :]
                z64 = jnp.zeros_like(b0)
                s_sc[d, p] = jnp.concatenate([jnp.concatenate([b0, z64], axis=1),
                                              jnp.concatenate([z64, b1], axis=1)], axis=0)
            else:
                s_sc[d, p] = jnp.zeros((PAIR, PAIR), F32)

    def bdiag(x):
        xb = x.astype(BF16)
        zero = jnp.zeros_like(xb)
        return jnp.concatenate([jnp.where(first, xb, zero), jnp.where(first, zero, xb)], axis=0)

    def pdot(x, y_bd):
        return jnp.dot(x.astype(BF16), y_bd, preferred_element_type=F32)

    def cols2(m, c0):
        return jnp.where(first, m[:, c0:c0 + 1], m[:, c0 + 1:c0 + 2])

    group = min(DN_GROUP, nc)
    systems =[(d, j, p) for j in range(group) for d in range(2) for p in range(N_PAIRS)]

    def chunk_row0(kg, d, j):
        c = kg * group + j if d == 0 else nc - 1 - (kg * group + j)
        return c, (c * cl if isinstance(c, int) else pl.multiple_of(c * cl, cl))

    def solve(kg):
        dirs = {}
        for d in range(2):
            for j in range(group):
                c, r0 = chunk_row0(kg, d, j)
                bg = bg_sc[pl.ds(r0, cl), :]
                bgT = bgT_sc[c]
                tri = lo if d == 0 else up
                triT = up if d == 0 else lo
                gc = _sel_dot(tri, bg)
                gcT = _dot_sel(bgT, triT)
                dirs[d, j] = (r0, bg, gc, gcT)
        st = []
        for d, j, p in systems:
            r0, bg, gc, gcT = dirs[d, j]
            last = cl - 1 if d == 0 else 0
            incl = (jj <= ii) if d == 0 else (jj >= ii)
            h0i = d * N_HEADS + 2 * p
            beta = cols2(bg, h0i)
            gcm = cols2(gc, nb + h0i)
            rgc = jnp.concatenate([gcT[nb + h0i:nb + h0i + 1, :], gcT[nb + h0i + 1:nb + h0i + 2, :]], axis=1)
            glast = jnp.where(first[0:1], gc[last:last + 1, nb + h0i:nb + h0i + 1],
                              gc[last:last + 1, nb + h0i + 1:nb + h0i + 2])
            decay = jnp.exp(jnp.where(incl, gcm - rgc, -jnp.inf))
            cols = pl.ds(p * PAIR, PAIR)
            kp = kn_sc[pl.ds(r0, cl), cols]
            qp = qn_sc[pl.ds(r0, cl), cols]
            vp = vc_sc[pl.ds(r0, cl), cols]
            egc = jnp.exp(gcm)
            kb = kp * beta
            st.append(dict(glast=glast, decay=decay, kp=kp, kb=kb, qp=qp, qg=qp * egc,
                           kd=kp * jnp.exp(glast - gcm), rhs_u=vp * beta, rhs_w=kb * egc))
        yield
        for e in st:
            k_bd = bdiag(e["kp"])
            kq = jnp.concatenate([e["kb"], e["qp"]], axis=0).astype(BF16)
            ga = lax.dot_general(kq, k_bd, (((1,), (1,)), ((), ())), preferred_element_type=F32)
            e["gm"] = ga[:cl]
            e["am"] = ga[cl:]
        yield
        for e in st:
            e["m"] = e["gm"] * e["decay"] * offdiag
            e["aq"] = e["am"] * e["decay"]
            e["n1"] = jnp.where(blk_base, e["m"], 0.0)
        for e in st:
            e["n2"] = pdot(e["n1"], bdiag(e["n1"]))
        yield
        for e in st:
            e["n2_bd"] = bdiag(e["n2"])
            e["n4"] = pdot(e["n2"], e["n2_bd"])
        yield
        for e in st:
            e["pm"] = eye + e["n2"] + e["n4"] + pdot(e["n4"], e["n2_bd"])
        yield
        for e in st:
            e["x"] = e["pm"] - pdot(e["n1"], bdiag(e["pm"]))
        yield
        for off_mask in off_masks:
            for e in st:
                e["cx"] = pdot(jnp.where(off_mask, e["m"], 0.0), bdiag(e["x"]))
            yield
            for e in st:
                e["x"] = e["x"] - pdot(e["x"], bdiag(e["cx"]))
            yield
        for e in st:
            rhs_bd = jnp.concatenate([bdiag(e["rhs_u"]), bdiag(e["rhs_w"])], axis=1)
            e["sol"] = pdot(e["x"], rhs_bd)
        yield
        for idx, e in enumerate(st):
            hu_sc[idx] = e["sol"][:, :PAIR]
            hb_sc[idx, 0] = e["sol"][:, PAIR:].astype(BF16)
            hb_sc[idx, 1] = e["qg"].astype(BF16)
            hb_sc[idx, 2] = e["aq"].astype(BF16)
            hb_sc[idx, 3] = e["kd"].astype(BF16)
            hg_sc[idx] = jnp.broadcast_to(e["glast"], (HALO, PAIR))

    def recur(kg):
        state = {(d, p): s_sc[d, p] for d in range(2) for p in range(N_PAIRS)}
        for j in range(group):
            cur = [(idx, d, p) for idx, (d, jj_, p) in enumerate(systems) if jj_ == j]
            tmp = {}
            for idx, d, p in cur:
                wq = jnp.dot(jnp.concatenate([hb_sc[idx, 0], hb_sc[idx, 1]], axis=0), state[d, p].astype(BF16),
                             preferred_element_type=F32)
                tmp[idx] = (hu_sc[idx] - wq[:cl], wq[cl:])
            yield
            for idx, d, p in cur:
                vnew, qs = tmp[idx]
                _, r0 = chunk_row0(kg, d, j)
                o_sc[pl.ds(r0, cl), pl.ds(p * PAIR, PAIR)] += qs + jnp.dot(hb_sc[idx, 2], bdiag(vnew),
                                                                           preferred_element_type=F32)
                upd = jnp.where(same, _dot_tn(hb_sc[idx, 3], vnew), 0.0)
                glast = hg_sc[idx][0:1, :]
                gl_rows = jnp.where(rr[:, 0:1] < HEAD, glast[:, 0:1], glast[:, HEAD:HEAD + 1])
                state[d, p] = state[d, p] * jnp.exp(gl_rows) + upd
            yield
        for (d, p), s in state.items():
            s_sc[d, p] = s

    def run(*gens):
        gens = list(gens)
        while gens:
            for g in list(gens):
                try:
                    next(g)
                except StopIteration:
                    gens.remove(g)

    n_groups = nc // group
    run(solve(0))

    def chunk(k, carry):
        run(recur(k), solve(k + 1))
        return carry

    lax.fori_loop(0, n_groups - 1, chunk, 0)
    run(recur(n_groups - 1))

    if emit_state:
        for d in range(2):
            for p in range(N_PAIRS):
                s = s_sc[d, p]
                st_ref[0, d, (2 * p) * HEAD:(2 * p + 1) * HEAD, :] = s[:HEAD, :HEAD]
                st_ref[0, d, (2 * p + 1) * HEAD:(2 * p + 2) * HEAD, :] = s[HEAD:, HEAD:]

    def finish(c, carry):
        r0 = pl.multiple_of(c * rc, rc)
        for p in range(N_PAIRS):
            cols = pl.ds(p * PAIR, PAIR)
            o = o_sc[pl.ds(r0, rc), cols]
            on = o * lax.rsqrt(_head_sumsq(o) * (1.0 / HEAD) + EPS) * gnorm_ref[...]
            o_ref[0, pl.ds(r0, rc), cols] = (on * _silu(gate_ref[0, pl.ds(r0, rc), cols])).astype(BF16)
        return carry

    lax.fori_loop(0, nrc, finish, 0)


def _dn(proj, projT, prm, s0, *, emit_state):
    bsz, t, _ = proj.shape
    nc = t // DN_CHUNK
    n_sys = min(DN_GROUP, nc) * 2 * N_PAIRS
    has_init = s0 is not None
    seq = lambda w, j: pl.BlockSpec((1, t, w), lambda b: (b, 0, j), pipeline_mode=pl.Buffered(1))
    in_specs = [seq(INNER, COL_QC), seq(INNER, COL_KC), seq(INNER, COL_VC), seq(INNER, COL_GC), seq(PAIR, COL_BA),
                pl.BlockSpec((1, 4 * N_HEADS, t), lambda b: (b, ROWT_BA, 0), pipeline_mode=pl.Buffered(1))]
    args = [proj] * 5 + [projT]
    for name in ("cwq", "cwk", "cwv", "cbq", "cbk", "cbv", "alog_row", "bias_row", "alog_col", "bias_col", "gnorm"):
        a = prm[name]
        in_specs.append(_full(a.shape))
        args.append(a)
    if has_init:
        in_specs.append(pl.BlockSpec((1, 2, INNER, HEAD), lambda b: (b, 0, 0, 0)))
        args.append(s0)
    out_shape = [jax.ShapeDtypeStruct((bsz, t, INNER), BF16)]
    out_specs = [pl.BlockSpec((1, t, INNER), lambda b: (b, 0, 0))]
    if emit_state:
        out_shape.append(jax.ShapeDtypeStruct((bsz, 2, INNER, HEAD), F32))
        out_specs.append(pl.BlockSpec((1, 2, INNER, HEAD), lambda b: (b, 0, 0, 0)))
    res = pl.pallas_call(
        functools.partial(_dn_body, t=t, has_init=has_init, emit_state=emit_state),
        out_shape=out_shape,
        grid=(bsz,),
        in_specs=in_specs,
        out_specs=out_specs,
        scratch_shapes=[pltpu.VMEM((t, INNER), F32), pltpu.VMEM((t, INNER), F32), pltpu.VMEM((t, INNER), F32),
                        pltpu.VMEM((t, PAIR), F32), pltpu.VMEM((nc, 4 * N_HEADS, DN_CHUNK), F32),
                        pltpu.VMEM((t, INNER), F32),
                        pltpu.VMEM((2, N_PAIRS, PAIR, PAIR), F32),
                        pltpu.VMEM((n_sys, DN_CHUNK, PAIR), F32), pltpu.VMEM((n_sys, 4, DN_CHUNK, PAIR), BF16),
                        pltpu.VMEM((n_sys, HALO, PAIR), F32)],
        compiler_params=_params(),
        name="deltanet",
    )(*args)
    return (res[0], res[1]) if emit_state else (res[0], None)


def _lru_body(*refs, t, has_init, emit_state):
    x_ref, y_ref, cw_ref, cb_ref, wbd_ref, bias_ref, lam_ref = refs[:7]
    pos = 7
    h0_ref = None
    if has_init:
        h0_ref = refs[pos]
        pos += 1
    o_ref = refs[pos]
    pos += 1
    st_ref = None
    if emit_state:
        st_ref = refs[pos]
        pos += 1
    af_sc, uf_sc, ab_sc, ub_sc = refs[pos:]

    rc = ROW_CHUNK
    nrc = t // rc
    row8 = _iota((HALO, INNER), 0)
    sp_lam = _softplus(-lam_ref[...])

    def gates(c, carry):
        r0 = pl.multiple_of(c * rc, rc)
        xl_all = _dwconv(_conv_window(x_ref, r0, c, nrc, rc, t), cw_ref[...], cb_ref[...], rc)
        for p in range(N_PAIRS):
            cols = pl.ds(p * PAIR, PAIR)
            xl = xl_all[:, p * PAIR:(p + 1) * PAIR]
            pre = _dot(xl, wbd_ref[p]) + bias_ref[p]
            for d, (a_sc, u_sc) in enumerate(((af_sc, uf_sc), (ab_sc, ub_sc))):
                r = jax.nn.sigmoid(pre[:, d * PAIR:(d + 1) * PAIR])
                ig = jax.nn.sigmoid(pre[:, (2 + d) * PAIR:(3 + d) * PAIR])
                log_a = -LRU_C * r * sp_lam[d:d + 1, p * PAIR:(p + 1) * PAIR]
                a = jnp.exp(log_a)
                a_sc[pl.ds(r0, rc), cols] = a
                u_sc[pl.ds(r0, rc), cols] = jnp.sqrt((1.0 - a) * (1.0 + a)) * ig * xl
        return carry

    lax.fori_loop(0, nrc, gates, 0)

    if has_init:
        cf0 = h0_ref[0, 0:1, :]
        cb0 = h0_ref[0, 1:2, :]
    else:
        cf0 = jnp.zeros((1, INNER), F32)
        cb0 = cf0

    def scan(k, carry):
        cf, cb = carry
        rf = pl.multiple_of(k * HALO, HALO)
        a8 = af_sc[pl.ds(rf, HALO), :]
        b8 = uf_sc[pl.ds(rf, HALO), :]
        for s in (1, 2, 4):
            ok = row8 >= s
            b8 = jnp.where(ok, a8 * pltpu.roll(b8, s, 0) + b8, b8)
            a8 = jnp.where(ok, a8 * pltpu.roll(a8, s, 0), a8)
        hf = a8 * cf + b8
        uf_sc[pl.ds(rf, HALO), :] = hf
        rb = pl.multiple_of(t - HALO - k * HALO, HALO)
        a8 = ab_sc[pl.ds(rb, HALO), :]
        b8 = ub_sc[pl.ds(rb, HALO), :]
        for s in (1, 2, 4):
            ok = row8 < HALO - s
            b8 = jnp.where(ok, a8 * pltpu.roll(b8, HALO - s, 0) + b8, b8)
            a8 = jnp.where(ok, a8 * pltpu.roll(a8, HALO - s, 0), a8)
        hb = a8 * cb + b8
        ub_sc[pl.ds(rb, HALO), :] = hb
        return hf[HALO - 1:HALO, :], hb[0:1, :]

    cf, cb = lax.fori_loop(0, t // HALO, scan, (cf0, cb0))
    if emit_state:
        st_ref[0, 0:1, :] = cf
        st_ref[0, 1:2, :] = cb

    def finish(c, carry):
        r0 = pl.multiple_of(c * rc, rc)
        y = y_ref[0, pl.ds(r0, rc), :]
        gelu = 0.5 * y * (1.0 + jnp.tanh(math.sqrt(2.0 / math.pi) * (y + 0.044715 * (y * y * y))))
        o_ref[0, pl.ds(r0, rc), :] = ((uf_sc[pl.ds(r0, rc), :] + ub_sc[pl.ds(r0, rc), :]) * gelu).astype(BF16)
        return carry

    lax.fori_loop(0, nrc, finish, 0)


def _lru(proj, prm, h0, *, emit_state):
    bsz, t, _ = proj.shape
    has_init = h0 is not None
    seq = lambda w, j: pl.BlockSpec((1, t, w), lambda b: (b, 0, j))
    in_specs = [seq(INNER, COL_XD), seq(INNER, COL_YD)]
    args = [proj, proj]
    for name in ("cw", "cb", "wbd", "bias", "lam"):
        a = prm[name]
        in_specs.append(_full(a.shape))
        args.append(a)
    if has_init:
        in_specs.append(pl.BlockSpec((1, 2, INNER), lambda b: (b, 0, 0)))
        args.append(h0)
    out_shape = [jax.ShapeDtypeStruct((bsz, t, INNER), BF16)]
    out_specs = [pl.BlockSpec((1, t, INNER), lambda b: (b, 0, 0))]
    if emit_state:
        out_shape.append(jax.ShapeDtypeStruct((bsz, 2, INNER), F32))
        out_specs.append(pl.BlockSpec((1, 2, INNER), lambda b: (b, 0, 0)))
    res = pl.pallas_call(
        functools.partial(_lru_body, t=t, has_init=has_init, emit_state=emit_state),
        out_shape=out_shape,
        grid=(bsz,),
        in_specs=in_specs,
        out_specs=out_specs,
        scratch_shapes=[pltpu.VMEM((t, INNER), F32)] * 4,
        compiler_params=_params(),
        name="rglru",
    )(*args)
    return (res[0], res[1]) if emit_state else (res[0], None)


def _pad_cols(x, n):
    return jnp.pad(x, ((0, 0), (0, n - x.shape[1])))


def _pack_w_in(w_in):
    w = w_in.astype(BF16)
    o = 0

    def take(n):
        nonlocal o
        part = w[:, :, o:o + n]
        o += n
        return part

    z_a, xs_a, bc_a, dt_a = take(INNER), take(INNER), take(2 * PAIR), take(2 * N_HEADS)
    q_b, k_b, v_b = take(INNER), take(PAIR), take(PAIR)
    q_c, k_c, v_c = take(INNER), take(INNER), take(INNER)
    beta_c, a_c, gate_c = take(2 * N_HEADS), take(2 * N_HEADS), take(INNER)
    x_d, y_d = take(INNER), take(INNER)
    gate_raw = take(N_BRANCH * D_MODEL)

    def dup(x):
        return jnp.concatenate([x[..., :HEAD], x[..., :HEAD], x[..., HEAD:], x[..., HEAD:]], axis=-1)

    def pad_last(x, n):
        return jnp.pad(x, ((0, 0), (0, 0), (0, n - x.shape[-1])))

    ba_c = jnp.concatenate([beta_c, a_c], axis=-1)
    w_proj = jnp.concatenate([z_a, xs_a, q_b, q_c, k_c, v_c, gate_c, x_d, y_d, bc_a, dup(k_b), dup(v_b),
                              pad_last(dt_a, PAIR), pad_last(ba_c, PAIR)], axis=-1)
    assert w_proj.shape[-1] == PROJ_N
    w_projT = jnp.swapaxes(jnp.concatenate([ba_c, dt_a], axis=-1), 1, 2)
    return w_proj, w_projT, gate_raw


def _layer_params(l, ssm_conv_w, ssm_conv_b, ssm_a_log, ssm_dt_bias, ssm_d, ssm_norm, attn_q_norm, attn_k_norm,
                  dn_conv_w, dn_conv_b, dn_a_log, dn_dt_bias, dn_norm, lru_conv_w, lru_conv_b, lru_w_a, lru_b_a,
                  lru_w_i, lru_b_i, lru_lambda):
    nh2 = 2 * N_HEADS
    prm = {}
    cw, cb = ssm_conv_w[l], ssm_conv_b[l][None, :]
    dtb = ssm_dt_bias[l].reshape(1, nh2)
    alog = ssm_a_log[l].reshape(1, nh2)
    prm["ssd"] = {
        "cwx": cw[:, :INNER], "cbx": cb[:, :INNER], "cwbc": cw[:, INNER:], "cbbc": cb[:, INNER:],
        "dtb_row": _pad_cols(dtb, PAIR), "dtb_col": jnp.broadcast_to(dtb.T, (nh2, SSD_CHUNK)),
        "alog_row": _pad_cols(alog, PAIR), "alog_col": jnp.broadcast_to(alog.T, (nh2, SSD_CHUNK)),
        "dskip": jnp.repeat(ssm_d[l], HEAD)[None, :], "gnorm": ssm_norm[l][None, :],
    }
    prm["att"] = {"qgain": jnp.tile(attn_q_norm[l], 2)[None, :], "kgain": jnp.tile(attn_k_norm[l], 2)[None, :]}
    dcw, dcb = dn_conv_w[l], dn_conv_b[l][None, :]
    zeros16 = jnp.zeros((1, nh2), F32)
    d_alog = jnp.concatenate([zeros16, dn_a_log[l].reshape(1, nh2)], axis=1)
    d_bias = jnp.concatenate([zeros16, dn_dt_bias[l].reshape(1, nh2)], axis=1)
    prm["dn"] = {
        "cwq": dcw[:, :INNER], "cwk": dcw[:, INNER:2 * INNER], "cwv": dcw[:, 2 * INNER:],
        "cbq": dcb[:, :INNER], "cbk": dcb[:, INNER:2 * INNER], "cbv": dcb[:, 2 * INNER:],
        "alog_row": _pad_cols(d_alog, PAIR), "bias_row": _pad_cols(d_bias, PAIR),
        "alog_col": jnp.broadcast_to(d_alog.T, (2 * nh2, ROW_CHUNK)),
        "bias_col": jnp.broadcast_to(d_bias.T, (2 * nh2, ROW_CHUNK)),
        "gnorm": jnp.tile(dn_norm[l], 2)[None, :],
    }
    wa, wi = lru_w_a[l], lru_w_i[l]
    z64 = jnp.zeros((HEAD, HEAD), F32)
    wbd, bias = [], []
    for p in range(N_PAIRS):
        blocks = []
        for wsrc in (wa, wi):
            for d in range(2):
                top = jnp.concatenate([wsrc[d, 2 * p], z64], axis=1)
                bot = jnp.concatenate([z64, wsrc[d, 2 * p + 1]], axis=1)
                blocks.append(jnp.concatenate([top, bot], axis=0))
        wbd.append(jnp.concatenate(blocks, axis=1))
        bias.append(jnp.concatenate([lru_b_a[l][0, p * PAIR:(p + 1) * PAIR], lru_b_a[l][1, p * PAIR:(p + 1) * PAIR],
                                     lru_b_i[l][0, p * PAIR:(p + 1) * PAIR], lru_b_i[l][1, p * PAIR:(p + 1) * PAIR]])[None, :])
    prm["lru"] = {"cw": lru_conv_w[l], "cb": lru_conv_b[l][None, :], "wbd": jnp.stack(wbd).astype(BF16),
                  "bias": jnp.stack(bias), "lam": lru_lambda[l]}
    return prm


def _rope_tables(t):
    n_freq = HEAD // 4
    inv = ROPE_THETA ** (-jnp.arange(n_freq, dtype=F32) / n_freq)
    rows = t // GRID_W
    row = jnp.repeat(jnp.arange(rows, dtype=F32), GRID_W)
    col = jnp.tile(jnp.arange(GRID_W, dtype=F32), rows)
    ang_r = row[:, None] * inv
    ang_c = col[:, None] * inv
    ang = jnp.concatenate([ang_r, ang_r, ang_c, ang_c], axis=1)
    sign = jnp.tile(jnp.concatenate([-jnp.ones((n_freq,), F32), jnp.ones((n_freq,), F32)]), 2)
    cos = jnp.cos(ang)
    sin = jnp.sin(ang) * sign
    return jnp.tile(cos, (1, 2)), jnp.tile(sin, (1, 2))


def _dup_kv(x):
    return jnp.concatenate([x[:, :, 0], x[:, :, 0], x[:, :, 1], x[:, :, 1]], axis=-1)


def _trunk_layer(h, mod, l, big, lw, prm, rope_tabs, ctx, fin_gain, *, final):
    h = _ffn(h, mod, lw["norm_ffn1"], big["ffn1_w13"], big["ffn1_w2"], fin_gain, layer=l, rows=(0, 1, 2), final=False)
    gain = lw["norm_mix"]
    emit = ctx is None
    proj, projT = _proj(h, mod, gain, big["w_proj"], big["w_projT"], layer=l)
    if emit:
        o_a, st_ssm = _ssd(proj, projT, prm["ssd"], None, emit_state=True)
        o_b, k_new, v_new = _attn(proj, prm["att"], None, None)
        o_c, st_dn = _dn(proj, projT, prm["dn"], None, emit_state=True)
        o_d, st_lru = _lru(proj, prm["lru"], None, emit_state=True)
        new_ctx = (k_new, v_new, st_ssm, st_dn, st_lru)
    else:
        ck, cv, ssm0, dn0, lru0 = ctx
        o_a, _ = _ssd(proj, projT, prm["ssd"], ssm0, emit_state=False)
        (o_b,) = _attn(proj, prm["att"], rope_tabs, (ck, cv))
        o_c, _ = _dn(proj, projT, prm["dn"], dn0, emit_state=False)
        o_d, _ = _lru(proj, prm["lru"], lru0, emit_state=False)
        new_ctx = None
    h = _merge(h, mod, gain, (o_a, o_b, o_c, o_d), big["w_gate"], big["w_branch"], big["w_out"], layer=l)
    h = _ffn(h, mod, lw["norm_ffn2"], big["ffn2_w13"], big["ffn2_w2"], fin_gain, layer=l, rows=(6, 7, 8), final=final)
    return h, new_ctx


def kernel(x_prompt, x_sample, cache_k, cache_v, state_ssm, state_delta, state_lru, c, c_ctx,
           w_ada, b_ada, norm_ffn1, ffn1_w13, ffn1_w2, norm_mix, w_in,
           ssm_conv_w, ssm_conv_b, ssm_a_log, ssm_dt_bias, ssm_d, ssm_norm,
           attn_q_norm, attn_k_norm,
           dn_conv_w, dn_conv_b, dn_a_log, dn_dt_bias, dn_norm,
           lru_conv_w, lru_conv_b, lru_w_a, lru_b_a, lru_w_i, lru_b_i, lru_lambda,
           w_branch, w_out, norm_ffn2, ffn2_w13, ffn2_w2, final_norm):
    depth = w_in.shape[0]
    bsz_p, t_p, _ = x_prompt.shape
    bsz_s, t_s, _ = x_sample.shape
    assert bsz_s + 1 <= MOD_ROWS
    cvec = jnp.concatenate([c, c_ctx[None], jnp.zeros((MOD_ROWS - bsz_s - 1, D_MODEL), F32)], axis=0)
    mod_all = _adaln(cvec, w_ada, b_ada)
    mod_all = mod_all.reshape(depth, MOD_ROWS, N_MOD, D_MODEL)
    mod_all = jnp.pad(mod_all, ((0, 0), (0, 0), (0, MOD_ROWS - N_MOD), (0, 0)))
    rope_tabs = _rope_tables(t_s)
    fin_gain = final_norm[None, :]

    w_proj, w_projT, w_gate = _pack_w_in(w_in)
    big = {
        "ffn1_w13": ffn1_w13.astype(BF16), "ffn1_w2": ffn1_w2.astype(BF16),
        "ffn2_w13": ffn2_w13.astype(BF16), "ffn2_w2": ffn2_w2.astype(BF16),
        "w_proj": w_proj, "w_projT": w_projT, "w_gate": w_gate,
        "w_branch": w_branch.astype(BF16), "w_out": w_out.astype(BF16),
    }

    hp, hs = x_prompt, x_sample
    ks, vs, ssm_s, dn_s, lru_s = [], [], [], [], []
    for l in range(depth):
        prm = _layer_params(l, ssm_conv_w, ssm_conv_b, ssm_a_log, ssm_dt_bias, ssm_d, ssm_norm,
                            attn_q_norm, attn_k_norm, dn_conv_w, dn_conv_b, dn_a_log, dn_dt_bias, dn_norm,
                            lru_conv_w, lru_conv_b, lru_w_a, lru_b_a, lru_w_i, lru_b_i, lru_lambda)
        lw = {"norm_ffn1": norm_ffn1[l][None, :], "norm_mix": norm_mix[l][None, :], "norm_ffn2": norm_ffn2[l][None, :]}
        final = l == depth - 1
        mod_lat = mod_all[l, :bsz_s]
        mod_ctx = mod_all[l, bsz_s:bsz_s + 1]
        hp, st = _trunk_layer(hp, mod_ctx, l, big, lw, prm, None, None, fin_gain, final=final)
        k_c, v_c, st_ssm, st_dn, st_lru = st
        ks.append(k_c.reshape(bsz_p, t_p, KV_HEADS, HEAD))
        vs.append(v_c.reshape(bsz_p, t_p, KV_HEADS, HEAD))
        ssm_s.append(st_ssm.reshape(bsz_p, 2, N_HEADS, HEAD, HEAD))
        dn_s.append(st_dn.reshape(bsz_p, 2, N_HEADS, HEAD, HEAD))
        lru_s.append(st_lru)
        ctx_l = (_dup_kv(cache_k[:, l]), _dup_kv(cache_v[:, l]),
                 state_ssm[:, l].reshape(bsz_s, 2, INNER, HEAD), state_delta[:, l].reshape(bsz_s, 2, INNER, HEAD),
                 state_lru[:, l])
        hs, _ = _trunk_layer(hs, mod_lat, l, big, lw, prm, rope_tabs, ctx_l, fin_gain, final=final)
    return (hp, hs, jnp.stack(ks, axis=1), jnp.stack(vs, axis=1), jnp.stack(ssm_s, axis=1),
            jnp.stack(dn_s, axis=1), jnp.stack(lru_s, axis=1))
```
